```python
import math
import numpy as np
import jax
import jax.numpy as jnp
from jax import lax

D_MODEL = 2048
BATCH = 4
SEQ = 2048
DEPTH = 4

N_MIXERS = 4
N_HEADS = 16
HEAD_DIM = 128
WIDTH = N_HEADS * HEAD_DIM
N_KV = 4
HPG = N_HEADS // N_KV
KV_WIDTH = N_KV * HEAD_DIM
QBLOCK = 128
REL_BUCKETS = 32
REL_MAX_DIST = 128
CMP_LEN = 32
CMP_STRIDE = 16
SEL_LEN = 64
SEL_TOPK = 8
SEL_QCHUNK = 64
NSA_WINDOW = 512
SWA_WINDOW = 128
FGATE_BIAS_INIT = 2.0
DN_ALPHA = (2 * DEPTH) ** 0.25
DN_BETA = (8 * DEPTH) ** -0.25
LN_EPS = 1e-5
NEG_INF = -1e30
FORCED_SCORE = 1e4

NSA_SPLITS = (WIDTH,) + (KV_WIDTH,) * 6 + (3 * N_HEADS, WIDTH)
SWA_SPLITS = (WIDTH, KV_WIDTH, KV_WIDTH, WIDTH)
SB_SPLITS = (WIDTH, WIDTH, WIDTH, WIDTH)
FOX_SPLITS = (WIDTH, WIDTH, WIDTH, N_HEADS, WIDTH)

kernel_name = 'hybrid_nsa_swa_stickbreaking_fox'


def split_cols(h, sizes):
    cuts = [int(c) for c in np.cumsum(sizes)[:-1]]
    return jnp.split(h, cuts, axis=-1)


def layer_norm(x, g, b):
    xf = x.astype(jnp.float32)
    mu = jnp.mean(xf, axis=-1, keepdims=True)
    var = jnp.mean(jnp.square(xf - mu), axis=-1, keepdims=True)
    return ((xf - mu) * lax.rsqrt(var + LN_EPS) * g + b).astype(x.dtype)


def rel_bucket(dist):
    dist = jnp.maximum(dist, 0)
    max_exact = REL_BUCKETS // 2
    ratio = jnp.maximum(dist, max_exact).astype(jnp.float32) / max_exact
    large = max_exact + (jnp.log(ratio) / math.log(REL_MAX_DIST / max_exact)
                         * (REL_BUCKETS - max_exact)).astype(jnp.int32)
    return jnp.where(dist < max_exact, dist, jnp.minimum(large, REL_BUCKETS - 1))


def masked_softmax(s, mask, sink=None):
    s = jnp.where(mask, s, NEG_INF)
    m = jnp.max(s, axis=-1, keepdims=True)
    if sink is not None:
        m = jnp.maximum(m, sink)
    p = jnp.where(mask, jnp.exp(s - m), 0.0)
    den = jnp.sum(p, axis=-1, keepdims=True)
    if sink is not None:
        den = den + jnp.exp(sink - m)
    return p / jnp.maximum(den, 1e-30)


def banded_gqa(q, k, v, window, rel_bias, sinks=None):
    b, s_len = q.shape[:2]
    nq = s_len // QBLOCK
    kw = QBLOCK + window
    pad = ((0, 0), (window, 0), (0, 0), (0, 0))
    idx = np.arange(nq)[:, None] * QBLOCK + np.arange(kw)[None, :]
    kb = jnp.pad(k, pad)[:, idx]
    vb = jnp.pad(v, pad)[:, idx]
    qr = q.reshape(b, nq, QBLOCK, N_KV, HPG, HEAD_DIM)
    s = jnp.einsum('bnqgrd,bnkgd->bngrqk', qr, kb).astype(jnp.float32) * HEAD_DIM ** -0.5
    qpos = np.arange(nq)[:, None] * QBLOCK + np.arange(QBLOCK)[None, :]
    kpos = idx - window
    dist = (qpos[:, :, None] - kpos[:, None, :]).astype(np.int32)
    mask = (dist >= 0) & (dist < window) & (kpos[:, None, :] >= 0)
    bias = rel_bias[rel_bucket(jnp.asarray(dist))]
    bias = bias.reshape(nq, QBLOCK, kw, N_KV, HPG).transpose(0, 3, 4, 1, 2)
    s = s + bias[None].astype(jnp.float32)
    mask = jnp.asarray(mask)[None, :, None, None]
    sink = None if sinks is None else sinks.astype(jnp.float32).reshape(1, 1, N_KV, HPG, 1, 1)
    p = masked_softmax(s, mask, sink)
    o = jnp.einsum('bngrqk,bnkgd->bnqgrd', p.astype(v.dtype), vb)
    return o.reshape(b, s_len, N_HEADS, HEAD_DIM)


def nsa_compress(kv, pe, w1, w2):
    n_cmp = (kv.shape[1] - CMP_LEN) // CMP_STRIDE + 1
    idx = np.arange(n_cmp)[:, None] * CMP_STRIDE + np.arange(CMP_LEN)[None, :]
    blocks = kv[:, idx] + pe[None, None, :, None, :]
    hid = jax.nn.silu(jnp.einsum('bnlgd,lde->bnge', blocks, w1))
    return jnp.einsum('bnge,ef->bngf', hid, w2)


def mixer_nsa(h, rel_bias, cmp_pe_k, cmp_w1_k, cmp_w2_k, cmp_pe_v, cmp_w1_v, cmp_w2_v):
    b, s_len = h.shape[:2]
    q, kc, vc, ks, vs, kw, vw, gl, z = split_cols(h, NSA_SPLITS)
    q = q.reshape(b, s_len, N_KV, HPG, HEAD_DIM)

    def kvh(t):
        return t.reshape(b, s_len, N_KV, HEAD_DIM)

    scale = HEAD_DIM ** -0.5
    pos = np.arange(s_len)

    k_cmp = nsa_compress(kvh(kc), cmp_pe_k, cmp_w1_k, cmp_w2_k)
    v_cmp = nsa_compress(kvh(vc), cmp_pe_v, cmp_w1_v, cmp_w2_v)
    n_cmp = k_cmp.shape[1]
    blk_end = np.arange(n_cmp) * CMP_STRIDE + CMP_LEN - 1
    dist_c = (pos[:, None] - blk_end[None, :]).astype(np.int32)
    bias_c = rel_bias[rel_bucket(jnp.asarray(dist_c))].reshape(s_len, n_cmp, N_KV, HPG)
    s_c = (jnp.einsum('bsgrd,bngd->bgrsn', q, k_cmp).astype(jnp.float32) * scale
           + bias_c.transpose(2, 3, 0, 1)[None].astype(jnp.float32))
    p_c = masked_softmax(s_c, jnp.asarray(dist_c >= 0)[None, None, None])
    o_cmp = jnp.einsum('bgrsn,bngd->bsgrd', p_c.astype(v_cmp.dtype), v_cmp)
    o_cmp = o_cmp.reshape(b, s_len, N_HEADS, HEAD_DIM)

    nb = s_len // SEL_LEN
    cstart = np.arange(n_cmp) * CMP_STRIDE
    sstart = np.arange(nb) * SEL_LEN
    inter = np.clip(np.minimum(cstart[:, None] + CMP_LEN, sstart[None, :] + SEL_LEN)
                    - np.maximum(cstart[:, None], sstart[None, :]), 0, None) / CMP_LEN
    imp = jnp.einsum('bgrsn,nj->bgsj', p_c, jnp.asarray(inter, jnp.float32))
    blk = np.arange(nb)
    cur = pos // SEL_LEN
    allowed = blk[None, :] * SEL_LEN <= pos[:, None]
    forced = (blk[None, :] == 0) | (blk[None, :] == cur[:, None]) | (blk[None, :] == cur[:, None] - 1)
    imp = jnp.where(jnp.asarray(forced & allowed), FORCED_SCORE,
                    jnp.where(jnp.asarray(allowed), imp, NEG_INF))
    n_pad = max(nb, SEL_TOPK) - nb
    imp = jnp.pad(imp, ((0, 0), (0, 0), (0, 0), (0, n_pad)), constant_values=NEG_INF)
    top_val, top_idx = lax.top_k(imp, SEL_TOPK)
    blk_ok = top_val > NEG_INF / 2
    top_idx = jnp.minimum(top_idx, nb - 1)

    ks_blocks = kvh(ks).reshape(b, nb, SEL_LEN, N_KV, HEAD_DIM).transpose(0, 3, 1, 2, 4)
    vs_blocks = kvh(vs).reshape(b, nb, SEL_LEN, N_KV, HEAD_DIM).transpose(0, 3, 1, 2, 4)
    nc = s_len // SEL_QCHUNK
    q_chunks = q.reshape(b, nc, SEL_QCHUNK, N_KV, HPG, HEAD_DIM).transpose(1, 0, 2, 3, 4, 5)
    idx_chunks = top_idx.reshape(b, N_KV, nc, SEL_QCHUNK, SEL_TOPK).transpose(2, 0, 1, 3, 4)
    ok_chunks = blk_ok.reshape(b, N_KV, nc, SEL_QCHUNK, SEL_TOPK).transpose(2, 0, 1, 3, 4)
    pos_chunks = jnp.arange(s_len, dtype=jnp.int32).reshape(nc, SEL_QCHUNK)
    bi = jnp.arange(b)[:, None, None, None]
    gi = jnp.arange(N_KV)[None, :, None, None]
    tbl = rel_bias.reshape(REL_BUCKETS, N_KV, HPG).transpose(1, 0, 2)

    def sel_chunk(args):
        qc, ic, okc, tc = args
        kg = ks_blocks[bi, gi, ic].reshape(b, N_KV, SEL_QCHUNK, SEL_TOPK * SEL_LEN, HEAD_DIM)
        vg = vs_blocks[bi, gi, ic].reshape(b, N_KV, SEL_QCHUNK, SEL_TOPK * SEL_LEN, HEAD_DIM)
        kpos = (ic[..., None] * SEL_LEN + jnp.arange(SEL_LEN, dtype=jnp.int32)).reshape(
            b, N_KV, SEL_QCHUNK, SEL_TOPK * SEL_LEN)
        dist = tc[None, None, :, None] - kpos
        mask = jnp.repeat(okc, SEL_LEN, axis=-1) & (dist >= 0)
        bias = tbl[gi, rel_bucket(dist)].transpose(0, 1, 4, 2, 3)
        s = jnp.einsum('bqgrd,bgqkd->bgrqk', qc, kg).astype(jnp.float32) * scale + bias.astype(jnp.float32)
        p = masked_softmax(s, mask[:, :, None])
        return jnp.einsum('bgrqk,bgqkd->bqgrd', p.astype(vg.dtype), vg)

    o_sel = lax.map(sel_chunk, (q_chunks, idx_chunks, ok_chunks, pos_chunks))
    o_sel = o_sel.transpose(1, 0, 2, 3, 4, 5).reshape(b, s_len, N_HEADS, HEAD_DIM)

    o_win = banded_gqa(q.reshape(b, s_len, N_HEADS, HEAD_DIM), kvh(kw), kvh(vw), NSA_WINDOW, rel_bias)

    g = jax.nn.sigmoid(gl.astype(jnp.float32)).reshape(b, s_len, N_HEADS, 3).astype(o_win.dtype)
    o = g[..., 0:1] * o_cmp + g[..., 1:2] * o_sel + g[..., 2:3] * o_win
    return o.reshape(b, s_len, WIDTH), z


def mixer_swa_sinks(h, rel_bias, sinks):
    b, s_len = h.shape[:2]
    q, k, v, z = split_cols(h, SWA_SPLITS)
    o = banded_gqa(q.reshape(b, s_len, N_HEADS, HEAD_DIM),
                   k.reshape(b, s_len, N_KV, HEAD_DIM),
                   v.reshape(b, s_len, N_KV, HEAD_DIM),
                   SWA_WINDOW, rel_bias, sinks)
    return o.reshape(b, s_len, WIDTH), z


def mixer_stick_breaking(h):
    b, s_len = h.shape[:2]
    q, k, v, z = split_cols(h, SB_SPLITS)
    q, k, v = (t.reshape(b, s_len, N_HEADS, HEAD_DIM) for t in (q, k, v))
    scale = HEAD_DIM ** -0.5
    outs = []
    for i in range(s_len // QBLOCK):
        lo, hi = i * QBLOCK, (i + 1) * QBLOCK
        zl = jnp.einsum('bqhd,bkhd->bhqk', q[:, lo:hi], k[:, :hi]).astype(jnp.float32) * scale
        strict = jnp.asarray(np.arange(hi)[None, :] < np.arange(lo, hi)[:, None])
        log_keep = jnp.where(strict, jax.nn.log_sigmoid(-zl), 0.0)
        after = lax.cumsum(log_keep, axis=3, reverse=True) - log_keep
        a = jnp.where(strict, jnp.exp(jax.nn.log_sigmoid(zl) + after), 0.0)
        outs.append(jnp.einsum('bhqk,bkhd->bqhd', a.astype(v.dtype), v[:, :hi]))
    o = jnp.concatenate(outs, axis=1)
    return o.reshape(b, s_len, WIDTH), z


def mixer_forgetting(h, fgate_bias):
    b, s_len = h.shape[:2]
    q, k, v, fl, z = split_cols(h, FOX_SPLITS)
    q, k, v = (t.reshape(b, s_len, N_HEADS, HEAD_DIM) for t in (q, k, v))
    log_f = jax.nn.log_sigmoid((fl + fgate_bias).astype(jnp.float32))
    c = jnp.cumsum(log_f, axis=1).transpose(0, 2, 1)
    scale = HEAD_DIM ** -0.5
    outs = []
    for i in range(s_len // QBLOCK):
        lo, hi = i * QBLOCK, (i + 1) * QBLOCK
        s = (jnp.einsum('bqhd,bkhd->bhqk', q[:, lo:hi], k[:, :hi]).astype(jnp.float32) * scale
             + c[:, :, lo:hi, None] - c[:, :, None, :hi])
        causal = jnp.asarray(np.arange(hi)[None, :] <= np.arange(lo, hi)[:, None])
        p = masked_softmax(s, causal)
        outs.append(jnp.einsum('bhqk,bkhd->bqhd', p.astype(v.dtype), v[:, :hi]))
    o = jnp.concatenate(outs, axis=1)
    return o.reshape(b, s_len, WIDTH), z


def setup_inputs(seed: int = 0) -> dict:
    key = jax.random.key(seed)
    keys = jax.random.split(key, 32)
    count = [0]

    def normal(shape, scale):
        k = keys[count[0]]
        count[0] += 1
        return jax.random.normal(k, shape, jnp.float32) * scale

    w_in_scale = D_MODEL ** -0.5
    w_out_scale = DN_BETA * WIDTH ** -0.5
    inp = {}
    inp['x'] = normal((BATCH, SEQ, D_MODEL), 1.0)
    inp['rel_bias'] = normal((REL_BUCKETS, N_HEADS), 0.5)
    inp['w_in_a'] = normal((D_MODEL, sum(NSA_SPLITS)), w_in_scale)
    inp['w_out_a'] = normal((WIDTH, D_MODEL), w_out_scale)
    inp['ln_g_a'] = 1.0 + normal((D_MODEL,), 0.02)
    inp['ln_b_a'] = normal((D_MODEL,), 0.02)
    inp['cmp_pe_k'] = normal((CMP_LEN, HEAD_DIM), 0.5)
    inp['cmp_w1_k'] = normal((CMP_LEN, HEAD_DIM, HEAD_DIM), (CMP_LEN * HEAD_DIM) ** -0.5)
    inp['cmp_w2_k'] = normal((HEAD_DIM, HEAD_DIM), HEAD_DIM ** -0.5)
    inp['cmp_pe_v'] = normal((CMP_LEN, HEAD_DIM), 0.5)
    inp['cmp_w1_v'] = normal((CMP_LEN, HEAD_DIM, HEAD_DIM), (CMP_LEN * HEAD_DIM) ** -0.5)
    inp['cmp_w2_v'] = normal((HEAD_DIM, HEAD_DIM), HEAD_DIM ** -0.5)
    inp['w_in_b'] = normal((D_MODEL, sum(SWA_SPLITS)), w_in_scale)
    inp['w_out_b'] = normal((WIDTH, D_MODEL), w_out_scale)
    inp['ln_g_b'] = 1.0 + normal((D_MODEL,), 0.02)
    inp['ln_b_b'] = normal((D_MODEL,), 0.02)
    inp['sinks_b'] = normal((N_HEADS,), 0.5)
    inp['w_in_c'] = normal((D_MODEL, sum(SB_SPLITS)), w_in_scale)
    inp['w_out_c'] = normal((WIDTH, D_MODEL), w_out_scale)
    inp['ln_g_c'] = 1.0 + normal((D_MODEL,), 0.02)
    inp['ln_b_c'] = normal((D_MODEL,), 0.02)
    inp['w_in_d'] = normal((D_MODEL, sum(FOX_SPLITS)), w_in_scale)
    inp['w_out_d'] = normal((WIDTH, D_MODEL), w_out_scale)
    inp['ln_g_d'] = 1.0 + normal((D_MODEL,), 0.02)
    inp['ln_b_d'] = normal((D_MODEL,), 0.02)
    inp['fgate_bias_d'] = FGATE_BIAS_INIT + normal((N_HEADS,), 0.5)
    return inp


def reference(x, rel_bias,
              w_in_a, w_out_a, ln_g_a, ln_b_a,
              cmp_pe_k, cmp_w1_k, cmp_w2_k, cmp_pe_v, cmp_w1_v, cmp_w2_v,
              w_in_b, w_out_b, ln_g_b, ln_b_b, sinks_b,
              w_in_c, w_out_c, ln_g_c, ln_b_c,
              w_in_d, w_out_d, ln_g_d, ln_b_d, fgate_bias_d):
    mixers = (
        lambda h: mixer_nsa(h, rel_bias, cmp_pe_k, cmp_w1_k, cmp_w2_k, cmp_pe_v, cmp_w1_v, cmp_w2_v),
        lambda h: mixer_swa_sinks(h, rel_bias, sinks_b),
        mixer_stick_breaking,
        lambda h: mixer_forgetting(h, fgate_bias_d),
    )
    w_in = (w_in_a, w_in_b, w_in_c, w_in_d)
    w_out = (w_out_a, w_out_b, w_out_c, w_out_d)
    ln_g = (ln_g_a, ln_g_b, ln_g_c, ln_g_d)
    ln_b = (ln_b_a, ln_b_b, ln_b_c, ln_b_d)
    for i in range(DEPTH):
        m = i % N_MIXERS
        h = x @ w_in[m]
        o, z = mixers[m](h)
        y = (o * jax.nn.silu(z)) @ w_out[m]
        x = layer_norm(DN_ALPHA * x + y, ln_g[m], ln_b[m])
    return x
```

```python
import functools
import math

import numpy as np
import jax
import jax.numpy as jnp
from jax import lax
from jax.experimental import pallas as pl
from jax.experimental.pallas import tpu as pltpu

F32 = jnp.float32
BF16 = jnp.bfloat16

N_HEADS = 16
HEAD_DIM = 128
N_KV = 4
HPG = N_HEADS // N_KV
WIDTH = N_HEADS * HEAD_DIM
KV_WIDTH = N_KV * HEAD_DIM
GROUP_W = HPG * HEAD_DIM
REL_BUCKETS = 32
REL_MAX_DIST = 128
CMP_LEN = 32
CMP_STRIDE = 16
SEL_LEN = 64
SEL_TOPK = 8
NSA_WINDOW = 512
SWA_WINDOW = 128
DEPTH = 4
DN_ALPHA = (2 * DEPTH) ** 0.25
LN_EPS = 1e-5
NEG_INF = -1e30
FORCED_SCORE = 1e4
SCALE = HEAD_DIM ** -0.5

LANES = 128
M_INIT = -1e29
VMEM_LIMIT = 56 * 1024 * 1024


def _cparams(*sem):
    return pltpu.CompilerParams(dimension_semantics=sem, vmem_limit_bytes=VMEM_LIMIT)


def _silu(z):
    return z / (1.0 + jnp.exp(-z))


def _softplus(z):
    return jnp.maximum(z, 0.0) + jnp.log1p(jnp.exp(-jnp.abs(z)))


def _dot_nt(a, b):
    return lax.dot_general(a, b, (((1,), (1,)), ((), ())), preferred_element_type=F32)


def _dot(a, b):
    return jnp.dot(a, b, preferred_element_type=F32)


def _stack_heads(x):
    return jnp.concatenate([x[:, r * HEAD_DIM:(r + 1) * HEAD_DIM] for r in range(HPG)], axis=0)


def _unstack_heads(x, rows):
    return jnp.concatenate([x[r * rows:(r + 1) * rows] for r in range(HPG)], axis=1)


def _split3(x):
    a1 = x.astype(BF16)
    r1 = x - a1.astype(F32)
    a2 = r1.astype(BF16)
    a3 = (r1 - a2.astype(F32)).astype(BF16)
    return a1, a2, a3


def _mm_kernel(x_ref, w_ref, o_ref):
    o_ref[...] = _dot(x_ref[...], w_ref[...]).astype(o_ref.dtype)


def _matmul(x, w, out_dtype, tm, tn, name):
    m, k = x.shape
    n = w.shape[1]
    assert m % tm == 0 and n % tn == 0
    return pl.pallas_call(
        _mm_kernel,
        grid=(n // tn, m // tm),
        in_specs=[pl.BlockSpec((tm, k), lambda j, i: (i, 0)),
                  pl.BlockSpec((k, tn), lambda j, i: (0, j))],
        out_specs=pl.BlockSpec((tm, tn), lambda j, i: (i, j)),
        out_shape=jax.ShapeDtypeStruct((m, n), out_dtype),
        compiler_params=_cparams("arbitrary", "arbitrary"),
        name=name,
    )(x, w)


def _outln_kernel(og_ref, w_ref, x_ref, g_ref, b_ref, xo_ref, xb_ref):
    y = _dot(og_ref[...], w_ref[...])
    t = DN_ALPHA * x_ref[...] + y
    mu = jnp.mean(t, axis=-1, keepdims=True)
    d = t - mu
    var = jnp.mean(d * d, axis=-1, keepdims=True)
    out = d * lax.rsqrt(var + LN_EPS) * g_ref[...] + b_ref[...]
    xo_ref[...] = out
    xb_ref[...] = out.astype(BF16)


def _out_proj_ln(og, w_out, x, ln_g, ln_b, tm=256):
    m, k = og.shape
    n = w_out.shape[1]
    return pl.pallas_call(
        _outln_kernel,
        grid=(m // tm,),
        in_specs=[pl.BlockSpec((tm, k), lambda i: (i, 0)),
                  pl.BlockSpec((k, n), lambda i: (0, 0)),
                  pl.BlockSpec((tm, n), lambda i: (i, 0)),
                  pl.BlockSpec((1, n), lambda i: (0, 0)),
                  pl.BlockSpec((1, n), lambda i: (0, 0))],
        out_specs=[pl.BlockSpec((tm, n), lambda i: (i, 0)),
                   pl.BlockSpec((tm, n), lambda i: (i, 0))],
        out_shape=[jax.ShapeDtypeStruct((m, n), F32), jax.ShapeDtypeStruct((m, n), BF16)],
        compiler_params=_cparams("arbitrary"),
        name="out_proj_ln",
    )(og, w_out.astype(BF16), x, ln_g.reshape(1, n), ln_b.reshape(1, n))


def _bucket_np(dist):
    max_exact = REL_BUCKETS // 2
    ratio = np.maximum(dist, max_exact).astype(np.float32) / max_exact
    large = max_exact + (np.log(ratio) / math.log(REL_MAX_DIST / max_exact)
                         * (REL_BUCKETS - max_exact)).astype(np.int32)
    return np.where(dist < max_exact, dist, np.minimum(large, REL_BUCKETS - 1))


def _band_tables(rel_bias, nd, window):
    ii = np.arange(LANES)[:, None]
    jj = np.arange(LANES)[None, :]
    dist = np.stack([LANES * d + ii - jj for d in range(nd)])
    ok = dist >= 0
    if window is not None:
        ok &= dist < window
    t = rel_bias[_bucket_np(np.maximum(dist, 0))]
    t = jnp.where(ok[..., None], t, NEG_INF).astype(F32)
    t = t.transpose(3, 0, 1, 2).reshape(N_KV, HPG, nd, LANES, LANES)
    return t.transpose(0, 2, 1, 3, 4).reshape(N_KV, nd, HPG * LANES, LANES)


def _online_update(s, v, m, l, acc):
    m_new = jnp.maximum(m, jnp.max(s, axis=1, keepdims=True))
    p = jnp.exp(s - m_new)
    alpha = jnp.exp(m - m_new)
    l = alpha * l + jnp.sum(p, axis=1, keepdims=True)
    acc = alpha * acc + _dot(p.astype(BF16), v)
    return m_new, l, acc


def _banded_kernel(*refs, nd, has_sink, gated):
    refs = list(refs)
    q_ref, k_ref, v_ref, tbl_ref = refs[:4]
    rest = refs[4:]
    sink_ref = rest.pop(0) if has_sink else None
    z_ref = rest.pop(0) if gated else None
    (o_ref,) = rest
    i = pl.program_id(2)
    rows = HPG * LANES
    q4 = _stack_heads(q_ref[...])
    if has_sink:
        m = sink_ref[...]
        l = jnp.ones((rows, 1), F32)
    else:
        m = jnp.full((rows, 1), M_INIT, F32)
        l = jnp.zeros((rows, 1), F32)
    acc = jnp.zeros((rows, HEAD_DIM), F32)
    for d in reversed(range(nd)):
        kb = i - d
        start = pl.multiple_of(jnp.maximum(kb, 0) * LANES, LANES)
        k = k_ref[pl.ds(start, LANES), :]
        v = v_ref[pl.ds(start, LANES), :]
        s = _dot_nt(q4, k) * SCALE + tbl_ref[d]
        s = jnp.where(kb >= 0, s, NEG_INF)
        m, l, acc = _online_update(s, v, m, l, acc)
    o = _unstack_heads(acc / jnp.maximum(l, 1e-30), LANES)
    if gated:
        o = o * _silu(z_ref[...].astype(F32))
    o_ref[...] = o.astype(o_ref.dtype)


def _banded_gqa(h, batch, seq, qblk, kblk, vblk, tbl, sinks=None, zblk=None, name="banded"):
    nq = seq // LANES
    nd = tbl.shape[1]
    in_specs = [pl.BlockSpec((LANES, GROUP_W), lambda b, g, i: (b * nq + i, qblk + g)),
                pl.BlockSpec((seq, HEAD_DIM), lambda b, g, i: (b, kblk + g)),
                pl.BlockSpec((seq, HEAD_DIM), lambda b, g, i: (b, vblk + g)),
                pl.BlockSpec((None, nd, HPG * LANES, LANES), lambda b, g, i: (g, 0, 0, 0))]
    args = [h, h, h, tbl]
    if sinks is not None:
        sink_col = jnp.repeat(sinks.astype(F32).reshape(N_KV, HPG), LANES, axis=1).reshape(N_KV, HPG * LANES, 1)
        in_specs.append(pl.BlockSpec((None, HPG * LANES, 1), lambda b, g, i: (g, 0, 0)))
        args.append(sink_col)
    if zblk is not None:
        in_specs.append(pl.BlockSpec((LANES, GROUP_W), lambda b, g, i: (b * nq + i, zblk + g)))
        args.append(h)
    return pl.pallas_call(
        functools.partial(_banded_kernel, nd=nd, has_sink=sinks is not None, gated=zblk is not None),
        grid=(batch, N_KV, nq),
        in_specs=in_specs,
        out_specs=pl.BlockSpec((LANES, GROUP_W), lambda b, g, i: (b * nq + i, g)),
        out_shape=jax.ShapeDtypeStruct((batch * seq, WIDTH), BF16),
        compiler_params=_cparams("arbitrary", "arbitrary", "arbitrary"),
        name=name,
    )(*args)


def _compress_kernel(kc_ref, vc_ref, pek_ref, w1k_ref, w2k_ref, pev_ref, w1v_ref, w2v_ref,
                     ko_ref, vo_ref, xs_ref, *, nc):
    half = CMP_LEN // 2
    for x_ref, pe_ref, w1_ref, w2_ref, o_ref in ((kc_ref, pek_ref, w1k_ref, w2k_ref, ko_ref),
                                                 (vc_ref, pev_ref, w1v_ref, w2v_ref, vo_ref)):
        xs_ref[...] = x_ref[...].astype(F32)
        first = jnp.zeros((nc, HEAD_DIM), F32)
        second = jnp.zeros((nc, HEAD_DIM), F32)
        for l in range(half):
            xl = xs_ref[pl.ds(l, nc, stride=CMP_STRIDE), :]
            first += _dot((xl + pe_ref[l:l + 1, :]).astype(BF16), w1_ref[l])
            second += _dot((xl + pe_ref[half + l:half + l + 1, :]).astype(BF16), w1_ref[half + l])
        hid = _silu(first + pltpu.roll(second, nc - 1, 0))
        o_ref[...] = _dot(hid.astype(BF16), w2_ref[...]).astype(o_ref.dtype)


def _nsa_compress(h, batch, seq, kcblk, vcblk, pe_k, w1_k, w2_k, pe_v, w1_v, w2_v):
    assert CMP_LEN == 2 * CMP_STRIDE
    nc = seq // CMP_STRIDE
    kv_spec = lambda blk: pl.BlockSpec((seq, HEAD_DIM), lambda b, g: (b, blk + g))
    full = lambda shape: pl.BlockSpec(shape, lambda b, g: (0,) * len(shape))
    out_spec = pl.BlockSpec((None, None, nc, HEAD_DIM), lambda b, g: (b, g, 0, 0))
    out_shape = jax.ShapeDtypeStruct((batch, N_KV, nc, HEAD_DIM), BF16)
    return pl.pallas_call(
        functools.partial(_compress_kernel, nc=nc),
        grid=(batch, N_KV),
        in_specs=[kv_spec(kcblk), kv_spec(vcblk),
                  full((CMP_LEN, HEAD_DIM)), full((CMP_LEN, HEAD_DIM, HEAD_DIM)), full((HEAD_DIM, HEAD_DIM)),
                  full((CMP_LEN, HEAD_DIM)), full((CMP_LEN, HEAD_DIM, HEAD_DIM)), full((HEAD_DIM, HEAD_DIM))],
        out_specs=[out_spec, out_spec],
        out_shape=[out_shape, out_shape],
        scratch_shapes=[pltpu.VMEM((seq, HEAD_DIM), F32)],
        compiler_params=_cparams("arbitrary", "arbitrary"),
        name="nsa_compress",
    )(h, h, pe_k, w1_k.astype(BF16), w2_k.astype(BF16), pe_v, w1_v.astype(BF16), w2_v.astype(BF16))


def _cmp_tables(rel_bias, seq, nc):
    n_cmp = (seq - CMP_LEN) // CMP_STRIDE + 1
    n = np.arange(nc)[None, :]
    dist = np.arange(seq)[:, None] - (n * CMP_STRIDE + CMP_LEN - 1)
    ok = (dist >= 0) & (n < n_cmp)
    t = rel_bias[_bucket_np(np.maximum(dist, 0))]
    t = jnp.where(ok[..., None], t, NEG_INF).astype(F32)
    return t.transpose(2, 0, 1), t.transpose(2, 1, 0)


def _inter_t(seq, nc):
    nb = seq // SEL_LEN
    cstart = np.arange(nc) * CMP_STRIDE
    sstart = np.arange(nb) * SEL_LEN
    inter = np.clip(np.minimum(cstart[None, :] + CMP_LEN, sstart[:, None] + SEL_LEN)
                    - np.maximum(cstart[None, :], sstart[:, None]), 0, None) / CMP_LEN
    return inter.astype(np.float32)


def _cmp_select_kernel(q_ref, kc_ref, vc_ref, tbl_ref, tblt_ref, inter_ref, o_ref, sel_ref, *, nb):
    i = pl.program_id(1)
    q = q_ref[...]
    kc = kc_ref[...]
    vc = vc_ref[...]
    outs = []
    p_sum_t = None
    for r in range(HPG):
        qr = q[:, r * HEAD_DIM:(r + 1) * HEAD_DIM]
        tbl = tbl_ref[r]
        s = _dot_nt(qr, kc) * SCALE + tbl
        p = jnp.where(tbl > 0.5 * NEG_INF, jnp.exp(s - jnp.max(s, axis=1, keepdims=True)), 0.0)
        p = p / jnp.maximum(jnp.sum(p, axis=1, keepdims=True), 1e-30)
        outs.append(_dot(p.astype(BF16), vc))
        tblt = tblt_ref[r]
        st = _dot_nt(kc, qr) * SCALE + tblt
        pt = jnp.where(tblt > 0.5 * NEG_INF, jnp.exp(st - jnp.max(st, axis=0, keepdims=True)), 0.0)
        pt = pt / jnp.maximum(jnp.sum(pt, axis=0, keepdims=True), 1e-30)
        p_sum_t = pt if p_sum_t is None else p_sum_t + pt
    o_ref[...] = jnp.concatenate(outs, axis=1).astype(o_ref.dtype)

    inter = inter_ref[...]
    imp = sum(_dot(inter, part) for part in _split3(p_sum_t))
    blk = lax.broadcasted_iota(jnp.int32, (nb, LANES), 0)
    pos = i * LANES + lax.broadcasted_iota(jnp.int32, (nb, LANES), 1)
    cur = lax.shift_right_logical(pos, int(math.log2(SEL_LEN)))
    allowed = blk * SEL_LEN <= pos
    forced = (blk == 0) | (blk == cur) | (blk == cur - 1)
    imp = jnp.where(allowed, jnp.where(forced, FORCED_SCORE, imp), NEG_INF)
    rank = jnp.zeros((nb, LANES), F32)
    for c in range(nb):
        row = imp[c:c + 1, :]
        earlier = jnp.where(blk > c, 1.0, 0.0)
        rank += jnp.where(row > imp, 1.0, jnp.where(row == imp, earlier, 0.0))
    sel_t = jnp.where((rank < SEL_TOPK) & (imp > 0.5 * NEG_INF), 1.0, 0.0)
    sel_t = jnp.concatenate([sel_t, jnp.zeros((LANES - nb, LANES), F32)], axis=0)
    sel_ref[...] = sel_t.T.astype(sel_ref.dtype)


def _nsa_cmp_select(h, batch, seq, k_cmp, v_cmp, rel_bias):
    nq = seq // LANES
    nc = seq // CMP_STRIDE
    nb = seq // SEL_LEN
    assert nc == LANES and nb <= LANES
    tbl, tblt = _cmp_tables(rel_bias, seq, nc)
    inter = jnp.asarray(_inter_t(seq, nc), BF16)
    cmp_spec = pl.BlockSpec((None, None, nc, HEAD_DIM), lambda g, i, b: (b, g, 0, 0))
    return pl.pallas_call(
        functools.partial(_cmp_select_kernel, nb=nb),
        grid=(N_KV, nq, batch),
        in_specs=[pl.BlockSpec((LANES, GROUP_W), lambda g, i, b: (b * nq + i, g)),
                  cmp_spec, cmp_spec,
                  pl.BlockSpec((HPG, LANES, nc), lambda g, i, b: (g, i, 0)),
                  pl.BlockSpec((HPG, nc, LANES), lambda g, i, b: (g, 0, i)),
                  pl.BlockSpec((nb, nc), lambda g, i, b: (0, 0))],
        out_specs=[pl.BlockSpec((LANES, GROUP_W), lambda g, i, b: (b * nq + i, g)),
                   pl.BlockSpec((None, None, LANES, LANES), lambda g, i, b: (b, g, i, 0))],
        out_shape=[jax.ShapeDtypeStruct((batch * seq, WIDTH), BF16),
                   jax.ShapeDtypeStruct((batch, N_KV, seq, LANES), BF16)],
        compiler_params=_cparams("arbitrary", "arbitrary", "arbitrary"),
        name="nsa_cmp_select",
    )(h, k_cmp, v_cmp, tbl, tblt, inter)


def _expand_np(seq):
    nk = seq // LANES
    j = np.arange(LANES)[None, :, None]
    key = np.arange(nk)[:, None, None] * LANES + np.arange(LANES)[None, None, :]
    return (j == key // SEL_LEN).astype(np.float32)


def _sel_attn_kernel(q_ref, k_ref, v_ref, sel_ref, tbl_ref, exp_ref, o_ref):
    i = pl.program_id(2)
    rows = HPG * LANES
    q4 = _stack_heads(q_ref[...])
    sel = sel_ref[...]
    far = tbl_ref.shape[0] - 1

    def body(kb, carry):
        m, l, acc = carry
        start = pl.multiple_of(kb * LANES, LANES)
        k = k_ref[pl.ds(start, LANES), :]
        v = v_ref[pl.ds(start, LANES), :]
        s = _dot_nt(q4, k) * SCALE + tbl_ref[jnp.minimum(i - kb, far)]
        chosen = _dot(sel, exp_ref[kb])
        chosen = jnp.concatenate([chosen] * HPG, axis=0)
        s = jnp.where(chosen > 0.5, s, NEG_INF)
        return _online_update(s, v, m, l, acc)

    init = (jnp.full((rows, 1), M_INIT, F32), jnp.zeros((rows, 1), F32), jnp.zeros((rows, HEAD_DIM), F32))
    m, l, acc = lax.fori_loop(0, i + 1, body, init)
    o_ref[...] = _unstack_heads(acc / jnp.maximum(l, 1e-30), LANES).astype(o_ref.dtype)


def _nsa_sel_attn(h, batch, seq, qblk, kblk, vblk, sel, rel_bias):
    nq = seq // LANES
    tbl = _band_tables(rel_bias, 3, None)
    expand = jnp.asarray(_expand_np(seq), BF16)
    return pl.pallas_call(
        _sel_attn_kernel,
        grid=(batch, N_KV, nq),
        in_specs=[pl.BlockSpec((LANES, GROUP_W), lambda b, g, i: (b * nq + i, qblk + g)),
                  pl.BlockSpec((seq, HEAD_DIM), lambda b, g, i: (b, kblk + g)),
                  pl.BlockSpec((seq, HEAD_DIM), lambda b, g, i: (b, vblk + g)),
                  pl.BlockSpec((None, None, LANES, LANES), lambda b, g, i: (b, g, i, 0)),
                  pl.BlockSpec((None, 3, HPG * LANES, LANES), lambda b, g, i: (g, 0, 0, 0)),
                  pl.BlockSpec((nq, LANES, LANES), lambda b, g, i: (0, 0, 0))],
        out_specs=pl.BlockSpec((LANES, GROUP_W), lambda b, g, i: (b * nq + i, g)),
        out_shape=jax.ShapeDtypeStruct((batch * seq, WIDTH), BF16),
        compiler_params=_cparams("arbitrary", "arbitrary", "arbitrary"),
        name="nsa_sel_attn",
    )(h, h, h, sel, tbl, expand)


def _combine_kernel(oc_ref, os_ref, ow_ref, gl_ref, z_ref, o_ref):
    gates = 1.0 / (1.0 + jnp.exp(-gl_ref[...]))
    oc = oc_ref[...].astype(F32)
    osel = os_ref[...].astype(F32)
    ow = ow_ref[...].astype(F32)
    z = z_ref[...].astype(F32)
    outs = []
    for r in range(HPG):
        sl = slice(r * HEAD_DIM, (r + 1) * HEAD_DIM)
        o = (gates[:, 3 * r:3 * r + 1] * oc[:, sl] + gates[:, 3 * r + 1:3 * r + 2] * osel[:, sl]
             + gates[:, 3 * r + 2:3 * r + 3] * ow[:, sl])
        outs.append(o * _silu(z[:, sl]))
    o_ref[...] = jnp.concatenate(outs, axis=1).astype(o_ref.dtype)


def _nsa_combine(o_cmp, o_sel, o_win, gl, h, zblk, tm=512):
    m = o_cmp.shape[0]
    grp = pl.BlockSpec((tm, GROUP_W), lambda i, g: (i, g))
    return pl.pallas_call(
        _combine_kernel,
        grid=(m // tm, N_KV),
        in_specs=[grp, grp, grp,
                  pl.BlockSpec((tm, LANES), lambda i, g: (i, g)),
                  pl.BlockSpec((tm, GROUP_W), lambda i, g: (i, zblk + g))],
        out_specs=grp,
        out_shape=jax.ShapeDtypeStruct((m, WIDTH), BF16),
        compiler_params=_cparams("arbitrary", "arbitrary"),
        name="nsa_combine",
    )(o_cmp, o_sel, o_win, gl, h)


def _sb_kernel(q_ref, k_ref, v_ref, z_ref, o_ref, *, t):
    i = pl.program_id(2)
    q = q_ref[...]
    row = lax.broadcasted_iota(jnp.int32, (t, t), 0)
    col = lax.broadcasted_iota(jnp.int32, (t, t), 1)
    later = jnp.where(row > col, 1.0, 0.0).astype(BF16)
    strict = col < row

    def tile(kb, carry, diag):
        run, acc = carry
        start = pl.multiple_of(kb * t, t)
        k = k_ref[pl.ds(start, t), :]
        v = v_ref[pl.ds(start, t), :]
        zl = _dot_nt(q, k) * SCALE
        sp = _softplus(zl)
        log_keep = -sp
        log_beta = zl - sp
        if diag:
            log_keep = jnp.where(strict, log_keep, 0.0)
        hi = log_keep.astype(BF16)
        lo = (log_keep - hi.astype(F32)).astype(BF16)
        after = _dot(hi, later) + _dot(lo, later) + run
        a = jnp.exp(log_beta + after)
        if diag:
            a = jnp.where(strict, a, 0.0)
        acc = acc + _dot(a.astype(BF16), v)
        run = run + jnp.sum(log_keep, axis=1, keepdims=True)
        return run, acc

    carry = tile(i, (jnp.zeros((t, 1), F32), jnp.zeros((t, HEAD_DIM), F32)), True)
    _, acc = lax.fori_loop(1, i + 1, lambda s, c: tile(i - s, c, False), carry)
    o_ref[...] = (acc * _silu(z_ref[...].astype(F32))).astype(o_ref.dtype)


def _full_attn_call(kernel, h, batch, seq, t, extra_specs, extra_args, name):
    nq = seq // t
    in_specs = [pl.BlockSpec((t, HEAD_DIM), lambda b, hh, i: (b * nq + i, hh)),
                pl.BlockSpec((seq, HEAD_DIM), lambda b, hh, i: (b, N_HEADS + hh)),
                pl.BlockSpec((seq, HEAD_DIM), lambda b, hh, i: (b, 2 * N_HEADS + hh)),
                pl.BlockSpec((t, HEAD_DIM), lambda b, hh, i: (b * nq + i, 3 * N_HEADS + hh))]
    return pl.pallas_call(
        functools.partial(kernel, t=t),
        grid=(batch, N_HEADS, nq),
        in_specs=in_specs + extra_specs,
        out_specs=pl.BlockSpec((t, HEAD_DIM), lambda b, hh, i: (b * nq + i, hh)),
        out_shape=jax.ShapeDtypeStruct((batch * seq, WIDTH), BF16),
        compiler_params=_cparams("arbitrary", "arbitrary", "arbitrary"),
        name=name,
    )(h, h, h, h, *extra_args)


def _fox_prep_kernel(fl_ref, bias_ref, ct_ref, *, seq):
    row = lax.broadcasted_iota(jnp.int32, (LANES, LANES), 0)
    col = lax.broadcasted_iota(jnp.int32, (LANES, LANES), 1)
    upto = jnp.where(row <= col, 1.0, 0.0).astype(BF16)
    carry = jnp.zeros((LANES, 1), F32)
    for blk in range(seq // LANES):
        x = fl_ref[blk * LANES:(blk + 1) * LANES, :] + bias_ref[...]
        log_f = (jnp.minimum(x, 0.0) - jnp.log1p(jnp.exp(-jnp.abs(x)))).T
        cs = sum(_dot(part, upto) for part in _split3(log_f)) + carry
        ct_ref[:, blk * LANES:(blk + 1) * LANES] = cs
        carry = cs[:, LANES - 1:LANES]


def _fox_prep(fl, fgate_bias, batch, seq):
    bias = jnp.zeros((1, LANES), F32).at[0, :N_HEADS].set(fgate_bias.astype(F32))
    return pl.pallas_call(
        functools.partial(_fox_prep_kernel, seq=seq),
        grid=(batch,),
        in_specs=[pl.BlockSpec((seq, LANES), lambda b: (b, 0)),
                  pl.BlockSpec((1, LANES), lambda b: (0, 0))],
        out_specs=pl.BlockSpec((None, LANES, seq), lambda b: (b, 0, 0)),
        out_shape=jax.ShapeDtypeStruct((batch, LANES, seq), F32),
        compiler_params=_cparams("arbitrary"),
        name="fox_prep",
    )(fl, bias)


def _fox_kernel(q_ref, k_ref, v_ref, z_ref, ccol_ref, crow_ref, o_ref, *, t):
    i = pl.program_id(2)
    q = q_ref[...]
    c_q = ccol_ref[i]
    row = lax.broadcasted_iota(jnp.int32, (t, t), 0)
    col = lax.broadcasted_iota(jnp.int32, (t, t), 1)
    causal = col <= row

    def tile(kb, carry, diag):
        m, l, acc = carry
        start = pl.multiple_of(kb * t, t)
        k = k_ref[pl.ds(start, t), :]
        v = v_ref[pl.ds(start, t), :]
        s = _dot_nt(q, k) * SCALE + c_q - crow_ref[kb]
        if diag:
            s = jnp.where(causal, s, NEG_INF)
        return _online_update(s, v, m, l, acc)

    init = (jnp.full((t, 1), M_INIT, F32), jnp.zeros((t, 1), F32), jnp.zeros((t, HEAD_DIM), F32))
    carry = tile(i, init, True)
    _, l, acc = lax.fori_loop(1, i + 1, lambda s, c: tile(i - s, c, False), carry)
    o = acc / jnp.maximum(l, 1e-30)
    o_ref[...] = (o * _silu(z_ref[...].astype(F32))).astype(o_ref.dtype)


def _w_cols(w, *ranges):
    return jnp.concatenate([w[:, a:b] for a, b in ranges], axis=1).astype(BF16)


def _pad_cols(w, n):
    return jnp.pad(w, ((0, 0), (0, n - w.shape[1]))).astype(BF16)


def _layer_nsa(xb, batch, seq, rel_bias, w_in, cmp_pe_k, cmp_w1_k, cmp_w2_k, cmp_pe_v, cmp_w1_v, cmp_w2_v):
    gate0 = WIDTH + 6 * KV_WIDTH
    gate1 = gate0 + 3 * N_HEADS
    h = _matmul(xb, _w_cols(w_in, (0, gate0), (gate1, w_in.shape[1])), BF16, 512, 1024, "proj_nsa")
    per_group = 3 * HPG
    w_gate = w_in[:, gate0:gate1].reshape(-1, N_KV, per_group)
    w_gate = jnp.pad(w_gate, ((0, 0), (0, 0), (0, LANES - per_group))).reshape(-1, N_KV * LANES)
    gl = _matmul(xb, w_gate.astype(BF16), F32, 512, N_KV * LANES, "proj_nsa_gates")
    kv0 = WIDTH // HEAD_DIM
    blk = lambda j: kv0 + j * N_KV
    k_cmp, v_cmp = _nsa_compress(h, batch, seq, blk(0), blk(1), cmp_pe_k, cmp_w1_k, cmp_w2_k,
                                 cmp_pe_v, cmp_w1_v, cmp_w2_v)
    o_cmp, sel = _nsa_cmp_select(h, batch, seq, k_cmp, v_cmp, rel_bias)
    o_sel = _nsa_sel_attn(h, batch, seq, 0, blk(2), blk(3), sel, rel_bias)
    nd = NSA_WINDOW // LANES + 1
    o_win = _banded_gqa(h, batch, seq, 0, blk(4), blk(5), _band_tables(rel_bias, nd, NSA_WINDOW), name="nsa_window")
    zblk = (WIDTH + 6 * KV_WIDTH) // GROUP_W
    return _nsa_combine(o_cmp, o_sel, o_win, gl, h, zblk)


def _layer_swa(xb, batch, seq, rel_bias, w_in, sinks):
    h = _matmul(xb, w_in.astype(BF16), BF16, 512, 1024, "proj_swa")
    kv0 = WIDTH // HEAD_DIM
    nd = SWA_WINDOW // LANES + 1
    return _banded_gqa(h, batch, seq, 0, kv0, kv0 + N_KV, _band_tables(rel_bias, nd, SWA_WINDOW),
                       sinks=sinks, zblk=(WIDTH + 2 * KV_WIDTH) // GROUP_W, name="swa")


def _layer_sb(xb, batch, seq, w_in, t=256):
    h = _matmul(xb, w_in.astype(BF16), BF16, 512, 1024, "proj_sb")
    return _full_attn_call(_sb_kernel, h, batch, seq, t, [], [], "stick_breaking")


def _layer_fox(xb, batch, seq, w_in, fgate_bias, t=256):
    f0 = 3 * WIDTH
    f1 = f0 + N_HEADS
    h = _matmul(xb, _w_cols(w_in, (0, f0), (f1, w_in.shape[1])), BF16, 512, 1024, "proj_fox")
    fl = _matmul(xb, _pad_cols(w_in[:, f0:f1], LANES), F32, 512, LANES, "proj_fox_gates")
    ct = _fox_prep(fl, fgate_bias, batch, seq)
    nq = seq // t
    ccol = ct.reshape(batch, LANES, nq, t, 1)
    crow = ct.reshape(batch, LANES, nq, 1, t)
    extra_specs = [pl.BlockSpec((None, None, nq, t, 1), lambda b, hh, i: (b, hh, 0, 0, 0)),
                   pl.BlockSpec((None, None, nq, 1, t), lambda b, hh, i: (b, hh, 0, 0, 0))]
    return _full_attn_call(_fox_kernel, h, batch, seq, t, extra_specs, [ccol, crow], "forgetting")


def kernel(x, rel_bias, w_in_a, w_out_a, ln_g_a, ln_b_a, cmp_pe_k, cmp_w1_k, cmp_w2_k, cmp_pe_v, cmp_w1_v, cmp_w2_v, w_in_b, w_out_b, ln_g_b, ln_b_b, sinks_b, w_in_c, w_out_c, ln_g_c, ln_b_c, w_in_d, w_out_d, ln_g_d, ln_b_d, fgate_bias_d):
    batch, seq, d_model = x.shape
    xf = x.reshape(batch * seq, d_model)
    xb = xf.astype(BF16)
    og = _layer_nsa(xb, batch, seq, rel_bias, w_in_a, cmp_pe_k, cmp_w1_k, cmp_w2_k, cmp_pe_v, cmp_w1_v, cmp_w2_v)
    xf, xb = _out_proj_ln(og, w_out_a, xf, ln_g_a, ln_b_a)
    og = _layer_swa(xb, batch, seq, rel_bias, w_in_b, sinks_b)
    xf, xb = _out_proj_ln(og, w_out_b, xf, ln_g_b, ln_b_b)
    og = _layer_sb(xb, batch, seq, w_in_c)
    xf, xb = _out_proj_ln(og, w_out_c, xf, ln_g_c, ln_b_c)
    og = _layer_fox(xb, batch, seq, w_in_d, fgate_bias_d)
    xf, xb = _out_proj_ln(og, w_out_d, xf, ln_g_d, ln_b_d)
    return xf.reshape(batch, seq, d_model)
```

```python
import functools
import math

import numpy as np
import jax
import jax.numpy as jnp
from jax import lax
from jax.experimental import pallas as pl
from jax.experimental.pallas import tpu as pltpu

F32 = jnp.float32
BF16 = jnp.bfloat16

N_HEADS = 16
HEAD_DIM = 128
N_KV = 4
HPG = N_HEADS // N_KV
WIDTH = N_HEADS * HEAD_DIM
KV_WIDTH = N_KV * HEAD_DIM
GROUP_W = HPG * HEAD_DIM
REL_BUCKETS = 32
REL_MAX_DIST = 128
CMP_LEN = 32
CMP_STRIDE = 16
SEL_LEN = 64
SEL_TOPK = 8
NSA_WINDOW = 512
SWA_WINDOW = 128
DEPTH = 4
DN_ALPHA = (2 * DEPTH) ** 0.25
LN_EPS = 1e-5
NEG_INF = -1e30
FORCED_SCORE = 1e4
SCALE = HEAD_DIM ** -0.5
LOG2E = math.log2(math.e)
QSCALE = SCALE * LOG2E

LANES = 128
M_INIT = -1e29
VMEM_LIMIT = 56 * 1024 * 1024


def _cparams(*sem):
    return pltpu.CompilerParams(dimension_semantics=sem, vmem_limit_bytes=VMEM_LIMIT)


def _silu(z):
    return z / (1.0 + jnp.exp(-z))


def _dot_nt(a, b):
    return lax.dot_general(a, b, (((1,), (1,)), ((), ())), preferred_element_type=F32)


def _dot(a, b):
    return jnp.dot(a, b, preferred_element_type=F32)


def _stack_heads(x):
    return jnp.concatenate([x[:, r * HEAD_DIM:(r + 1) * HEAD_DIM] for r in range(HPG)], axis=0)


def _unstack_heads(x, rows):
    return jnp.concatenate([x[r * rows:(r + 1) * rows] for r in range(HPG)], axis=1)


def _split3(x):
    a1 = x.astype(BF16)
    r1 = x - a1.astype(F32)
    a2 = r1.astype(BF16)
    a3 = (r1 - a2.astype(F32)).astype(BF16)
    return a1, a2, a3


def _mm_kernel(x_ref, w_ref, o_ref):
    o_ref[...] = _dot(x_ref[...], w_ref[...]).astype(o_ref.dtype)


def _matmul(x, w, out_dtype, tm, tn, name):
    m, k = x.shape
    n = w.shape[1]
    assert m % tm == 0 and n % tn == 0
    return pl.pallas_call(
        _mm_kernel,
        grid=(n // tn, m // tm),
        in_specs=[pl.BlockSpec((tm, k), lambda j, i: (i, 0)),
                  pl.BlockSpec((k, tn), lambda j, i: (0, j))],
        out_specs=pl.BlockSpec((tm, tn), lambda j, i: (i, j)),
        out_shape=jax.ShapeDtypeStruct((m, n), out_dtype),
        compiler_params=_cparams("arbitrary", "arbitrary"),
        name=name,
    )(x, w)


def _outln_kernel(og_ref, w_ref, x_ref, g_ref, b_ref, xo_ref, xb_ref):
    y = _dot(og_ref[...], w_ref[...])
    t = DN_ALPHA * x_ref[...] + y
    mu = jnp.mean(t, axis=-1, keepdims=True)
    d = t - mu
    var = jnp.mean(d * d, axis=-1, keepdims=True)
    out = d * lax.rsqrt(var + LN_EPS) * g_ref[...] + b_ref[...]
    xo_ref[...] = out
    xb_ref[...] = out.astype(BF16)


def _out_proj_ln(og, w_out, x, ln_g, ln_b, tm=256):
    m, k = og.shape
    n = w_out.shape[1]
    return pl.pallas_call(
        _outln_kernel,
        grid=(m // tm,),
        in_specs=[pl.BlockSpec((tm, k), lambda i: (i, 0)),
                  pl.BlockSpec((k, n), lambda i: (0, 0)),
                  pl.BlockSpec((tm, n), lambda i: (i, 0)),
                  pl.BlockSpec((1, n), lambda i: (0, 0)),
                  pl.BlockSpec((1, n), lambda i: (0, 0))],
        out_specs=[pl.BlockSpec((tm, n), lambda i: (i, 0)),
                   pl.BlockSpec((tm, n), lambda i: (i, 0))],
        out_shape=[jax.ShapeDtypeStruct((m, n), F32), jax.ShapeDtypeStruct((m, n), BF16)],
        compiler_params=_cparams("arbitrary"),
        name="out_proj_ln",
    )(og, w_out.astype(BF16), x, ln_g.reshape(1, n), ln_b.reshape(1, n))


def _bucket_np(dist):
    max_exact = REL_BUCKETS // 2
    ratio = np.maximum(dist, max_exact).astype(np.float32) / max_exact
    large = max_exact + (np.log(ratio) / math.log(REL_MAX_DIST / max_exact)
                         * (REL_BUCKETS - max_exact)).astype(np.int32)
    return np.where(dist < max_exact, dist, np.minimum(large, REL_BUCKETS - 1))


def _rel_table(rel_bias, dist, ok):
    bucket = _bucket_np(np.maximum(dist, 0)).reshape(-1)
    onehot_t = np.zeros((REL_BUCKETS, bucket.size), np.float32)
    onehot_t[bucket, np.arange(bucket.size)] = 1.0
    t = jnp.dot(rel_bias.T.astype(F32) * LOG2E, jnp.asarray(onehot_t), precision=lax.Precision.HIGHEST)
    blocked = np.where(ok.reshape(-1), 0.0, NEG_INF).astype(np.float32)
    return (t + blocked[None, :]).reshape((N_HEADS,) + dist.shape)


def _band_tables(rel_bias, nd, window, with_blocked=False):
    ii = np.arange(LANES)[:, None]
    jj = np.arange(LANES)[None, :]
    dist = np.stack([LANES * d + ii - jj for d in range(nd)])
    ok = dist >= 0
    if window is not None:
        ok &= dist < window
    if with_blocked:
        dist = np.concatenate([dist, np.zeros((1, LANES, LANES), dist.dtype)])
        ok = np.concatenate([ok, np.zeros((1, LANES, LANES), bool)])
    return _rel_table(rel_bias, dist, ok)


def _group_table(tbl_ref, idx):
    return jnp.concatenate([tbl_ref[r, idx] for r in range(HPG)], axis=0)


def _online_update(x, v, m, l, acc):
    m_new = jnp.maximum(m, jnp.max(x, axis=1, keepdims=True))
    p = jnp.exp2(x - m_new)
    alpha = jnp.exp2(m - m_new)
    l = alpha * l + jnp.sum(p, axis=1, keepdims=True)
    acc = alpha * acc + _dot(p.astype(BF16), v)
    return m_new, l, acc


def _banded_kernel(*refs, nd, has_sink, gated):
    refs = list(refs)
    q_ref, k_ref, v_ref, tbl_ref = refs[:4]
    rest = refs[4:]
    sink_ref = rest.pop(0) if has_sink else None
    z_ref = rest.pop(0) if gated else None
    (o_ref,) = rest
    i = pl.program_id(2)
    q4 = _stack_heads(q_ref[...])
    xs, vs = [], []
    for d in range(nd):
        kb = i - d
        start = pl.multiple_of(jnp.maximum(kb, 0) * LANES, LANES)
        xs.append(_dot_nt(q4, k_ref[pl.ds(start, LANES), :]) + _group_table(tbl_ref, jnp.where(kb >= 0, d, nd)))
        vs.append(v_ref[pl.ds(start, LANES), :])
    m = functools.reduce(jnp.maximum, [jnp.max(x, axis=1, keepdims=True) for x in xs])
    if has_sink:
        m = jnp.maximum(m, sink_ref[...])
    ps = [jnp.exp2(x - m) for x in xs]
    l = sum(jnp.sum(p, axis=1, keepdims=True) for p in ps)
    if has_sink:
        l = l + jnp.exp2(sink_ref[...] - m)
    acc = sum(_dot(p.astype(BF16), v) for p, v in zip(ps, vs))
    o = _unstack_heads(acc / jnp.maximum(l, 1e-30), LANES)
    if gated:
        o = o * _silu(z_ref[...].astype(F32))
    o_ref[...] = o.astype(o_ref.dtype)


def _banded_gqa(h, batch, seq, qblk, kblk, vblk, tbl, sinks=None, zblk=None, name="banded"):
    nq = seq // LANES
    nd = tbl.shape[1] - 1
    in_specs = [pl.BlockSpec((LANES, GROUP_W), lambda b, g, i: (b * nq + i, qblk + g)),
                pl.BlockSpec((seq, HEAD_DIM), lambda b, g, i: (b, kblk + g)),
                pl.BlockSpec((seq, HEAD_DIM), lambda b, g, i: (b, vblk + g)),
                pl.BlockSpec((HPG, nd + 1, LANES, LANES), lambda b, g, i: (g, 0, 0, 0))]
    args = [h, h, h, tbl]
    if sinks is not None:
        sink_col = jnp.repeat(sinks.astype(F32).reshape(N_KV, HPG) * LOG2E, LANES, axis=1)
        sink_col = sink_col.reshape(N_KV, HPG * LANES, 1)
        in_specs.append(pl.BlockSpec((None, HPG * LANES, 1), lambda b, g, i: (g, 0, 0)))
        args.append(sink_col)
    if zblk is not None:
        in_specs.append(pl.BlockSpec((LANES, GROUP_W), lambda b, g, i: (b * nq + i, zblk + g)))
        args.append(h)
    return pl.pallas_call(
        functools.partial(_banded_kernel, nd=nd, has_sink=sinks is not None, gated=zblk is not None),
        grid=(batch, N_KV, nq),
        in_specs=in_specs,
        out_specs=pl.BlockSpec((LANES, GROUP_W), lambda b, g, i: (b * nq + i, g)),
        out_shape=jax.ShapeDtypeStruct((batch * seq, WIDTH), BF16),
        compiler_params=_cparams("arbitrary", "arbitrary", "arbitrary"),
        name=name,
    )(*args)


def _compress_kernel(kc_ref, vc_ref, pek_ref, w1k_ref, w2k_ref, pev_ref, w1v_ref, w2v_ref,
                     ko_ref, vo_ref, xs_ref, *, nc):
    half = CMP_LEN // 2
    for x_ref, pe_ref, w1_ref, w2_ref, o_ref in ((kc_ref, pek_ref, w1k_ref, w2k_ref, ko_ref),
                                                 (vc_ref, pev_ref, w1v_ref, w2v_ref, vo_ref)):
        xs_ref[...] = x_ref[...].astype(F32)
        first = jnp.zeros((nc, HEAD_DIM), F32)
        second = jnp.zeros((nc, HEAD_DIM), F32)
        for l in range(half):
            xl = xs_ref[pl.ds(l, nc, stride=CMP_STRIDE), :]
            first += _dot((xl + pe_ref[l:l + 1, :]).astype(BF16), w1_ref[l])
            second += _dot((xl + pe_ref[half + l:half + l + 1, :]).astype(BF16), w1_ref[half + l])
        hid = _silu(first + pltpu.roll(second, nc - 1, 0))
        o_ref[...] = _dot(hid.astype(BF16), w2_ref[...]).astype(o_ref.dtype)


def _nsa_compress(h, batch, seq, kcblk, vcblk, pe_k, w1_k, w2_k, pe_v, w1_v, w2_v):
    assert CMP_LEN == 2 * CMP_STRIDE
    nc = seq // CMP_STRIDE
    kv_spec = lambda blk: pl.BlockSpec((seq, HEAD_DIM), lambda b, g: (b, blk + g))
    full = lambda shape: pl.BlockSpec(shape, lambda b, g: (0,) * len(shape))
    out_spec = pl.BlockSpec((None, None, nc, HEAD_DIM), lambda b, g: (b, g, 0, 0))
    out_shape = jax.ShapeDtypeStruct((batch, N_KV, nc, HEAD_DIM), BF16)
    return pl.pallas_call(
        functools.partial(_compress_kernel, nc=nc),
        grid=(batch, N_KV),
        in_specs=[kv_spec(kcblk), kv_spec(vcblk),
                  full((CMP_LEN, HEAD_DIM)), full((CMP_LEN, HEAD_DIM, HEAD_DIM)), full((HEAD_DIM, HEAD_DIM)),
                  full((CMP_LEN, HEAD_DIM)), full((CMP_LEN, HEAD_DIM, HEAD_DIM)), full((HEAD_DIM, HEAD_DIM))],
        out_specs=[out_spec, out_spec],
        out_shape=[out_shape, out_shape],
        scratch_shapes=[pltpu.VMEM((seq, HEAD_DIM), F32)],
        compiler_params=_cparams("arbitrary", "arbitrary"),
        name="nsa_compress",
    )(h, h, pe_k, w1_k.astype(BF16), w2_k.astype(BF16), pe_v, w1_v.astype(BF16), w2_v.astype(BF16))


def _cmp_table(rel_bias, nc):
    u = np.arange(2 * nc)[:, None] - nc
    dist = np.arange(LANES)[None, :] - (u * CMP_STRIDE + CMP_LEN - 1)
    return _rel_table(rel_bias, dist, dist >= 0)


def _inter_t(seq, nc):
    nb = seq // SEL_LEN
    cstart = np.arange(nc) * CMP_STRIDE
    sstart = np.arange(nb) * SEL_LEN
    inter = np.clip(np.minimum(cstart[None, :] + CMP_LEN, sstart[:, None] + SEL_LEN)
                    - np.maximum(cstart[None, :], sstart[:, None]), 0, None) / CMP_LEN
    return inter.astype(np.float32)


def _cmp_select_kernel(q_ref, kc_ref, vc_ref, tbl_ref, inter_ref, o_ref, sel_ref, *, nb, nc):
    i = pl.program_id(1)
    q = q_ref[...]
    kc = kc_ref[...]
    vc = vc_ref[...]
    row0 = pl.multiple_of(nc - i * (LANES // CMP_STRIDE), LANES // CMP_STRIDE)
    outs = []
    p_sum_t = None
    for r in range(HPG):
        qr = q[:, r * HEAD_DIM:(r + 1) * HEAD_DIM]
        tbl = tbl_ref[r, pl.ds(row0, nc), :]
        st = _dot_nt(kc, qr) + tbl
        pt = jnp.where(tbl > 0.5 * NEG_INF, jnp.exp2(st - jnp.max(st, axis=0, keepdims=True)), 0.0)
        pt = pt / jnp.maximum(jnp.sum(pt, axis=0, keepdims=True), 1e-30)
        p_sum_t = pt if p_sum_t is None else p_sum_t + pt
        outs.append(_dot(pt.T.astype(BF16), vc))
    o_ref[...] = jnp.concatenate(outs, axis=1).astype(o_ref.dtype)

    inter = inter_ref[...]
    imp = sum(_dot(inter, part) for part in _split3(p_sum_t))
    blk = lax.broadcasted_iota(jnp.int32, (nb, LANES), 0)
    pos = i * LANES + lax.broadcasted_iota(jnp.int32, (nb, LANES), 1)
    cur = lax.shift_right_logical(pos, int(math.log2(SEL_LEN)))
    allowed = blk * SEL_LEN <= pos
    forced = (blk == 0) | (blk == cur) | (blk == cur - 1)
    imp = jnp.where(allowed, jnp.where(forced, FORCED_SCORE, imp), NEG_INF)
    rank = jnp.zeros((nb, LANES), F32)
    for c in range(nb):
        row = imp[c:c + 1, :]
        earlier = jnp.where(blk > c, 1.0, 0.0)
        rank += jnp.where(row > imp, 1.0, jnp.where(row == imp, earlier, 0.0))
    sel_t = jnp.where((rank < SEL_TOPK) & (imp > 0.5 * NEG_INF), 1.0, 0.0)
    sel_t = jnp.concatenate([sel_t, jnp.zeros((LANES - nb, LANES), F32)], axis=0)
    sel_ref[...] = sel_t.T.astype(sel_ref.dtype)


def _nsa_cmp_select(h, batch, seq, k_cmp, v_cmp, rel_bias):
    nq = seq // LANES
    nc = seq // CMP_STRIDE
    nb = seq // SEL_LEN
    assert nc == LANES and nb <= LANES
    tbl = _cmp_table(rel_bias, nc)
    inter = jnp.asarray(_inter_t(seq, nc), BF16)
    cmp_spec = pl.BlockSpec((None, None, nc, HEAD_DIM), lambda g, i, b: (b, g, 0, 0))
    return pl.pallas_call(
        functools.partial(_cmp_select_kernel, nb=nb, nc=nc),
        grid=(N_KV, nq, batch),
        in_specs=[pl.BlockSpec((LANES, GROUP_W), lambda g, i, b: (b * nq + i, g)),
                  cmp_spec, cmp_spec,
                  pl.BlockSpec((HPG, 2 * nc, LANES), lambda g, i, b: (g, 0, 0)),
                  pl.BlockSpec((nb, nc), lambda g, i, b: (0, 0))],
        out_specs=[pl.BlockSpec((LANES, GROUP_W), lambda g, i, b: (b * nq + i, g)),
                   pl.BlockSpec((None, None, LANES, LANES), lambda g, i, b: (b, g, i, 0))],
        out_shape=[jax.ShapeDtypeStruct((batch * seq, WIDTH), BF16),
                   jax.ShapeDtypeStruct((batch, N_KV, seq, LANES), BF16)],
        compiler_params=_cparams("arbitrary", "arbitrary", "arbitrary"),
        name="nsa_cmp_select",
    )(h, k_cmp, v_cmp, tbl, inter)


def _expand_np(seq):
    nk = seq // LANES
    j = np.arange(LANES)[None, :, None]
    key = np.arange(nk)[:, None, None] * LANES + np.arange(LANES)[None, None, :]
    return (j == key // SEL_LEN).astype(np.float32)


def _sel_attn_kernel(q_ref, k_ref, v_ref, sel_ref, tbl_ref, exp_ref, o_ref):
    i = pl.program_id(2)
    rows = HPG * LANES
    q4 = _stack_heads(q_ref[...])
    blocked = ((1.0 - sel_ref[...].astype(F32)) * NEG_INF).astype(BF16)
    far = tbl_ref.shape[1] - 1

    def scores(kb):
        start = pl.multiple_of(kb * LANES, LANES)
        return _dot_nt(q4, k_ref[pl.ds(start, LANES), :])

    def body(kb, carry):
        s, m, l, acc = carry
        s_next = scores(jnp.minimum(kb + 1, i))
        start = pl.multiple_of(kb * LANES, LANES)
        v = v_ref[pl.ds(start, LANES), :]
        unselected = _dot(blocked, exp_ref[kb])
        x = s + _group_table(tbl_ref, jnp.minimum(i - kb, far)) + jnp.concatenate([unselected] * HPG, axis=0)
        return (s_next,) + _online_update(x, v, m, l, acc)

    init = (scores(0), jnp.full((rows, 1), M_INIT, F32), jnp.zeros((rows, 1), F32),
            jnp.zeros((rows, HEAD_DIM), F32))
    _, m, l, acc = lax.fori_loop(0, i + 1, body, init)
    o_ref[...] = _unstack_heads(acc / jnp.maximum(l, 1e-30), LANES).astype(o_ref.dtype)


def _nsa_sel_attn(h, batch, seq, qblk, kblk, vblk, sel, rel_bias):
    nq = seq // LANES
    tbl = _band_tables(rel_bias, 3, None)
    expand = jnp.asarray(_expand_np(seq), BF16)
    return pl.pallas_call(
        _sel_attn_kernel,
        grid=(batch, N_KV, nq),
        in_specs=[pl.BlockSpec((LANES, GROUP_W), lambda b, g, i: (b * nq + i, qblk + g)),
                  pl.BlockSpec((seq, HEAD_DIM), lambda b, g, i: (b, kblk + g)),
                  pl.BlockSpec((seq, HEAD_DIM), lambda b, g, i: (b, vblk + g)),
                  pl.BlockSpec((None, None, LANES, LANES), lambda b, g, i: (b, g, i, 0)),
                  pl.BlockSpec((HPG, 3, LANES, LANES), lambda b, g, i: (g, 0, 0, 0)),
                  pl.BlockSpec((nq, LANES, LANES), lambda b, g, i: (0, 0, 0))],
        out_specs=pl.BlockSpec((LANES, GROUP_W), lambda b, g, i: (b * nq + i, g)),
        out_shape=jax.ShapeDtypeStruct((batch * seq, WIDTH), BF16),
        compiler_params=_cparams("arbitrary", "arbitrary", "arbitrary"),
        name="nsa_sel_attn",
    )(h, h, h, sel, tbl, expand)


def _combine_kernel(oc_ref, os_ref, ow_ref, gl_ref, z_ref, o_ref):
    gates = 1.0 / (1.0 + jnp.exp(-gl_ref[...]))
    oc = oc_ref[...].astype(F32)
    osel = os_ref[...].astype(F32)
    ow = ow_ref[...].astype(F32)
    z = z_ref[...].astype(F32)
    outs = []
    for r in range(HPG):
        sl = slice(r * HEAD_DIM, (r + 1) * HEAD_DIM)
        o = (gates[:, 3 * r:3 * r + 1] * oc[:, sl] + gates[:, 3 * r + 1:3 * r + 2] * osel[:, sl]
             + gates[:, 3 * r + 2:3 * r + 3] * ow[:, sl])
        outs.append(o * _silu(z[:, sl]))
    o_ref[...] = jnp.concatenate(outs, axis=1).astype(o_ref.dtype)


def _nsa_combine(o_cmp, o_sel, o_win, gl, h, zblk, tm=512):
    m = o_cmp.shape[0]
    grp = pl.BlockSpec((tm, GROUP_W), lambda i, g: (i, g))
    return pl.pallas_call(
        _combine_kernel,
        grid=(m // tm, N_KV),
        in_specs=[grp, grp, grp,
                  pl.BlockSpec((tm, LANES), lambda i, g: (i, g)),
                  pl.BlockSpec((tm, GROUP_W), lambda i, g: (i, zblk + g))],
        out_specs=grp,
        out_shape=jax.ShapeDtypeStruct((m, WIDTH), BF16),
        compiler_params=_cparams("arbitrary", "arbitrary"),
        name="nsa_combine",
    )(o_cmp, o_sel, o_win, gl, h)


def _sb_kernel(q_ref, k_ref, v_ref, z_ref, o_ref, *, t):
    i = pl.program_id(2)
    q = q_ref[...]
    row = lax.broadcasted_iota(jnp.int32, (t, t), 0)
    col = lax.broadcasted_iota(jnp.int32, (t, t), 1)
    later = jnp.where(row > col, 1.0, 0.0).astype(BF16)
    strict = col < row

    def scores(kb):
        start = pl.multiple_of(kb * t, t)
        return _dot_nt(q, k_ref[pl.ds(start, t), :])

    def consume(y, kb, run, acc, diag):
        start = pl.multiple_of(kb * t, t)
        v = v_ref[pl.ds(start, t), :]
        soft = jnp.log(1.0 + jnp.exp2(-jnp.abs(y))) * LOG2E
        log_keep = jnp.minimum(-y, 0.0) - soft
        log_beta = y + log_keep
        if diag:
            log_keep = jnp.where(strict, log_keep, 0.0)
        after = _dot(log_keep.astype(BF16), later) + run
        a = jnp.exp2(log_beta + after)
        if diag:
            a = jnp.where(strict, a, 0.0)
        acc = acc + _dot(a.astype(BF16), v)
        run = run + jnp.sum(log_keep, axis=1, keepdims=True)
        return run, acc

    y_next = scores(jnp.maximum(i - 1, 0))
    run, acc = consume(scores(i), i, jnp.zeros((t, 1), F32), jnp.zeros((t, HEAD_DIM), F32), True)

    def body(step, carry):
        y, run, acc = carry
        kb = i - step
        y_next = scores(jnp.maximum(kb - 1, 0))
        return (y_next,) + consume(y, kb, run, acc, False)

    _, _, acc = lax.fori_loop(1, i + 1, body, (y_next, run, acc))
    o_ref[...] = (acc * _silu(z_ref[...].astype(F32))).astype(o_ref.dtype)


def _full_attn_call(kernel, h, batch, seq, t, extra_specs, extra_args, name):
    nq = seq // t
    in_specs = [pl.BlockSpec((t, HEAD_DIM), lambda b, hh, i: (b * nq + i, hh)),
                pl.BlockSpec((seq, HEAD_DIM), lambda b, hh, i: (b, N_HEADS + hh)),
                pl.BlockSpec((seq, HEAD_DIM), lambda b, hh, i: (b, 2 * N_HEADS + hh)),
                pl.BlockSpec((t, HEAD_DIM), lambda b, hh, i: (b * nq + i, 3 * N_HEADS + hh))]
    return pl.pallas_call(
        functools.partial(kernel, t=t),
        grid=(batch, N_HEADS, nq),
        in_specs=in_specs + extra_specs,
        out_specs=pl.BlockSpec((t, HEAD_DIM), lambda b, hh, i: (b * nq + i, hh)),
        out_shape=jax.ShapeDtypeStruct((batch * seq, WIDTH), BF16),
        compiler_params=_cparams("arbitrary", "arbitrary", "arbitrary"),
        name=name,
    )(h, h, h, h, *extra_args)


def _fox_prep_kernel(fl_ref, bias_ref, ct_ref, *, seq):
    row = lax.broadcasted_iota(jnp.int32, (LANES, LANES), 0)
    col = lax.broadcasted_iota(jnp.int32, (LANES, LANES), 1)
    upto = jnp.where(row <= col, 1.0, 0.0).astype(BF16)
    carry = jnp.zeros((LANES, 1), F32)
    for blk in range(seq // LANES):
        x = fl_ref[blk * LANES:(blk + 1) * LANES, :] + bias_ref[...]
        log_f = (jnp.minimum(x, 0.0) - jnp.log1p(jnp.exp(-jnp.abs(x)))).T
        cs = sum(_dot(part, upto) for part in _split3(log_f)) + carry
        ct_ref[:, blk * LANES:(blk + 1) * LANES] = cs * LOG2E
        carry = cs[:, LANES - 1:LANES]


def _fox_prep(fl, fgate_bias, batch, seq):
    bias = jnp.zeros((1, LANES), F32).at[0, :N_HEADS].set(fgate_bias.astype(F32))
    return pl.pallas_call(
        functools.partial(_fox_prep_kernel, seq=seq),
        grid=(batch,),
        in_specs=[pl.BlockSpec((seq, LANES), lambda b: (b, 0)),
                  pl.BlockSpec((1, LANES), lambda b: (0, 0))],
        out_specs=pl.BlockSpec((None, LANES, seq), lambda b: (b, 0, 0)),
        out_shape=jax.ShapeDtypeStruct((batch, LANES, seq), F32),
        compiler_params=_cparams("arbitrary"),
        name="fox_prep",
    )(fl, bias)


def _fox_kernel(q_ref, k_ref, v_ref, z_ref, ccol_ref, crow_ref, o_ref, *, t):
    i = pl.program_id(2)
    q = q_ref[...]
    c_q = ccol_ref[i]
    row = lax.broadcasted_iota(jnp.int32, (t, t), 0)
    col = lax.broadcasted_iota(jnp.int32, (t, t), 1)
    causal = col <= row

    def scores(kb):
        start = pl.multiple_of(kb * t, t)
        return _dot_nt(q, k_ref[pl.ds(start, t), :])

    def consume(s, kb, m, l, acc, diag):
        start = pl.multiple_of(kb * t, t)
        v = v_ref[pl.ds(start, t), :]
        x = s - crow_ref[kb]
        if diag:
            x = jnp.where(causal, x, NEG_INF)
        m_new = jnp.maximum(m, jnp.max(x, axis=1, keepdims=True) + c_q)
        p = jnp.exp2(x + (c_q - m_new))
        alpha = jnp.exp2(m - m_new)
        l = alpha * l + jnp.sum(p, axis=1, keepdims=True)
        acc = alpha * acc + _dot(p.astype(BF16), v)
        return m_new, l, acc

    s_next = scores(jnp.maximum(i - 1, 0))
    carry = consume(scores(i), i, jnp.full((t, 1), M_INIT, F32), jnp.zeros((t, 1), F32),
                    jnp.zeros((t, HEAD_DIM), F32), True)

    def body(step, carry):
        s, m, l, acc = carry
        kb = i - step
        s_next = scores(jnp.maximum(kb - 1, 0))
        return (s_next,) + consume(s, kb, m, l, acc, False)

    _, _, l, acc = lax.fori_loop(1, i + 1, body, (s_next,) + carry)
    o = acc / jnp.maximum(l, 1e-30)
    o_ref[...] = (o * _silu(z_ref[...].astype(F32))).astype(o_ref.dtype)


def _w_cols(w, *ranges):
    cols = [w[:, :WIDTH] * QSCALE] + [w[:, a:b] for a, b in ranges]
    return jnp.concatenate(cols, axis=1).astype(BF16)


def _pad_cols(w, n):
    return jnp.pad(w, ((0, 0), (0, n - w.shape[1]))).astype(BF16)


def _layer_nsa(xb, batch, seq, rel_bias, w_in, cmp_pe_k, cmp_w1_k, cmp_w2_k, cmp_pe_v, cmp_w1_v, cmp_w2_v):
    gate0 = WIDTH + 6 * KV_WIDTH
    gate1 = gate0 + 3 * N_HEADS
    h = _matmul(xb, _w_cols(w_in, (WIDTH, gate0), (gate1, w_in.shape[1])), BF16, 512, 1024, "proj_nsa")
    per_group = 3 * HPG
    w_gate = w_in[:, gate0:gate1].reshape(-1, N_KV, per_group)
    w_gate = jnp.pad(w_gate, ((0, 0), (0, 0), (0, LANES - per_group))).reshape(-1, N_KV * LANES)
    gl = _matmul(xb, w_gate.astype(BF16), F32, 512, N_KV * LANES, "proj_nsa_gates")
    kv0 = WIDTH // HEAD_DIM
    blk = lambda j: kv0 + j * N_KV
    k_cmp, v_cmp = _nsa_compress(h, batch, seq, blk(0), blk(1), cmp_pe_k, cmp_w1_k, cmp_w2_k,
                                 cmp_pe_v, cmp_w1_v, cmp_w2_v)
    o_cmp, sel = _nsa_cmp_select(h, batch, seq, k_cmp, v_cmp, rel_bias)
    o_sel = _nsa_sel_attn(h, batch, seq, 0, blk(2), blk(3), sel, rel_bias)
    nd = NSA_WINDOW // LANES + 1
    o_win = _banded_gqa(h, batch, seq, 0, blk(4), blk(5), _band_tables(rel_bias, nd, NSA_WINDOW, True),
                        name="nsa_window")
    zblk = (WIDTH + 6 * KV_WIDTH) // GROUP_W
    return _nsa_combine(o_cmp, o_sel, o_win, gl, h, zblk)


def _layer_swa(xb, batch, seq, rel_bias, w_in, sinks):
    h = _matmul(xb, _w_cols(w_in, (WIDTH, w_in.shape[1])), BF16, 512, 1024, "proj_swa")
    kv0 = WIDTH // HEAD_DIM
    nd = SWA_WINDOW // LANES + 1
    return _banded_gqa(h, batch, seq, 0, kv0, kv0 + N_KV, _band_tables(rel_bias, nd, SWA_WINDOW, True),
                       sinks=sinks, zblk=(WIDTH + 2 * KV_WIDTH) // GROUP_W, name="swa")


def _layer_sb(xb, batch, seq, w_in, t=256):
    h = _matmul(xb, _w_cols(w_in, (WIDTH, w_in.shape[1])), BF16, 512, 1024, "proj_sb")
    return _full_attn_call(_sb_kernel, h, batch, seq, t, [], [], "stick_breaking")


def _layer_fox(xb, batch, seq, w_in, fgate_bias, t=256):
    f0 = 3 * WIDTH
    f1 = f0 + N_HEADS
    h = _matmul(xb, _w_cols(w_in, (WIDTH, f0), (f1, w_in.shape[1])), BF16, 512, 1024, "proj_fox")
    fl = _matmul(xb, _pad_cols(w_in[:, f0:f1], LANES), F32, 512, LANES, "proj_fox_gates")
    ct = _fox_prep(fl, fgate_bias, batch, seq)
    nq = seq // t
    ccol = ct.reshape(batch, LANES, nq, t, 1)
    crow = ct.reshape(batch, LANES, nq, 1, t)
    extra_specs = [pl.BlockSpec((None, None, nq, t, 1), lambda b, hh, i: (b, hh, 0, 0, 0)),
                   pl.BlockSpec((None, None, nq, 1, t), lambda b, hh, i: (b, hh, 0, 0, 0))]
    return _full_attn_call(_fox_kernel, h, batch, seq, t, extra_specs, [ccol, crow], "forgetting")


def kernel(x, rel_bias, w_in_a, w_out_a, ln_g_a, ln_b_a, cmp_pe_k, cmp_w1_k, cmp_w2_k, cmp_pe_v, cmp_w1_v, cmp_w2_v, w_in_b, w_out_b, ln_g_b, ln_b_b, sinks_b, w_in_c, w_out_c, ln_g_c, ln_b_c, w_in_d, w_out_d, ln_g_d, ln_b_d, fgate_bias_d):
    batch, seq, d_model = x.shape
    xf = x.reshape(batch * seq, d_model)
    xb = xf.astype(BF16)
    og = _layer_nsa(xb, batch, seq, rel_bias, w_in_a, cmp_pe_k, cmp_w1_k, cmp_w2_k, cmp_pe_v, cmp_w1_v, cmp_w2_v)
    xf, xb = _out_proj_ln(og, w_out_a, xf, ln_g_a, ln_b_a)
    og = _layer_swa(xb, batch, seq, rel_bias, w_in_b, sinks_b)
    xf, xb = _out_proj_ln(og, w_out_b, xf, ln_g_b, ln_b_b)
    og = _layer_sb(xb, batch, seq, w_in_c)
    xf, xb = _out_proj_ln(og, w_out_c, xf, ln_g_c, ln_b_c)
    og = _layer_fox(xb, batch, seq, w_in_d, fgate_bias_d)
    xf, xb = _out_proj_ln(og, w_out_d, xf, ln_g_d, ln_b_d)
    return xf.reshape(batch, seq, d_model)
```

```python
import functools
import math

import numpy as np
import jax
import jax.numpy as jnp
from jax import lax
from jax.experimental import pallas as pl
from jax.experimental.pallas import tpu as pltpu

F32 = jnp.float32
BF16 = jnp.bfloat16

N_HEADS = 16
HEAD_DIM = 128
N_KV = 4
HPG = N_HEADS // N_KV
WIDTH = N_HEADS * HEAD_DIM
KV_WIDTH = N_KV * HEAD_DIM
GROUP_W = HPG * HEAD_DIM
REL_BUCKETS = 32
REL_MAX_DIST = 128
CMP_LEN = 32
CMP_STRIDE = 16
SEL_LEN = 64
SEL_TOPK = 8
NSA_WINDOW = 512
SWA_WINDOW = 128
DEPTH = 4
DN_ALPHA = (2 * DEPTH) ** 0.25
LN_EPS = 1e-5
NEG_INF = -1e30
FORCED_SCORE = 1e4
SCALE = HEAD_DIM ** -0.5
LOG2E = math.log2(math.e)
QSCALE = SCALE * LOG2E

LANES = 128
BF16_ROWS = 16
M_INIT = -1e29
VMEM_LIMIT = 56 * 1024 * 1024


def _cparams(*sem):
    return pltpu.CompilerParams(dimension_semantics=sem, vmem_limit_bytes=VMEM_LIMIT)


def _silu(z):
    return z / (1.0 + jnp.exp(-z))


def _dot_nt(a, b):
    return lax.dot_general(a, b, (((1,), (1,)), ((), ())), preferred_element_type=F32)


def _dot(a, b):
    return jnp.dot(a, b, preferred_element_type=F32)


def _group_queries(qt):
    return jnp.concatenate([qt[r * HEAD_DIM:(r + 1) * HEAD_DIM] for r in range(HPG)], axis=1)


def _split3(x):
    a1 = x.astype(BF16)
    r1 = x - a1.astype(F32)
    a2 = r1.astype(BF16)
    a3 = (r1 - a2.astype(F32)).astype(BF16)
    return a1, a2, a3


def _mm_kernel(x_ref, w_ref, o_ref):
    o_ref[...] = _dot(x_ref[...], w_ref[...]).astype(o_ref.dtype)


def _mm_nt_kernel(wt_ref, x_ref, o_ref):
    o_ref[...] = _dot_nt(wt_ref[...], x_ref[...]).astype(o_ref.dtype)


def _matmul_nt(wt, x, name, tm=512):
    n, k = wt.shape
    m = x.shape[0]
    tn = _feature_tile(n)
    assert m % tm == 0 and n % tn == 0
    return pl.pallas_call(
        _mm_nt_kernel,
        grid=(n // tn, m // tm),
        in_specs=[pl.BlockSpec((tn, k), lambda j, i: (j, 0)),
                  pl.BlockSpec((tm, k), lambda j, i: (i, 0))],
        out_specs=pl.BlockSpec((tn, tm), lambda j, i: (j, i)),
        out_shape=jax.ShapeDtypeStruct((n, m), BF16),
        compiler_params=_cparams("arbitrary", "arbitrary"),
        name=name,
    )(wt, x)


def _feature_tile(n):
    return next(t for t in (1024, 1280, 1536, 768, 512, 256, 128) if n % t == 0)


def _matmul(x, w, out_dtype, tm, tn, name):
    m, k = x.shape
    n = w.shape[1]
    assert m % tm == 0 and n % tn == 0
    return pl.pallas_call(
        _mm_kernel,
        grid=(n // tn, m // tm),
        in_specs=[pl.BlockSpec((tm, k), lambda j, i: (i, 0)),
                  pl.BlockSpec((k, tn), lambda j, i: (0, j))],
        out_specs=pl.BlockSpec((tm, tn), lambda j, i: (i, j)),
        out_shape=jax.ShapeDtypeStruct((m, n), out_dtype),
        compiler_params=_cparams("arbitrary", "arbitrary"),
        name=name,
    )(x, w)


def _outln_kernel(og_ref, w_ref, x_ref, g_ref, b_ref, xo_ref, xb_ref):
    y = _dot(og_ref[...], w_ref[...])
    t = DN_ALPHA * x_ref[...] + y
    mu = jnp.mean(t, axis=-1, keepdims=True)
    d = t - mu
    var = jnp.mean(d * d, axis=-1, keepdims=True)
    out = d * lax.rsqrt(var + LN_EPS) * g_ref[...] + b_ref[...]
    xo_ref[...] = out
    xb_ref[...] = out.astype(BF16)


def _out_proj_ln(og, w_out, x, ln_g, ln_b, tm=512):
    m, k = og.shape
    n = w_out.shape[1]
    return pl.pallas_call(
        _outln_kernel,
        grid=(m // tm,),
        in_specs=[pl.BlockSpec((tm, k), lambda i: (i, 0)),
                  pl.BlockSpec((k, n), lambda i: (0, 0)),
                  pl.BlockSpec((tm, n), lambda i: (i, 0)),
                  pl.BlockSpec((1, n), lambda i: (0, 0)),
                  pl.BlockSpec((1, n), lambda i: (0, 0))],
        out_specs=[pl.BlockSpec((tm, n), lambda i: (i, 0)),
                   pl.BlockSpec((tm, n), lambda i: (i, 0))],
        out_shape=[jax.ShapeDtypeStruct((m, n), F32), jax.ShapeDtypeStruct((m, n), BF16)],
        compiler_params=_cparams("arbitrary"),
        name="out_proj_ln",
    )(og, w_out.astype(BF16), x, ln_g.reshape(1, n), ln_b.reshape(1, n))


def _bucket_np(dist):
    max_exact = REL_BUCKETS // 2
    ratio = np.maximum(dist, max_exact).astype(np.float32) / max_exact
    large = max_exact + (np.log(ratio) / math.log(REL_MAX_DIST / max_exact)
                         * (REL_BUCKETS - max_exact)).astype(np.int32)
    return np.where(dist < max_exact, dist, np.minimum(large, REL_BUCKETS - 1))


def _rel_table(rel_bias, dist, ok):
    bucket = _bucket_np(np.maximum(dist, 0)).reshape(-1)
    onehot_t = np.zeros((REL_BUCKETS, bucket.size), np.float32)
    onehot_t[bucket, np.arange(bucket.size)] = 1.0
    t = jnp.dot(rel_bias.T.astype(F32) * LOG2E, jnp.asarray(onehot_t), precision=lax.Precision.HIGHEST)
    blocked = np.where(ok.reshape(-1), 0.0, NEG_INF).astype(np.float32)
    return (t + blocked[None, :]).reshape((N_HEADS,) + dist.shape)


def _band_tables(rel_bias, nd, window, with_blocked=False):
    key = np.arange(LANES)[:, None]
    qry = np.arange(LANES)[None, :]
    dist = np.stack([LANES * d + qry - key for d in range(nd)])
    ok = dist >= 0
    if window is not None:
        ok &= dist < window
    if with_blocked:
        dist = np.concatenate([dist, np.zeros((1, LANES, LANES), dist.dtype)])
        ok = np.concatenate([ok, np.zeros((1, LANES, LANES), bool)])
    return _rel_table(rel_bias, dist, ok)


def _group_table(tbl_ref, idx):
    return jnp.concatenate([tbl_ref[r, idx] for r in range(HPG)], axis=1)


def _with_ones_rows(vt):
    return jnp.concatenate([vt, jnp.ones((BF16_ROWS, vt.shape[1]), BF16)], axis=0)


def _finish_group(acc):
    o_t = acc[:HEAD_DIM] / jnp.maximum(acc[HEAD_DIM:HEAD_DIM + 1], 1e-30)
    return jnp.concatenate([o_t[:, r * LANES:(r + 1) * LANES].T for r in range(HPG)], axis=1)


def _online_update(x, vt, m, acc):
    m_new = jnp.maximum(m, jnp.max(x, axis=0, keepdims=True))
    p = jnp.exp2(x - m_new)
    acc = jnp.exp2(m - m_new) * acc + _dot(_with_ones_rows(vt), p.astype(BF16))
    return m_new, acc


def _banded_kernel(*refs, nd, has_sink, gated):
    refs = list(refs)
    qt_ref, k_ref, vt_ref, tbl_ref = refs[:4]
    rest = refs[4:]
    sink_ref = rest.pop(0) if has_sink else None
    z_ref = rest.pop(0) if gated else None
    (o_ref,) = rest
    i = pl.program_id(2)
    ks, vts, tbls = [], [], []
    for d in range(nd):
        kb = i - d
        start = pl.multiple_of(jnp.maximum(kb, 0) * LANES, LANES)
        ks.append(k_ref[pl.ds(start, LANES), :])
        vts.append(vt_ref[:, pl.ds(start, LANES)])
        tbls.append(_group_table(tbl_ref, jnp.where(kb >= 0, d, nd)))
    x = _dot(jnp.concatenate(ks, axis=0), _group_queries(qt_ref[...])) + jnp.concatenate(tbls, axis=0)
    m = jnp.max(x, axis=0, keepdims=True)
    if has_sink:
        m = jnp.maximum(m, sink_ref[...])
    p = jnp.exp2(x - m)
    acc = _dot(_with_ones_rows(jnp.concatenate(vts, axis=1)), p.astype(BF16))
    if has_sink:
        rows = lax.broadcasted_iota(jnp.int32, acc.shape, 0)
        acc = acc + jnp.where(rows >= HEAD_DIM, jnp.exp2(sink_ref[...] - m), 0.0)
    o = _finish_group(acc)
    if gated:
        o = o * _silu(z_ref[...].astype(F32))
    o_ref[...] = o.astype(o_ref.dtype)


def _banded_gqa(h, qvt, batch, seq, kblk, vtblk, tbl, sinks=None, zblk=None, name="banded"):
    nq = seq // LANES
    nd = tbl.shape[1] - 1
    in_specs = [pl.BlockSpec((GROUP_W, LANES), lambda b, g, i: (g, b * nq + i)),
                pl.BlockSpec((seq, HEAD_DIM), lambda b, g, i: (b, kblk + g)),
                pl.BlockSpec((HEAD_DIM, seq), lambda b, g, i: (vtblk + g, b)),
                pl.BlockSpec((HPG, nd + 1, LANES, LANES), lambda b, g, i: (g, 0, 0, 0))]
    args = [qvt, h, qvt, tbl]
    if sinks is not None:
        sink_row = jnp.repeat(sinks.astype(F32).reshape(N_KV, HPG) * LOG2E, LANES, axis=1)
        in_specs.append(pl.BlockSpec((None, 1, HPG * LANES), lambda b, g, i: (g, 0, 0)))
        args.append(sink_row.reshape(N_KV, 1, HPG * LANES))
    if zblk is not None:
        in_specs.append(pl.BlockSpec((LANES, GROUP_W), lambda b, g, i: (b * nq + i, zblk + g)))
        args.append(h)
    return pl.pallas_call(
        functools.partial(_banded_kernel, nd=nd, has_sink=sinks is not None, gated=zblk is not None),
        grid=(batch, N_KV, nq),
        in_specs=in_specs,
        out_specs=pl.BlockSpec((LANES, GROUP_W), lambda b, g, i: (b * nq + i, g)),
        out_shape=jax.ShapeDtypeStruct((batch * seq, WIDTH), BF16),
        compiler_params=_cparams("arbitrary", "arbitrary", "arbitrary"),
        name=name,
    )(*args)


def _compress_kernel(kc_ref, vc_ref, pek_ref, w1k_ref, w2k_ref, pev_ref, w1v_ref, w2v_ref,
                     ko_ref, vo_ref, xs_ref, *, nc):
    half = CMP_LEN // 2
    for x_ref, pe_ref, w1_ref, w2_ref, o_ref in ((kc_ref, pek_ref, w1k_ref, w2k_ref, ko_ref),
                                                 (vc_ref, pev_ref, w1v_ref, w2v_ref, vo_ref)):
        xs_ref[...] = x_ref[...].astype(F32)
        first = jnp.zeros((nc, HEAD_DIM), F32)
        second = jnp.zeros((nc, HEAD_DIM), F32)
        for l in range(half):
            xl = xs_ref[pl.ds(l, nc, stride=CMP_STRIDE), :]
            first += _dot((xl + pe_ref[l:l + 1, :]).astype(BF16), w1_ref[l])
            second += _dot((xl + pe_ref[half + l:half + l + 1, :]).astype(BF16), w1_ref[half + l])
        hid = _silu(first + pltpu.roll(second, nc - 1, 0))
        o_ref[...] = _dot(hid.astype(BF16), w2_ref[...]).astype(o_ref.dtype)


def _nsa_compress(h, batch, seq, kcblk, vcblk, pe_k, w1_k, w2_k, pe_v, w1_v, w2_v):
    assert CMP_LEN == 2 * CMP_STRIDE
    nc = seq // CMP_STRIDE
    kv_spec = lambda blk: pl.BlockSpec((seq, HEAD_DIM), lambda b, g: (b, blk + g))
    full = lambda shape: pl.BlockSpec(shape, lambda b, g: (0,) * len(shape))
    out_spec = pl.BlockSpec((None, None, nc, HEAD_DIM), lambda b, g: (b, g, 0, 0))
    out_shape = jax.ShapeDtypeStruct((batch, N_KV, nc, HEAD_DIM), BF16)
    return pl.pallas_call(
        functools.partial(_compress_kernel, nc=nc),
        grid=(batch, N_KV),
        in_specs=[kv_spec(kcblk), kv_spec(vcblk),
                  full((CMP_LEN, HEAD_DIM)), full((CMP_LEN, HEAD_DIM, HEAD_DIM)), full((HEAD_DIM, HEAD_DIM)),
                  full((CMP_LEN, HEAD_DIM)), full((CMP_LEN, HEAD_DIM, HEAD_DIM)), full((HEAD_DIM, HEAD_DIM))],
        out_specs=[out_spec, out_spec],
        out_shape=[out_shape, out_shape],
        scratch_shapes=[pltpu.VMEM((seq, HEAD_DIM), F32)],
        compiler_params=_cparams("arbitrary", "arbitrary"),
        name="nsa_compress",
    )(h, h, pe_k, w1_k.astype(BF16), w2_k.astype(BF16), pe_v, w1_v.astype(BF16), w2_v.astype(BF16))


def _cmp_table(rel_bias, nc):
    u = np.arange(2 * nc)[:, None] - nc
    dist = np.arange(LANES)[None, :] - (u * CMP_STRIDE + CMP_LEN - 1)
    return _rel_table(rel_bias, dist, dist >= 0)


def _inter_t(seq, nc):
    nb = seq // SEL_LEN
    cstart = np.arange(nc) * CMP_STRIDE
    sstart = np.arange(nb) * SEL_LEN
    inter = np.clip(np.minimum(cstart[None, :] + CMP_LEN, sstart[:, None] + SEL_LEN)
                    - np.maximum(cstart[None, :], sstart[:, None]), 0, None) / CMP_LEN
    return inter.astype(np.float32)


def _cmp_select_kernel(qt_ref, kc_ref, vc_ref, tbl_ref, inter_ref, o_ref, sel_ref, *, nb, nc):
    i = pl.program_id(1)
    kc = kc_ref[...]
    vc = vc_ref[...]
    row0 = pl.multiple_of(nc - i * (LANES // CMP_STRIDE), LANES // CMP_STRIDE)
    outs = []
    p_sum_t = None
    for r in range(HPG):
        tbl = tbl_ref[r, pl.ds(row0, nc), :]
        st = _dot(kc, qt_ref[r * HEAD_DIM:(r + 1) * HEAD_DIM, :]) + tbl
        pt = jnp.where(tbl > 0.5 * NEG_INF, jnp.exp2(st - jnp.max(st, axis=0, keepdims=True)), 0.0)
        pt = pt / jnp.maximum(jnp.sum(pt, axis=0, keepdims=True), 1e-30)
        p_sum_t = pt if p_sum_t is None else p_sum_t + pt
        outs.append(_dot(pt.T.astype(BF16), vc))
    o_ref[...] = jnp.concatenate(outs, axis=1).astype(o_ref.dtype)

    inter = inter_ref[...]
    imp = sum(_dot(inter, part) for part in _split3(p_sum_t))
    blk = lax.broadcasted_iota(jnp.int32, (nb, LANES), 0)
    pos = i * LANES + lax.broadcasted_iota(jnp.int32, (nb, LANES), 1)
    cur = lax.shift_right_logical(pos, int(math.log2(SEL_LEN)))
    allowed = blk * SEL_LEN <= pos
    forced = (blk == 0) | (blk == cur) | (blk == cur - 1)
    imp = jnp.where(allowed, jnp.where(forced, FORCED_SCORE, imp), NEG_INF)
    rank = jnp.zeros((nb, LANES), F32)
    for c in range(nb):
        row = imp[c:c + 1, :]
        earlier = jnp.where(blk > c, 1.0, 0.0)
        rank += jnp.where(row > imp, 1.0, jnp.where(row == imp, earlier, 0.0))
    sel_t = jnp.where((rank < SEL_TOPK) & (imp > 0.5 * NEG_INF), 1.0, 0.0)
    sel_t = jnp.concatenate([sel_t, jnp.zeros((LANES - nb, LANES), F32)], axis=0)
    sel_ref[...] = sel_t.astype(sel_ref.dtype)


def _nsa_cmp_select(qvt, batch, seq, k_cmp, v_cmp, rel_bias):
    nq = seq // LANES
    nc = seq // CMP_STRIDE
    nb = seq // SEL_LEN
    assert nc == LANES and nb <= LANES
    tbl = _cmp_table(rel_bias, nc)
    inter = jnp.asarray(_inter_t(seq, nc), BF16)
    cmp_spec = pl.BlockSpec((None, None, nc, HEAD_DIM), lambda g, i, b: (b, g, 0, 0))
    return pl.pallas_call(
        functools.partial(_cmp_select_kernel, nb=nb, nc=nc),
        grid=(N_KV, nq, batch),
        in_specs=[pl.BlockSpec((GROUP_W, LANES), lambda g, i, b: (g, b * nq + i)),
                  cmp_spec, cmp_spec,
                  pl.BlockSpec((HPG, 2 * nc, LANES), lambda g, i, b: (g, 0, 0)),
                  pl.BlockSpec((nb, nc), lambda g, i, b: (0, 0))],
        out_specs=[pl.BlockSpec((LANES, GROUP_W), lambda g, i, b: (b * nq + i, g)),
                   pl.BlockSpec((None, None, None, LANES, LANES), lambda g, i, b: (b, g, i, 0, 0))],
        out_shape=[jax.ShapeDtypeStruct((batch * seq, WIDTH), BF16),
                   jax.ShapeDtypeStruct((batch, N_KV, nq, LANES, LANES), BF16)],
        compiler_params=_cparams("arbitrary", "arbitrary", "arbitrary"),
        name="nsa_cmp_select",
    )(qvt, k_cmp, v_cmp, tbl, inter)


def _expand_np(seq):
    nk = seq // LANES
    j = np.arange(LANES)[None, None, :]
    key = np.arange(nk)[:, None, None] * LANES + np.arange(LANES)[None, :, None]
    return (j == key // SEL_LEN).astype(np.float32)


def _sel_attn_kernel(qt_ref, k_ref, vt_ref, sel_ref, tbl_ref, exp_ref, o_ref):
    i = pl.program_id(2)
    lanes = HPG * LANES
    blocked = ((1.0 - sel_ref[...].astype(F32)) * NEG_INF).astype(BF16)
    qt_aug = jnp.concatenate([_group_queries(qt_ref[...]), jnp.concatenate([blocked] * HPG, axis=1)], axis=0)
    far = tbl_ref.shape[1] - 1

    def scores(kb):
        start = pl.multiple_of(kb * LANES, LANES)
        k_aug = jnp.concatenate([k_ref[pl.ds(start, LANES), :], exp_ref[kb]], axis=1)
        return _dot(k_aug, qt_aug)

    def body(kb, carry):
        s, m, acc = carry
        s_next = scores(jnp.minimum(kb + 1, i))
        start = pl.multiple_of(kb * LANES, LANES)
        x = s + _group_table(tbl_ref, jnp.minimum(i - kb, far))
        return (s_next,) + _online_update(x, vt_ref[:, pl.ds(start, LANES)], m, acc)

    init = (scores(0), jnp.full((1, lanes), M_INIT, F32), jnp.zeros((HEAD_DIM + BF16_ROWS, lanes), F32))
    _, _, acc = lax.fori_loop(0, i + 1, body, init)
    o_ref[...] = _finish_group(acc).astype(o_ref.dtype)


def _nsa_sel_attn(h, qvt, batch, seq, kblk, vtblk, sel, rel_bias):
    nq = seq // LANES
    tbl = _band_tables(rel_bias, 3, None)
    expand = jnp.asarray(_expand_np(seq), BF16)
    return pl.pallas_call(
        _sel_attn_kernel,
        grid=(batch, N_KV, nq),
        in_specs=[pl.BlockSpec((GROUP_W, LANES), lambda b, g, i: (g, b * nq + i)),
                  pl.BlockSpec((seq, HEAD_DIM), lambda b, g, i: (b, kblk + g)),
                  pl.BlockSpec((HEAD_DIM, seq), lambda b, g, i: (vtblk + g, b)),
                  pl.BlockSpec((None, None, None, LANES, LANES), lambda b, g, i: (b, g, i, 0, 0)),
                  pl.BlockSpec((HPG, 3, LANES, LANES), lambda b, g, i: (g, 0, 0, 0)),
                  pl.BlockSpec((nq, LANES, LANES), lambda b, g, i: (0, 0, 0))],
        out_specs=pl.BlockSpec((LANES, GROUP_W), lambda b, g, i: (b * nq + i, g)),
        out_shape=jax.ShapeDtypeStruct((batch * seq, WIDTH), BF16),
        compiler_params=_cparams("arbitrary", "arbitrary", "arbitrary"),
        name="nsa_sel_attn",
    )(qvt, h, qvt, sel, tbl, expand)


def _combine_kernel(oc_ref, os_ref, ow_ref, gl_ref, z_ref, o_ref):
    gates = 1.0 / (1.0 + jnp.exp(-gl_ref[...]))
    oc = oc_ref[...].astype(F32)
    osel = os_ref[...].astype(F32)
    ow = ow_ref[...].astype(F32)
    z = z_ref[...].astype(F32)
    outs = []
    for r in range(HPG):
        sl = slice(r * HEAD_DIM, (r + 1) * HEAD_DIM)
        o = (gates[:, 3 * r:3 * r + 1] * oc[:, sl] + gates[:, 3 * r + 1:3 * r + 2] * osel[:, sl]
             + gates[:, 3 * r + 2:3 * r + 3] * ow[:, sl])
        outs.append(o * _silu(z[:, sl]))
    o_ref[...] = jnp.concatenate(outs, axis=1).astype(o_ref.dtype)


def _nsa_combine(o_cmp, o_sel, o_win, gl, h, zblk, tm=512):
    m = o_cmp.shape[0]
    grp = pl.BlockSpec((tm, GROUP_W), lambda i, g: (i, g))
    return pl.pallas_call(
        _combine_kernel,
        grid=(m // tm, N_KV),
        in_specs=[grp, grp, grp,
                  pl.BlockSpec((tm, LANES), lambda i, g: (i, g)),
                  pl.BlockSpec((tm, GROUP_W), lambda i, g: (i, zblk + g))],
        out_specs=grp,
        out_shape=jax.ShapeDtypeStruct((m, WIDTH), BF16),
        compiler_params=_cparams("arbitrary", "arbitrary"),
        name="nsa_combine",
    )(o_cmp, o_sel, o_win, gl, h)


HEADS_PER_STEP = 4
STEP_W = HEADS_PER_STEP * HEAD_DIM


def _head(j):
    return slice(j * HEAD_DIM, (j + 1) * HEAD_DIM)


def _pipelined_heads(i, scores, consume, init):
    heads = range(HEADS_PER_STEP)
    ahead = [scores(j, jnp.maximum(i - 1, 0)) for j in heads]
    state = [consume(j, scores(j, i), i, *init, True) for j in heads]

    def body(step, carry):
        kb = i - step
        return tuple((scores(j, jnp.maximum(kb - 1, 0)),) + consume(j, carry[j][0], kb, *carry[j][1:], False)
                     for j in heads)

    carry = lax.fori_loop(1, i + 1, body, tuple((ahead[j],) + state[j] for j in heads))
    return [c[1:] for c in carry]


def _sb_kernel(qt_ref, k_ref, vt_ref, z_ref, o_ref, *, t):
    i = pl.program_id(2)
    key = lax.broadcasted_iota(jnp.int32, (t, t), 0)
    other = lax.broadcasted_iota(jnp.int32, (t, t), 1)
    strict = key < other
    later = jnp.concatenate([jnp.where(other > key, 1.0, 0.0).astype(BF16), jnp.ones((BF16_ROWS, t), BF16)], axis=0)

    def scores(j, kb):
        start = pl.multiple_of(kb * t, t)
        return _dot(k_ref[pl.ds(start, t), _head(j)], qt_ref[_head(j), :])

    def consume(j, y, kb, run, acc, diag):
        start = pl.multiple_of(kb * t, t)
        soft = jnp.log(1.0 + jnp.exp2(-jnp.abs(y))) * LOG2E
        log_keep = jnp.minimum(-y, 0.0) - soft
        log_beta = y + log_keep
        if diag:
            log_keep = jnp.where(strict, log_keep, 0.0)
        keep = _dot(later, log_keep.astype(BF16))
        a = jnp.exp2(log_beta + keep[:t] + run)
        if diag:
            a = jnp.where(strict, a, 0.0)
        acc = acc + _dot(vt_ref[_head(j), pl.ds(start, t)], a.astype(BF16))
        return run + keep[t:t + 1], acc

    final = _pipelined_heads(i, scores, consume, (jnp.zeros((1, t), F32), jnp.zeros((HEAD_DIM, t), F32)))
    o = jnp.concatenate([acc.T for _, acc in final], axis=1)
    o_ref[...] = (o * _silu(z_ref[...].astype(F32))).astype(o_ref.dtype)


def _full_attn_call(kernel, h, qvt, batch, seq, t, extra_specs, extra_args, name):
    nq = seq // t
    per_w = WIDTH // STEP_W
    in_specs = [pl.BlockSpec((STEP_W, t), lambda b, hp, i: (hp, b * nq + i)),
                pl.BlockSpec((seq, STEP_W), lambda b, hp, i: (b, hp)),
                pl.BlockSpec((STEP_W, seq), lambda b, hp, i: (per_w + hp, b)),
                pl.BlockSpec((t, STEP_W), lambda b, hp, i: (b * nq + i, per_w + hp))]
    return pl.pallas_call(
        functools.partial(kernel, t=t),
        grid=(batch, per_w, nq),
        in_specs=in_specs + extra_specs,
        out_specs=pl.BlockSpec((t, STEP_W), lambda b, hp, i: (b * nq + i, hp)),
        out_shape=jax.ShapeDtypeStruct((batch * seq, WIDTH), BF16),
        compiler_params=_cparams("arbitrary", "arbitrary", "arbitrary"),
        name=name,
    )(qvt, h, qvt, h, *extra_args)


KEY_PARTS = 3


def _fox_prep_kernel(fl_ref, bias_ref, ct_ref, ck_ref, *, seq):
    row = lax.broadcasted_iota(jnp.int32, (LANES, LANES), 0)
    col = lax.broadcasted_iota(jnp.int32, (LANES, LANES), 1)
    upto = jnp.where(row <= col, 1.0, 0.0).astype(BF16)
    downto = jnp.where(col <= row, 1.0, 0.0).astype(BF16)
    place = [jnp.where((col == row + j * N_HEADS) & (row < N_HEADS), 1.0, 0.0).astype(BF16)
             for j in range(KEY_PARTS)]
    carry_t = jnp.zeros((LANES, 1), F32)
    carry = jnp.zeros((1, LANES), F32)
    for blk in range(seq // LANES):
        x = fl_ref[blk * LANES:(blk + 1) * LANES, :] + bias_ref[...]
        log_f = jnp.minimum(x, 0.0) - jnp.log1p(jnp.exp(-jnp.abs(x)))
        cs_t = sum(_dot(part, upto) for part in _split3(log_f.T)) + carry_t
        ct_ref[:, blk * LANES:(blk + 1) * LANES] = cs_t * LOG2E
        carry_t = cs_t[:, LANES - 1:LANES]
        cs = sum(_dot(downto, part) for part in _split3(log_f)) + carry
        carry = cs[LANES - 1:LANES, :]
        parts = _split3(-(cs * LOG2E))
        ck_ref[blk * LANES:(blk + 1) * LANES, :] = sum(
            _dot(part, sel) for part, sel in zip(parts, place)).astype(ck_ref.dtype)


def _fox_prep(fl, fgate_bias, batch, seq):
    assert KEY_PARTS * N_HEADS <= LANES
    bias = jnp.zeros((1, LANES), F32).at[0, :N_HEADS].set(fgate_bias.astype(F32))
    return pl.pallas_call(
        functools.partial(_fox_prep_kernel, seq=seq),
        grid=(batch,),
        in_specs=[pl.BlockSpec((seq, LANES), lambda b: (b, 0)),
                  pl.BlockSpec((1, LANES), lambda b: (0, 0))],
        out_specs=[pl.BlockSpec((None, LANES, seq), lambda b: (b, 0, 0)),
                   pl.BlockSpec((seq, LANES), lambda b: (b, 0))],
        out_shape=[jax.ShapeDtypeStruct((batch, LANES, seq), F32),
                   jax.ShapeDtypeStruct((batch * seq, LANES), BF16)],
        compiler_params=_cparams("arbitrary"),
        name="fox_prep",
    )(fl, bias)


def _fox_kernel(qt_ref, k_ref, vt_ref, z_ref, crow_ref, ck_ref, o_ref, *, t):
    first_head = pl.program_id(1) * HEADS_PER_STEP
    i = pl.program_id(2)
    key = lax.broadcasted_iota(jnp.int32, (t, t), 0)
    qry = lax.broadcasted_iota(jnp.int32, (t, t), 1)
    causal = key <= qry
    row = lax.broadcasted_iota(jnp.int32, (LANES, t), 0)
    qt_aug = [jnp.concatenate(
        [qt_ref[_head(j), :],
         jnp.where((row % N_HEADS == first_head + j) & (row < KEY_PARTS * N_HEADS), 1.0, 0.0).astype(BF16)], axis=0)
        for j in range(HEADS_PER_STEP)]

    def scores(j, kb):
        start = pl.multiple_of(kb * t, t)
        k_aug = jnp.concatenate([k_ref[pl.ds(start, t), _head(j)], ck_ref[pl.ds(start, t), :]], axis=1)
        return _dot(k_aug, qt_aug[j])

    def consume(j, x, kb, m, acc, diag):
        start = pl.multiple_of(kb * t, t)
        c_q = crow_ref[j, i]
        if diag:
            x = jnp.where(causal, x, NEG_INF)
        m_new = jnp.maximum(m, jnp.max(x, axis=0, keepdims=True) + c_q)
        p = jnp.exp2(x + (c_q - m_new))
        vt = _with_ones_rows(vt_ref[_head(j), pl.ds(start, t)])
        return m_new, jnp.exp2(m - m_new) * acc + _dot(vt, p.astype(BF16))

    final = _pipelined_heads(i, scores, consume,
                             (jnp.full((1, t), M_INIT, F32), jnp.zeros((HEAD_DIM + BF16_ROWS, t), F32)))
    o = jnp.concatenate([(acc[:HEAD_DIM] / jnp.maximum(acc[HEAD_DIM:HEAD_DIM + 1], 1e-30)).T for _, acc in final],
                        axis=1)
    o_ref[...] = (o * _silu(z_ref[...].astype(F32))).astype(o_ref.dtype)


def _w_cols(w, *ranges):
    return jnp.concatenate([w[:, a:b] for a, b in ranges], axis=1).astype(BF16)


def _pad_cols(w, n):
    return jnp.pad(w, ((0, 0), (0, n - w.shape[1]))).astype(BF16)


def _w_rows(w, *ranges):
    cols = [w[:, :WIDTH] * QSCALE] + [w[:, a:b] for a, b in ranges]
    return jnp.concatenate(cols, axis=1).T.astype(BF16)


def _project(xb, w_main, name):
    return _matmul(xb, w_main, BF16, 512, _feature_tile(w_main.shape[1]), name)


def _layer_nsa(xb, batch, seq, rel_bias, w_in, cmp_pe_k, cmp_w1_k, cmp_w2_k, cmp_pe_v, cmp_w1_v, cmp_w2_v):
    kc0, ks0, vs0, kw0, vw0, gate0 = (WIDTH + j * KV_WIDTH for j in (0, 2, 3, 4, 5, 6))
    gate1 = gate0 + 3 * N_HEADS
    h = _project(xb, _w_cols(w_in, (kc0, vs0), (kw0, vw0), (gate1, w_in.shape[1])), "proj_nsa")
    qvt = _matmul_nt(_w_rows(w_in, (vs0, kw0), (vw0, gate0)), xb, "proj_nsa_t")
    per_group = 3 * HPG
    w_gate = w_in[:, gate0:gate1].reshape(-1, N_KV, per_group)
    w_gate = jnp.pad(w_gate, ((0, 0), (0, 0), (0, LANES - per_group))).reshape(-1, N_KV * LANES)
    gl = _matmul(xb, w_gate.astype(BF16), F32, 512, N_KV * LANES, "proj_nsa_gates")
    blk = lambda j: j * N_KV
    vrow = WIDTH // HEAD_DIM
    k_cmp, v_cmp = _nsa_compress(h, batch, seq, blk(0), blk(1), cmp_pe_k, cmp_w1_k, cmp_w2_k,
                                 cmp_pe_v, cmp_w1_v, cmp_w2_v)
    o_cmp, sel = _nsa_cmp_select(qvt, batch, seq, k_cmp, v_cmp, rel_bias)
    o_sel = _nsa_sel_attn(h, qvt, batch, seq, blk(2), vrow, sel, rel_bias)
    nd = NSA_WINDOW // LANES + 1
    o_win = _banded_gqa(h, qvt, batch, seq, blk(3), vrow + N_KV, _band_tables(rel_bias, nd, NSA_WINDOW, True),
                        name="nsa_window")
    zblk = 4 * KV_WIDTH // GROUP_W
    return _nsa_combine(o_cmp, o_sel, o_win, gl, h, zblk)


def _layer_swa(xb, batch, seq, rel_bias, w_in, sinks):
    v0, z0 = WIDTH + KV_WIDTH, WIDTH + 2 * KV_WIDTH
    h = _project(xb, _w_cols(w_in, (WIDTH, v0), (z0, w_in.shape[1])), "proj_swa")
    qvt = _matmul_nt(_w_rows(w_in, (v0, z0)), xb, "proj_swa_t")
    nd = SWA_WINDOW // LANES + 1
    return _banded_gqa(h, qvt, batch, seq, 0, WIDTH // HEAD_DIM, _band_tables(rel_bias, nd, SWA_WINDOW, True),
                       sinks=sinks, zblk=KV_WIDTH // GROUP_W, name="swa")


def _layer_sb(xb, batch, seq, w_in, t=256):
    h = _project(xb, _w_cols(w_in, (WIDTH, 2 * WIDTH), (3 * WIDTH, 4 * WIDTH)), "proj_sb")
    qvt = _matmul_nt(_w_rows(w_in, (2 * WIDTH, 3 * WIDTH)), xb, "proj_sb_t")
    return _full_attn_call(_sb_kernel, h, qvt, batch, seq, t, [], [], "stick_breaking")


def _layer_fox(xb, batch, seq, w_in, fgate_bias, t=256):
    f0 = 3 * WIDTH
    f1 = f0 + N_HEADS
    h = _project(xb, _w_cols(w_in, (WIDTH, 2 * WIDTH), (f1, w_in.shape[1])), "proj_fox")
    qvt = _matmul_nt(_w_rows(w_in, (2 * WIDTH, f0)), xb, "proj_fox_t")
    fl = _matmul(xb, _pad_cols(w_in[:, f0:f1], LANES), F32, 512, LANES, "proj_fox_gates")
    ct, ck = _fox_prep(fl, fgate_bias, batch, seq)
    nq = seq // t
    extra_specs = [pl.BlockSpec((None, HEADS_PER_STEP, nq, 1, t), lambda b, hp, i: (b, hp, 0, 0, 0)),
                   pl.BlockSpec((seq, LANES), lambda b, hp, i: (b, 0))]
    return _full_attn_call(_fox_kernel, h, qvt, batch, seq, t, extra_specs,
                           [ct.reshape(batch, LANES, nq, 1, t), ck], "forgetting")


def kernel(x, rel_bias, w_in_a, w_out_a, ln_g_a, ln_b_a, cmp_pe_k, cmp_w1_k, cmp_w2_k, cmp_pe_v, cmp_w1_v, cmp_w2_v, w_in_b, w_out_b, ln_g_b, ln_b_b, sinks_b, w_in_c, w_out_c, ln_g_c, ln_b_c, w_in_d, w_out_d, ln_g_d, ln_b_d, fgate_bias_d):
    batch, seq, d_model = x.shape
    xf = x.reshape(batch * seq, d_model)
    xb = xf.astype(BF16)
    og = _layer_nsa(xb, batch, seq, rel_bias, w_in_a, cmp_pe_k, cmp_w1_k, cmp_w2_k, cmp_pe_v, cmp_w1_v, cmp_w2_v)
    xf, xb = _out_proj_ln(og, w_out_a, xf, ln_g_a, ln_b_a)
    og = _layer_swa(xb, batch, seq, rel_bias, w_in_b, sinks_b)
    xf, xb = _out_proj_ln(og, w_out_b, xf, ln_g_b, ln_b_b)
    og = _layer_sb(xb, batch, seq, w_in_c)
    xf, xb = _out_proj_ln(og, w_out_c, xf, ln_g_c, ln_b_c)
    og = _layer_fox(xb, batch, seq, w_in_d, fgate_bias_d)
    xf, xb = _out_proj_ln(og, w_out_d, xf, ln_g_d, ln_b_d)
    return xf.reshape(batch, seq, d_model)
```

```python
import functools
import math

import numpy as np
import jax
import jax.numpy as jnp
from jax import lax
from jax.experimental import pallas as pl
from jax.experimental.pallas import tpu as pltpu

F32 = jnp.float32
BF16 = jnp.bfloat16

N_HEADS = 16
HEAD_DIM = 128
N_KV = 4
HPG = N_HEADS // N_KV
WIDTH = N_HEADS * HEAD_DIM
KV_WIDTH = N_KV * HEAD_DIM
GROUP_W = HPG * HEAD_DIM
REL_BUCKETS = 32
REL_MAX_DIST = 128
CMP_LEN = 32
CMP_STRIDE = 16
SEL_LEN = 64
SEL_TOPK = 8
NSA_WINDOW = 512
SWA_WINDOW = 128
DEPTH = 4
DN_ALPHA = (2 * DEPTH) ** 0.25
LN_EPS = 1e-5
NEG_INF = -1e30
FORCED_SCORE = 1e4
SCALE = HEAD_DIM ** -0.5
LOG2E = math.log2(math.e)
QSCALE = SCALE * LOG2E

LANES = 128
BF16_ROWS = 16
TILES_PER_STEP = 4
M_INIT = -1e29
VMEM_LIMIT = 56 * 1024 * 1024


def _cparams(*sem):
    return pltpu.CompilerParams(dimension_semantics=sem, vmem_limit_bytes=VMEM_LIMIT)


def _silu(z):
    return z / (1.0 + jnp.exp(-z))


def _dot_nt(a, b):
    return lax.dot_general(a, b, (((1,), (1,)), ((), ())), preferred_element_type=F32)


def _dot(a, b):
    return jnp.dot(a, b, preferred_element_type=F32)


def _head(j):
    return slice(j * HEAD_DIM, (j + 1) * HEAD_DIM)


def _group_queries(qt):
    return jnp.concatenate([qt[r * HEAD_DIM:(r + 1) * HEAD_DIM] for r in range(HPG)], axis=1)


def _split3(x):
    a1 = x.astype(BF16)
    r1 = x - a1.astype(F32)
    a2 = r1.astype(BF16)
    a3 = (r1 - a2.astype(F32)).astype(BF16)
    return a1, a2, a3


def _mm_kernel(x_ref, w_ref, o_ref):
    o_ref[...] = _dot(x_ref[...], w_ref[...]).astype(o_ref.dtype)


def _mm_nt_kernel(wt_ref, x_ref, o_ref):
    o_ref[...] = _dot_nt(wt_ref[...], x_ref[...]).astype(o_ref.dtype)


def _matmul_nt(wt, x, name, tm=512):
    n, k = wt.shape
    m = x.shape[0]
    tn = _feature_tile(n)
    assert m % tm == 0 and n % tn == 0
    return pl.pallas_call(
        _mm_nt_kernel,
        grid=(n // tn, m // tm),
        in_specs=[pl.BlockSpec((tn, k), lambda j, i: (j, 0)),
                  pl.BlockSpec((tm, k), lambda j, i: (i, 0))],
        out_specs=pl.BlockSpec((tn, tm), lambda j, i: (j, i)),
        out_shape=jax.ShapeDtypeStruct((n, m), BF16),
        compiler_params=_cparams("arbitrary", "arbitrary"),
        name=name,
    )(wt, x)


def _feature_tile(n):
    return next(t for t in (1024, 1280, 1536, 768, 512, 256, 128) if n % t == 0)


def _matmul(x, w, out_dtype, tm, tn, name):
    m, k = x.shape
    n = w.shape[1]
    assert m % tm == 0 and n % tn == 0
    return pl.pallas_call(
        _mm_kernel,
        grid=(n // tn, m // tm),
        in_specs=[pl.BlockSpec((tm, k), lambda j, i: (i, 0)),
                  pl.BlockSpec((k, tn), lambda j, i: (0, j))],
        out_specs=pl.BlockSpec((tm, tn), lambda j, i: (i, j)),
        out_shape=jax.ShapeDtypeStruct((m, n), out_dtype),
        compiler_params=_cparams("arbitrary", "arbitrary"),
        name=name,
    )(x, w)


def _outln_kernel(og_ref, w_ref, x_ref, g_ref, b_ref, xo_ref, xb_ref):
    y = _dot(og_ref[...], w_ref[...])
    t = DN_ALPHA * x_ref[...] + y
    mu = jnp.mean(t, axis=-1, keepdims=True)
    d = t - mu
    var = jnp.mean(d * d, axis=-1, keepdims=True)
    out = d * lax.rsqrt(var + LN_EPS) * g_ref[...] + b_ref[...]
    xo_ref[...] = out
    xb_ref[...] = out.astype(BF16)


def _out_proj_ln(og, w_out, x, ln_g, ln_b, tm=512):
    m, k = og.shape
    n = w_out.shape[1]
    return pl.pallas_call(
        _outln_kernel,
        grid=(m // tm,),
        in_specs=[pl.BlockSpec((tm, k), lambda i: (i, 0)),
                  pl.BlockSpec((k, n), lambda i: (0, 0)),
                  pl.BlockSpec((tm, n), lambda i: (i, 0)),
                  pl.BlockSpec((1, n), lambda i: (0, 0)),
                  pl.BlockSpec((1, n), lambda i: (0, 0))],
        out_specs=[pl.BlockSpec((tm, n), lambda i: (i, 0)),
                   pl.BlockSpec((tm, n), lambda i: (i, 0))],
        out_shape=[jax.ShapeDtypeStruct((m, n), F32), jax.ShapeDtypeStruct((m, n), BF16)],
        compiler_params=_cparams("arbitrary"),
        name="out_proj_ln",
    )(og, w_out.astype(BF16), x, ln_g.reshape(1, n), ln_b.reshape(1, n))


def _bucket_np(dist):
    max_exact = REL_BUCKETS // 2
    ratio = np.maximum(dist, max_exact).astype(np.float32) / max_exact
    large = max_exact + (np.log(ratio) / math.log(REL_MAX_DIST / max_exact)
                         * (REL_BUCKETS - max_exact)).astype(np.int32)
    return np.where(dist < max_exact, dist, np.minimum(large, REL_BUCKETS - 1))


def _rel_table(rel_bias, dist, ok):
    bucket = _bucket_np(np.maximum(dist, 0)).reshape(-1)
    onehot_t = np.zeros((REL_BUCKETS, bucket.size), np.float32)
    onehot_t[bucket, np.arange(bucket.size)] = 1.0
    t = jnp.dot(rel_bias.T.astype(F32) * LOG2E, jnp.asarray(onehot_t), precision=lax.Precision.HIGHEST)
    blocked = np.where(ok.reshape(-1), 0.0, NEG_INF).astype(np.float32)
    return (t + blocked[None, :]).reshape((N_HEADS,) + dist.shape)


def _band_tables(rel_bias, nd, window, with_blocked=False):
    key = np.arange(LANES)[:, None]
    qry = np.arange(LANES)[None, :]
    dist = np.stack([LANES * d + qry - key for d in range(nd)])
    ok = dist >= 0
    if window is not None:
        ok &= dist < window
    if with_blocked:
        dist = np.concatenate([dist, np.zeros((1, LANES, LANES), dist.dtype)])
        ok = np.concatenate([ok, np.zeros((1, LANES, LANES), bool)])
    return _rel_table(rel_bias, dist, ok)


def _group_table(tbl_ref, idx):
    return jnp.concatenate([tbl_ref[r, idx] for r in range(HPG)], axis=1)


def _with_ones_rows(vt):
    return jnp.concatenate([vt, jnp.ones((BF16_ROWS, vt.shape[1]), BF16)], axis=0)


def _finish_group(acc):
    o_t = acc[:HEAD_DIM] / jnp.maximum(acc[HEAD_DIM:HEAD_DIM + 1], 1e-30)
    return jnp.concatenate([o_t[:, r * LANES:(r + 1) * LANES].T for r in range(HPG)], axis=1)


def _online_update(x, vt, m, acc):
    m_new = jnp.maximum(m, jnp.max(x, axis=0, keepdims=True))
    p = jnp.exp2(x - m_new)
    acc = jnp.exp2(m - m_new) * acc + _dot(_with_ones_rows(vt), p.astype(BF16))
    return m_new, acc


def _banded_kernel(*refs, nd, has_sink, mixed):
    refs = list(refs)
    qt_ref, k_ref, vt_ref, tbl_ref, z_ref = refs[:5]
    rest = refs[5:]
    sink_ref = rest.pop(0) if has_sink else None
    oc_ref, os_ref, gl_ref = (rest.pop(0), rest.pop(0), rest.pop(0)) if mixed else (None, None, None)
    (o_ref,) = rest
    for u in range(TILES_PER_STEP):
        i = pl.program_id(2) * TILES_PER_STEP + u
        tile = slice(u * LANES, (u + 1) * LANES)
        ks, vts, tbls = [], [], []
        for d in range(nd):
            kb = i - d
            start = pl.multiple_of(jnp.maximum(kb, 0) * LANES, LANES)
            ks.append(k_ref[pl.ds(start, LANES), :])
            vts.append(vt_ref[:, pl.ds(start, LANES)])
            tbls.append(_group_table(tbl_ref, jnp.where(kb >= 0, d, nd)))
        x = _dot(jnp.concatenate(ks, axis=0), _group_queries(qt_ref[:, tile])) + jnp.concatenate(tbls, axis=0)
        m = jnp.max(x, axis=0, keepdims=True)
        if has_sink:
            m = jnp.maximum(m, sink_ref[...])
        p = jnp.exp2(x - m)
        acc = _dot(_with_ones_rows(jnp.concatenate(vts, axis=1)), p.astype(BF16))
        if has_sink:
            rows = lax.broadcasted_iota(jnp.int32, acc.shape, 0)
            acc = acc + jnp.where(rows >= HEAD_DIM, jnp.exp2(sink_ref[...] - m), 0.0)
        o = _finish_group(acc)
        if mixed:
            gates = 1.0 / (1.0 + jnp.exp(-gl_ref[tile, :]))
            oc = oc_ref[tile, :].astype(F32)
            osel = os_ref[tile, :].astype(F32)
            o = jnp.concatenate(
                [gates[:, 3 * r:3 * r + 1] * oc[:, _head(r)] + gates[:, 3 * r + 1:3 * r + 2] * osel[:, _head(r)]
                 + gates[:, 3 * r + 2:3 * r + 3] * o[:, _head(r)] for r in range(HPG)], axis=1)
        o_ref[tile, :] = (o * _silu(z_ref[tile, :].astype(F32))).astype(o_ref.dtype)


def _banded_gqa(h, qvt, batch, seq, kblk, vtblk, zblk, tbl, sinks=None, mix=None, name="banded"):
    step = TILES_PER_STEP * LANES
    nq = seq // step
    nd = tbl.shape[1] - 1
    rows_of_group = lambda blk: pl.BlockSpec((step, GROUP_W), lambda b, g, i: (b * nq + i, blk + g))
    in_specs = [pl.BlockSpec((GROUP_W, step), lambda b, g, i: (g, b * nq + i)),
                pl.BlockSpec((seq, HEAD_DIM), lambda b, g, i: (b, kblk + g)),
                pl.BlockSpec((HEAD_DIM, seq), lambda b, g, i: (vtblk + g, b)),
                pl.BlockSpec((HPG, nd + 1, LANES, LANES), lambda b, g, i: (g, 0, 0, 0)),
                rows_of_group(zblk)]
    args = [qvt, h, qvt, tbl, h]
    if sinks is not None:
        sink_row = jnp.repeat(sinks.astype(F32).reshape(N_KV, HPG) * LOG2E, LANES, axis=1)
        in_specs.append(pl.BlockSpec((None, 1, HPG * LANES), lambda b, g, i: (g, 0, 0)))
        args.append(sink_row.reshape(N_KV, 1, HPG * LANES))
    if mix is not None:
        in_specs += [rows_of_group(0), rows_of_group(0),
                     pl.BlockSpec((step, LANES), lambda b, g, i: (b * nq + i, g))]
        args += list(mix)
    return pl.pallas_call(
        functools.partial(_banded_kernel, nd=nd, has_sink=sinks is not None, mixed=mix is not None),
        grid=(batch, N_KV, nq),
        in_specs=in_specs,
        out_specs=pl.BlockSpec((step, GROUP_W), lambda b, g, i: (b * nq + i, g)),
        out_shape=jax.ShapeDtypeStruct((batch * seq, WIDTH), BF16),
        compiler_params=_cparams("arbitrary", "arbitrary", "arbitrary"),
        name=name,
    )(*args)


def _compress_kernel(kc_ref, vc_ref, pek_ref, w1k_ref, w2k_ref, pev_ref, w1v_ref, w2v_ref,
                     ko_ref, vo_ref, xs_ref, *, nc):
    half = CMP_LEN // 2
    for x_ref, pe_ref, w1_ref, w2_ref, o_ref in ((kc_ref, pek_ref, w1k_ref, w2k_ref, ko_ref),
                                                 (vc_ref, pev_ref, w1v_ref, w2v_ref, vo_ref)):
        xs_ref[...] = x_ref[...].astype(F32)
        first = jnp.zeros((nc, HEAD_DIM), F32)
        second = jnp.zeros((nc, HEAD_DIM), F32)
        for l in range(half):
            xl = xs_ref[pl.ds(l, nc, stride=CMP_STRIDE), :]
            first += _dot((xl + pe_ref[l:l + 1, :]).astype(BF16), w1_ref[l])
            second += _dot((xl + pe_ref[half + l:half + l + 1, :]).astype(BF16), w1_ref[half + l])
        hid = _silu(first + pltpu.roll(second, nc - 1, 0))
        o_ref[...] = _dot(hid.astype(BF16), w2_ref[...]).astype(o_ref.dtype)


def _nsa_compress(h, batch, seq, kcblk, vcblk, pe_k, w1_k, w2_k, pe_v, w1_v, w2_v):
    assert CMP_LEN == 2 * CMP_STRIDE
    nc = seq // CMP_STRIDE
    kv_spec = lambda blk: pl.BlockSpec((seq, HEAD_DIM), lambda b, g: (b, blk + g))
    full = lambda shape: pl.BlockSpec(shape, lambda b, g: (0,) * len(shape))
    out_spec = pl.BlockSpec((None, None, nc, HEAD_DIM), lambda b, g: (b, g, 0, 0))
    out_shape = jax.ShapeDtypeStruct((batch, N_KV, nc, HEAD_DIM), BF16)
    return pl.pallas_call(
        functools.partial(_compress_kernel, nc=nc),
        grid=(batch, N_KV),
        in_specs=[kv_spec(kcblk), kv_spec(vcblk),
                  full((CMP_LEN, HEAD_DIM)), full((CMP_LEN, HEAD_DIM, HEAD_DIM)), full((HEAD_DIM, HEAD_DIM)),
                  full((CMP_LEN, HEAD_DIM)), full((CMP_LEN, HEAD_DIM, HEAD_DIM)), full((HEAD_DIM, HEAD_DIM))],
        out_specs=[out_spec, out_spec],
        out_shape=[out_shape, out_shape],
        scratch_shapes=[pltpu.VMEM((seq, HEAD_DIM), F32)],
        compiler_params=_cparams("arbitrary", "arbitrary"),
        name="nsa_compress",
    )(h, h, pe_k, w1_k.astype(BF16), w2_k.astype(BF16), pe_v, w1_v.astype(BF16), w2_v.astype(BF16))


def _cmp_table(rel_bias, nc):
    u = np.arange(2 * nc)[:, None] - nc
    dist = np.arange(LANES)[None, :] - (u * CMP_STRIDE + CMP_LEN - 1)
    return _rel_table(rel_bias, dist, dist >= 0)


def _inter_t(seq, nc):
    nb = seq // SEL_LEN
    cstart = np.arange(nc) * CMP_STRIDE
    sstart = np.arange(nb) * SEL_LEN
    inter = np.clip(np.minimum(cstart[None, :] + CMP_LEN, sstart[:, None] + SEL_LEN)
                    - np.maximum(cstart[None, :], sstart[:, None]), 0, None) / CMP_LEN
    return inter.astype(np.float32)


def _cmp_select_kernel(qt_ref, kc_ref, vc_ref, tbl_ref, inter_ref, o_ref, sel_ref, *, nb, nc):
    first = pl.program_id(1) * TILES_PER_STEP
    width = TILES_PER_STEP * LANES
    kc = kc_ref[...]
    vc = vc_ref[...]
    per_tile = LANES // CMP_STRIDE
    row0 = [pl.multiple_of(nc - (first + u) * per_tile, per_tile) for u in range(TILES_PER_STEP)]
    outs = []
    p_sum_t = None
    for r in range(HPG):
        tbl = jnp.concatenate([tbl_ref[r, pl.ds(row, nc), :] for row in row0], axis=1)
        st = _dot(kc, qt_ref[r * HEAD_DIM:(r + 1) * HEAD_DIM, :]) + tbl
        pt = jnp.where(tbl > 0.5 * NEG_INF, jnp.exp2(st - jnp.max(st, axis=0, keepdims=True)), 0.0)
        pt = pt / jnp.maximum(jnp.sum(pt, axis=0, keepdims=True), 1e-30)
        p_sum_t = pt if p_sum_t is None else p_sum_t + pt
        outs.append(_dot(pt.T.astype(BF16), vc))
    o_ref[...] = jnp.concatenate(outs, axis=1).astype(o_ref.dtype)

    inter = inter_ref[...]
    imp = sum(_dot(inter, part) for part in _split3(p_sum_t))
    blk = lax.broadcasted_iota(jnp.int32, (nb, width), 0)
    pos = first * LANES + lax.broadcasted_iota(jnp.int32, (nb, width), 1)
    cur = lax.shift_right_logical(pos, int(math.log2(SEL_LEN)))
    allowed = blk * SEL_LEN <= pos
    forced = (blk == 0) | (blk == cur) | (blk == cur - 1)
    imp = jnp.where(allowed, jnp.where(forced, FORCED_SCORE, imp), NEG_INF)
    rank = jnp.zeros((nb, width), F32)
    for c in range(nb):
        row = imp[c:c + 1, :]
        earlier = jnp.where(blk > c, 1.0, 0.0)
        rank += jnp.where(row > imp, 1.0, jnp.where(row == imp, earlier, 0.0))
    sel_t = jnp.where((rank < SEL_TOPK) & (imp > 0.5 * NEG_INF), 1.0, 0.0)
    sel_t = jnp.concatenate([sel_t, jnp.zeros((LANES - nb, width), F32)], axis=0).astype(sel_ref.dtype)
    for u in range(TILES_PER_STEP):
        sel_ref[u] = sel_t[:, u * LANES:(u + 1) * LANES]


def _nsa_cmp_select(qvt, batch, seq, k_cmp, v_cmp, rel_bias):
    step = TILES_PER_STEP * LANES
    nq = seq // step
    nc = seq // CMP_STRIDE
    nb = seq // SEL_LEN
    assert nc == LANES and nb <= LANES
    tbl = _cmp_table(rel_bias, nc)
    inter = jnp.asarray(_inter_t(seq, nc), BF16)
    cmp_spec = pl.BlockSpec((None, None, nc, HEAD_DIM), lambda g, i, b: (b, g, 0, 0))
    return pl.pallas_call(
        functools.partial(_cmp_select_kernel, nb=nb, nc=nc),
        grid=(N_KV, nq, batch),
        in_specs=[pl.BlockSpec((GROUP_W, step), lambda g, i, b: (g, b * nq + i)),
                  cmp_spec, cmp_spec,
                  pl.BlockSpec((HPG, 2 * nc, LANES), lambda g, i, b: (g, 0, 0)),
                  pl.BlockSpec((nb, nc), lambda g, i, b: (0, 0))],
        out_specs=[pl.BlockSpec((step, GROUP_W), lambda g, i, b: (b * nq + i, g)),
                   pl.BlockSpec((None, None, TILES_PER_STEP, LANES, LANES), lambda g, i, b: (b, g, i, 0, 0))],
        out_shape=[jax.ShapeDtypeStruct((batch * seq, WIDTH), BF16),
                   jax.ShapeDtypeStruct((batch, N_KV, seq // LANES, LANES, LANES), BF16)],
        compiler_params=_cparams("arbitrary", "arbitrary", "arbitrary"),
        name="nsa_cmp_select",
    )(qvt, k_cmp, v_cmp, tbl, inter)


SEL_TILES = 2
SEL_KEYS = SEL_TILES * LANES


def _expand_np(seq):
    j = np.arange(LANES)[None, None, :]
    key = np.arange(seq // SEL_KEYS)[:, None, None] * SEL_KEYS + np.arange(SEL_KEYS)[None, :, None]
    return (j == key // SEL_LEN).astype(np.float32)


def _sel_attn_kernel(qt_ref, k_ref, vt_ref, sel_ref, tbl_ref, exp_ref, o_ref):
    i = pl.program_id(2)
    lanes = HPG * LANES
    blocked = ((1.0 - sel_ref[...].astype(F32)) * NEG_INF).astype(BF16)
    qt_aug = jnp.concatenate([_group_queries(qt_ref[...]), jnp.concatenate([blocked] * HPG, axis=1)], axis=0)
    beyond = tbl_ref.shape[1] - 1
    far = beyond - 1
    last = lax.div(i, SEL_TILES)

    def scores(kt):
        start = pl.multiple_of(kt * SEL_KEYS, SEL_KEYS)
        k_aug = jnp.concatenate([k_ref[pl.ds(start, SEL_KEYS), :], exp_ref[kt]], axis=1)
        return _dot(k_aug, qt_aug)

    def body(kt, carry):
        s, m, acc = carry
        s_next = scores(jnp.minimum(kt + 1, last))
        start = pl.multiple_of(kt * SEL_KEYS, SEL_KEYS)
        behind = [i - (kt * SEL_TILES + c) for c in range(SEL_TILES)]
        x = s + jnp.concatenate(
            [_group_table(tbl_ref, jnp.where(d >= 0, jnp.minimum(d, far), beyond)) for d in behind], axis=0)
        return (s_next,) + _online_update(x, vt_ref[:, pl.ds(start, SEL_KEYS)], m, acc)

    init = (scores(0), jnp.full((1, lanes), M_INIT, F32), jnp.zeros((HEAD_DIM + BF16_ROWS, lanes), F32))
    _, _, acc = lax.fori_loop(0, last + 1, body, init)
    o_ref[...] = _finish_group(acc).astype(o_ref.dtype)


def _nsa_sel_attn(h, qvt, batch, seq, kblk, vtblk, sel, rel_bias):
    nq = seq // LANES
    tbl = _band_tables(rel_bias, 3, None, True)
    expand = jnp.asarray(_expand_np(seq), BF16)
    return pl.pallas_call(
        _sel_attn_kernel,
        grid=(batch, N_KV, nq),
        in_specs=[pl.BlockSpec((GROUP_W, LANES), lambda b, g, i: (g, b * nq + i)),
                  pl.BlockSpec((seq, HEAD_DIM), lambda b, g, i: (b, kblk + g)),
                  pl.BlockSpec((HEAD_DIM, seq), lambda b, g, i: (vtblk + g, b)),
                  pl.BlockSpec((None, None, None, LANES, LANES), lambda b, g, i: (b, g, i, 0, 0)),
                  pl.BlockSpec((HPG, 4, LANES, LANES), lambda b, g, i: (g, 0, 0, 0)),
                  pl.BlockSpec((seq // SEL_KEYS, SEL_KEYS, LANES), lambda b, g, i: (0, 0, 0))],
        out_specs=pl.BlockSpec((LANES, GROUP_W), lambda b, g, i: (b * nq + i, g)),
        out_shape=jax.ShapeDtypeStruct((batch * seq, WIDTH), BF16),
        compiler_params=_cparams("arbitrary", "arbitrary", "arbitrary"),
        name="nsa_sel_attn",
    )(qvt, h, qvt, sel, tbl, expand)


HEADS_PER_STEP = 4
STEP_W = HEADS_PER_STEP * HEAD_DIM


def _pipelined_heads(i, scores, consume, init):
    heads = range(HEADS_PER_STEP)
    ahead = [scores(j, jnp.maximum(i - 1, 0)) for j in heads]
    state = [consume(j, scores(j, i), i, *init, True) for j in heads]

    def body(step, carry):
        kb = i - step
        return tuple((scores(j, jnp.maximum(kb - 1, 0)),) + consume(j, carry[j][0], kb, *carry[j][1:], False)
                     for j in heads)

    carry = lax.fori_loop(1, i + 1, body, tuple((ahead[j],) + state[j] for j in heads))
    return [c[1:] for c in carry]


def _sb_kernel(qt_ref, k_ref, vt_ref, z_ref, o_ref, *, t):
    i = pl.program_id(2)
    key = lax.broadcasted_iota(jnp.int32, (t, t), 0)
    other = lax.broadcasted_iota(jnp.int32, (t, t), 1)
    strict = key < other
    later = jnp.concatenate([jnp.where(other >= key, 1.0, 0.0).astype(BF16), jnp.ones((BF16_ROWS, t), BF16)], axis=0)

    def scores(j, kb):
        start = pl.multiple_of(kb * t, t)
        return _dot(k_ref[pl.ds(start, t), _head(j)], qt_ref[_head(j), :])

    def consume(j, y, kb, run, acc, diag):
        start = pl.multiple_of(kb * t, t)
        neg = -y
        soft = jnp.log(1.0 + jnp.exp2(jnp.minimum(y, neg))) * LOG2E
        log_keep = jnp.minimum(neg, 0.0) - soft
        if diag:
            log_keep = jnp.where(strict, log_keep, 0.0)
        keep = _dot(later, log_keep.astype(BF16))
        a = jnp.exp2(y + keep[:t] + run)
        if diag:
            a = jnp.where(strict, a, 0.0)
        acc = acc + _dot(vt_ref[_head(j), pl.ds(start, t)], a.astype(BF16))
        return run + keep[t:t + 1], acc

    final = _pipelined_heads(i, scores, consume, (jnp.zeros((1, t), F32), jnp.zeros((HEAD_DIM, t), F32)))
    o = jnp.concatenate([acc.T for _, acc in final], axis=1)
    o_ref[...] = (o * _silu(z_ref[...].astype(F32))).astype(o_ref.dtype)


def _full_attn_call(kernel, h, qvt, batch, seq, t, extra_specs, extra_args, name):
    nq = seq // t
    per_w = WIDTH // STEP_W
    in_specs = [pl.BlockSpec((STEP_W, t), lambda b, hp, i: (hp, b * nq + i)),
                pl.BlockSpec((seq, STEP_W), lambda b, hp, i: (b, hp)),
                pl.BlockSpec((STEP_W, seq), lambda b, hp, i: (per_w + hp, b)),
                pl.BlockSpec((t, STEP_W), lambda b, hp, i: (b * nq + i, per_w + hp))]
    return pl.pallas_call(
        functools.partial(kernel, t=t),
        grid=(batch, per_w, nq),
        in_specs=in_specs + extra_specs,
        out_specs=pl.BlockSpec((t, STEP_W), lambda b, hp, i: (b * nq + i, hp)),
        out_shape=jax.ShapeDtypeStruct((batch * seq, WIDTH), BF16),
        compiler_params=_cparams("arbitrary", "arbitrary", "arbitrary"),
        name=name,
    )(qvt, h, qvt, h, *extra_args)


KEY_PARTS = 3


def _fox_prep_kernel(fl_ref, bias_ref, ct_ref, ck_ref, *, seq):
    row = lax.broadcasted_iota(jnp.int32, (LANES, LANES), 0)
    col = lax.broadcasted_iota(jnp.int32, (LANES, LANES), 1)
    upto = jnp.where(row <= col, 1.0, 0.0).astype(BF16)
    downto = jnp.where(col <= row, 1.0, 0.0).astype(BF16)
    place = [jnp.where((col == row + j * N_HEADS) & (row < N_HEADS), 1.0, 0.0).astype(BF16)
             for j in range(KEY_PARTS)]
    carry_t = jnp.zeros((LANES, 1), F32)
    carry = jnp.zeros((1, LANES), F32)
    for blk in range(seq // LANES):
        x = fl_ref[blk * LANES:(blk + 1) * LANES, :] + bias_ref[...]
        log_f = jnp.minimum(x, 0.0) - jnp.log1p(jnp.exp(-jnp.abs(x)))
        cs_t = sum(_dot(part, upto) for part in _split3(log_f.T)) + carry_t
        ct_ref[:, blk * LANES:(blk + 1) * LANES] = cs_t * LOG2E
        carry_t = cs_t[:, LANES - 1:LANES]
        cs = sum(_dot(downto, part) for part in _split3(log_f)) + carry
        carry = cs[LANES - 1:LANES, :]
        parts = _split3(-(cs * LOG2E))
        ck_ref[blk * LANES:(blk + 1) * LANES, :] = sum(
            _dot(part, sel) for part, sel in zip(parts, place)).astype(ck_ref.dtype)


def _fox_prep(fl, fgate_bias, batch, seq):
    assert KEY_PARTS * N_HEADS <= LANES
    bias = jnp.zeros((1, LANES), F32).at[0, :N_HEADS].set(fgate_bias.astype(F32))
    return pl.pallas_call(
        functools.partial(_fox_prep_kernel, seq=seq),
        grid=(batch,),
        in_specs=[pl.BlockSpec((seq, LANES), lambda b: (b, 0)),
                  pl.BlockSpec((1, LANES), lambda b: (0, 0))],
        out_specs=[pl.BlockSpec((None, LANES, seq), lambda b: (b, 0, 0)),
                   pl.BlockSpec((seq, LANES), lambda b: (b, 0))],
        out_shape=[jax.ShapeDtypeStruct((batch, LANES, seq), F32),
                   jax.ShapeDtypeStruct((batch * seq, LANES), BF16)],
        compiler_params=_cparams("arbitrary"),
        name="fox_prep",
    )(fl, bias)


def _fox_kernel(qt_ref, k_ref, vt_ref, z_ref, crow_ref, ck_ref, o_ref, *, t):
    first_head = pl.program_id(1) * HEADS_PER_STEP
    i = pl.program_id(2)
    key = lax.broadcasted_iota(jnp.int32, (t, t), 0)
    qry = lax.broadcasted_iota(jnp.int32, (t, t), 1)
    causal = key <= qry
    row = lax.broadcasted_iota(jnp.int32, (LANES, t), 0)
    qt_aug = [jnp.concatenate(
        [qt_ref[_head(j), :],
         jnp.where((row % N_HEADS == first_head + j) & (row < KEY_PARTS * N_HEADS), 1.0, 0.0).astype(BF16)], axis=0)
        for j in range(HEADS_PER_STEP)]

    def scores(j, kb):
        start = pl.multiple_of(kb * t, t)
        k_aug = jnp.concatenate([k_ref[pl.ds(start, t), _head(j)], ck_ref[pl.ds(start, t), :]], axis=1)
        return _dot(k_aug, qt_aug[j])

    def consume(j, x, kb, m, acc, diag):
        start = pl.multiple_of(kb * t, t)
        c_q = crow_ref[j, i]
        if diag:
            x = jnp.where(causal, x, NEG_INF)
        m_new = jnp.maximum(m, jnp.max(x, axis=0, keepdims=True) + c_q)
        p = jnp.exp2(x + (c_q - m_new))
        vt = _with_ones_rows(vt_ref[_head(j), pl.ds(start, t)])
        return m_new, jnp.exp2(m - m_new) * acc + _dot(vt, p.astype(BF16))

    final = _pipelined_heads(i, scores, consume,
                             (jnp.full((1, t), M_INIT, F32), jnp.zeros((HEAD_DIM + BF16_ROWS, t), F32)))
    o = jnp.concatenate([(acc[:HEAD_DIM] / jnp.maximum(acc[HEAD_DIM:HEAD_DIM + 1], 1e-30)).T for _, acc in final],
                        axis=1)
    o_ref[...] = (o * _silu(z_ref[...].astype(F32))).astype(o_ref.dtype)


def _w_cols(w, *ranges):
    return jnp.concatenate([w[:, a:b] for a, b in ranges], axis=1).astype(BF16)


def _pad_cols(w, n):
    return jnp.pad(w, ((0, 0), (0, n - w.shape[1]))).astype(BF16)


def _w_rows(w, *ranges):
    cols = [w[:, :WIDTH] * QSCALE] + [w[:, a:b] for a, b in ranges]
    return jnp.concatenate(cols, axis=1).T.astype(BF16)


def _project(xb, w_main, name):
    return _matmul(xb, w_main, BF16, 512, _feature_tile(w_main.shape[1]), name)


def _layer_nsa(xb, batch, seq, rel_bias, w_in, cmp_pe_k, cmp_w1_k, cmp_w2_k, cmp_pe_v, cmp_w1_v, cmp_w2_v):
    kc0, ks0, vs0, kw0, vw0, gate0 = (WIDTH + j * KV_WIDTH for j in (0, 2, 3, 4, 5, 6))
    gate1 = gate0 + 3 * N_HEADS
    h = _project(xb, _w_cols(w_in, (kc0, vs0), (kw0, vw0), (gate1, w_in.shape[1])), "proj_nsa")
    qvt = _matmul_nt(_w_rows(w_in, (vs0, kw0), (vw0, gate0)), xb, "proj_nsa_t")
    per_group = 3 * HPG
    w_gate = w_in[:, gate0:gate1].reshape(-1, N_KV, per_group)
    w_gate = jnp.pad(w_gate, ((0, 0), (0, 0), (0, LANES - per_group))).reshape(-1, N_KV * LANES)
    gl = _matmul(xb, w_gate.astype(BF16), F32, 512, N_KV * LANES, "proj_nsa_gates")
    blk = lambda j: j * N_KV
    vrow = WIDTH // HEAD_DIM
    k_cmp, v_cmp = _nsa_compress(h, batch, seq, blk(0), blk(1), cmp_pe_k, cmp_w1_k, cmp_w2_k,
                                 cmp_pe_v, cmp_w1_v, cmp_w2_v)
    o_cmp, sel = _nsa_cmp_select(qvt, batch, seq, k_cmp, v_cmp, rel_bias)
    o_sel = _nsa_sel_attn(h, qvt, batch, seq, blk(2), vrow, sel, rel_bias)
    nd = NSA_WINDOW // LANES + 1
    return _banded_gqa(h, qvt, batch, seq, blk(3), vrow + N_KV, 4 * KV_WIDTH // GROUP_W,
                       _band_tables(rel_bias, nd, NSA_WINDOW, True), mix=(o_cmp, o_sel, gl), name="nsa_window_mix")


def _layer_swa(xb, batch, seq, rel_bias, w_in, sinks):
    v0, z0 = WIDTH + KV_WIDTH, WIDTH + 2 * KV_WIDTH
    h = _project(xb, _w_cols(w_in, (WIDTH, v0), (z0, w_in.shape[1])), "proj_swa")
    qvt = _matmul_nt(_w_rows(w_in, (v0, z0)), xb, "proj_swa_t")
    nd = SWA_WINDOW // LANES + 1
    return _banded_gqa(h, qvt, batch, seq, 0, WIDTH // HEAD_DIM, KV_WIDTH // GROUP_W,
                       _band_tables(rel_bias, nd, SWA_WINDOW, True), sinks=sinks, name="swa")


def _layer_sb(xb, batch, seq, w_in, t=256):
    h = _project(xb, _w_cols(w_in, (WIDTH, 2 * WIDTH), (3 * WIDTH, 4 * WIDTH)), "proj_sb")
    qvt = _matmul_nt(_w_rows(w_in, (2 * WIDTH, 3 * WIDTH)), xb, "proj_sb_t")
    return _full_attn_call(_sb_kernel, h, qvt, batch, seq, t, [], [], "stick_breaking")


def _layer_fox(xb, batch, seq, w_in, fgate_bias, t=256):
    f0 = 3 * WIDTH
    f1 = f0 + N_HEADS
    h = _project(xb, _w_cols(w_in, (WIDTH, 2 * WIDTH), (f1, w_in.shape[1])), "proj_fox")
    qvt = _matmul_nt(_w_rows(w_in, (2 * WIDTH, f0)), xb, "proj_fox_t")
    fl = _matmul(xb, _pad_cols(w_in[:, f0:f1], LANES), F32, 512, LANES, "proj_fox_gates")
    ct, ck = _fox_prep(fl, fgate_bias, batch, seq)
    nq = seq // t
    extra_specs = [pl.BlockSpec((None, HEADS_PER_STEP, nq, 1, t), lambda b, hp, i: (b, hp, 0, 0, 0)),
                   pl.BlockSpec((seq, LANES), lambda b, hp, i: (b, 0))]
    return _full_attn_call(_fox_kernel, h, qvt, batch, seq, t, extra_specs,
                           [ct.reshape(batch, LANES, nq, 1, t), ck], "forgetting")


def kernel(x, rel_bias, w_in_a, w_out_a, ln_g_a, ln_b_a, cmp_pe_k, cmp_w1_k, cmp_w2_k, cmp_pe_v, cmp_w1_v, cmp_w2_v, w_in_b, w_out_b, ln_g_b, ln_b_b, sinks_b, w_in_c, w_out_c, ln_g_c, ln_b_c, w_in_d, w_out_d, ln_g_d, ln_b_d, fgate_bias_d):
    batch, seq, d_model = x.shape
    xf = x.reshape(batch * seq, d_model)
    xb = xf.astype(BF16)
    og = _layer_nsa(xb, batch, seq, rel_bias, w_in_a, cmp_pe_k, cmp_w1_k, cmp_w2_k, cmp_pe_v, cmp_w1_v, cmp_w2_v)
    xf, xb = _out_proj_ln(og, w_out_a, xf, ln_g_a, ln_b_a)
    og = _layer_swa(xb, batch, seq, rel_bias, w_in_b, sinks_b)
    xf, xb = _out_proj_ln(og, w_out_b, xf, ln_g_b, ln_b_b)
    og = _layer_sb(xb, batch, seq, w_in_c)
    xf, xb = _out_proj_ln(og, w_out_c, xf, ln_g_c, ln_b_c)
    og = _layer_fox(xb, batch, seq, w_in_d, fgate_bias_d)
    xf, xb = _out_proj_ln(og, w_out_d, xf, ln_g_d, ln_b_d)
    return xf.reshape(batch, seq, d_model)
```

```python
import functools
import math

import numpy as np
import jax
import jax.numpy as jnp
from jax import lax
from jax.experimental import pallas as pl
from jax.experimental.pallas import tpu as pltpu

F32 = jnp.float32
BF16 = jnp.bfloat16

N_HEADS = 16
HEAD_DIM = 128
N_KV = 4
HPG = N_HEADS // N_KV
WIDTH = N_HEADS * HEAD_DIM
KV_WIDTH = N_KV * HEAD_DIM
GROUP_W = HPG * HEAD_DIM
REL_BUCKETS = 32
REL_MAX_DIST = 128
CMP_LEN = 32
CMP_STRIDE = 16
SEL_LEN = 64
SEL_TOPK = 8
NSA_WINDOW = 512
SWA_WINDOW = 128
DEPTH = 4
DN_ALPHA = (2 * DEPTH) ** 0.25
LN_EPS = 1e-5
NEG_INF = -1e30
FORCED_SCORE = 1e4
SCALE = HEAD_DIM ** -0.5
LOG2E = math.log2(math.e)
QSCALE = SCALE * LOG2E

LANES = 128
BF16_ROWS = 16
TILES_PER_STEP = 4
M_INIT = -1e29
VMEM_LIMIT = 56 * 1024 * 1024


def _cparams(*sem):
    return pltpu.CompilerParams(dimension_semantics=sem, vmem_limit_bytes=VMEM_LIMIT)


def _silu(z):
    return z / (1.0 + jnp.exp(-z))


def _dot_nt(a, b):
    return lax.dot_general(a, b, (((1,), (1,)), ((), ())), preferred_element_type=F32)


def _dot(a, b):
    return jnp.dot(a, b, preferred_element_type=F32)


def _head(j):
    return slice(j * HEAD_DIM, (j + 1) * HEAD_DIM)


def _group_queries(qt):
    return jnp.concatenate([qt[r * HEAD_DIM:(r + 1) * HEAD_DIM] for r in range(HPG)], axis=1)


def _split3(x):
    a1 = x.astype(BF16)
    r1 = x - a1.astype(F32)
    a2 = r1.astype(BF16)
    a3 = (r1 - a2.astype(F32)).astype(BF16)
    return a1, a2, a3


def _mm_kernel(x_ref, w_ref, o_ref):
    o_ref[...] = _dot(x_ref[...], w_ref[...]).astype(o_ref.dtype)


def _mm_nt_kernel(wt_ref, x_ref, o_ref):
    o_ref[...] = _dot_nt(wt_ref[...], x_ref[...]).astype(o_ref.dtype)


def _matmul_nt(wt, x, name, tm=1024):
    n, k = wt.shape
    m = x.shape[0]
    tn = _feature_tile(n)
    assert m % tm == 0 and n % tn == 0
    return pl.pallas_call(
        _mm_nt_kernel,
        grid=(n // tn, m // tm),
        in_specs=[pl.BlockSpec((tn, k), lambda j, i: (j, 0)),
                  pl.BlockSpec((tm, k), lambda j, i: (i, 0))],
        out_specs=pl.BlockSpec((tn, tm), lambda j, i: (j, i)),
        out_shape=jax.ShapeDtypeStruct((n, m), BF16),
        compiler_params=_cparams("arbitrary", "arbitrary"),
        name=name,
    )(wt, x)


def _feature_tile(n):
    return next(t for t in (1024, 1280, 1536, 768, 512, 256, 128) if n % t == 0)


def _matmul(x, w, out_dtype, tm, tn, name):
    m, k = x.shape
    n = w.shape[1]
    assert m % tm == 0 and n % tn == 0
    return pl.pallas_call(
        _mm_kernel,
        grid=(n // tn, m // tm),
        in_specs=[pl.BlockSpec((tm, k), lambda j, i: (i, 0)),
                  pl.BlockSpec((k, tn), lambda j, i: (0, j))],
        out_specs=pl.BlockSpec((tm, tn), lambda j, i: (i, j)),
        out_shape=jax.ShapeDtypeStruct((m, n), out_dtype),
        compiler_params=_cparams("arbitrary", "arbitrary"),
        name=name,
    )(x, w)


def _outln_kernel(og_ref, w_ref, x_ref, g_ref, b_ref, xo_ref, xb_ref):
    y = _dot(og_ref[...], w_ref[...])
    t = DN_ALPHA * x_ref[...] + y
    mu = jnp.mean(t, axis=-1, keepdims=True)
    d = t - mu
    var = jnp.mean(d * d, axis=-1, keepdims=True)
    out = d * lax.rsqrt(var + LN_EPS) * g_ref[...] + b_ref[...]
    xo_ref[...] = out
    xb_ref[...] = out.astype(BF16)


def _out_proj_ln(og, w_out, x, ln_g, ln_b, tm=512):
    m, k = og.shape
    n = w_out.shape[1]
    return pl.pallas_call(
        _outln_kernel,
        grid=(m // tm,),
        in_specs=[pl.BlockSpec((tm, k), lambda i: (i, 0)),
                  pl.BlockSpec((k, n), lambda i: (0, 0)),
                  pl.BlockSpec((tm, n), lambda i: (i, 0)),
                  pl.BlockSpec((1, n), lambda i: (0, 0)),
                  pl.BlockSpec((1, n), lambda i: (0, 0))],
        out_specs=[pl.BlockSpec((tm, n), lambda i: (i, 0)),
                   pl.BlockSpec((tm, n), lambda i: (i, 0))],
        out_shape=[jax.ShapeDtypeStruct((m, n), F32), jax.ShapeDtypeStruct((m, n), BF16)],
        compiler_params=_cparams("arbitrary"),
        name="out_proj_ln",
    )(og, w_out.astype(BF16), x, ln_g.reshape(1, n), ln_b.reshape(1, n))


def _bucket_np(dist):
    max_exact = REL_BUCKETS // 2
    ratio = np.maximum(dist, max_exact).astype(np.float32) / max_exact
    large = max_exact + (np.log(ratio) / math.log(REL_MAX_DIST / max_exact)
                         * (REL_BUCKETS - max_exact)).astype(np.int32)
    return np.where(dist < max_exact, dist, np.minimum(large, REL_BUCKETS - 1))


def _rel_table(rel_bias, dist, ok):
    bucket = _bucket_np(np.maximum(dist, 0)).reshape(-1)
    onehot_t = np.zeros((REL_BUCKETS, bucket.size), np.float32)
    onehot_t[bucket, np.arange(bucket.size)] = 1.0
    t = jnp.dot(rel_bias.T.astype(F32) * LOG2E, jnp.asarray(onehot_t), precision=lax.Precision.HIGHEST)
    blocked = np.where(ok.reshape(-1), 0.0, NEG_INF).astype(np.float32)
    return (t + blocked[None, :]).reshape((N_HEADS,) + dist.shape)


def _band_tables(rel_bias, nd, window, with_blocked=False):
    key = np.arange(LANES)[:, None]
    qry = np.arange(LANES)[None, :]
    dist = np.stack([LANES * d + qry - key for d in range(nd)])
    ok = dist >= 0
    if window is not None:
        ok &= dist < window
    if with_blocked:
        dist = np.concatenate([dist, np.zeros((1, LANES, LANES), dist.dtype)])
        ok = np.concatenate([ok, np.zeros((1, LANES, LANES), bool)])
    return _rel_table(rel_bias, dist, ok)


def _group_table(tbl_ref, idx):
    return jnp.concatenate([tbl_ref[r, idx] for r in range(HPG)], axis=1)


def _with_ones_rows(vt):
    return jnp.concatenate([vt, jnp.ones((BF16_ROWS, vt.shape[1]), BF16)], axis=0)


def _finish_group(acc):
    o_t = acc[:HEAD_DIM] / jnp.maximum(acc[HEAD_DIM:HEAD_DIM + 1], 1e-30)
    return jnp.concatenate([o_t[:, r * LANES:(r + 1) * LANES].T for r in range(HPG)], axis=1)


def _online_update(x, vt, m, acc):
    m_new = jnp.maximum(m, jnp.max(x, axis=0, keepdims=True))
    p = jnp.exp2(x - m_new)
    acc = jnp.exp2(m - m_new) * acc + _dot(_with_ones_rows(vt), p.astype(BF16))
    return m_new, acc


def _banded_kernel(*refs, nd, has_sink, mixed):
    refs = list(refs)
    qt_ref, k_ref, vt_ref, tbl_ref, z_ref = refs[:5]
    rest = refs[5:]
    sink_ref = rest.pop(0) if has_sink else None
    oc_ref, os_ref, gl_ref = (rest.pop(0), rest.pop(0), rest.pop(0)) if mixed else (None, None, None)
    (o_ref,) = rest
    for u in range(TILES_PER_STEP):
        i = pl.program_id(2) * TILES_PER_STEP + u
        tile = slice(u * LANES, (u + 1) * LANES)
        ks, vts, tbls = [], [], []
        for d in range(nd):
            kb = i - d
            start = pl.multiple_of(jnp.maximum(kb, 0) * LANES, LANES)
            ks.append(k_ref[pl.ds(start, LANES), :])
            vts.append(vt_ref[:, pl.ds(start, LANES)])
            tbls.append(_group_table(tbl_ref, jnp.where(kb >= 0, d, nd)))
        x = _dot(jnp.concatenate(ks, axis=0), _group_queries(qt_ref[:, tile])) + jnp.concatenate(tbls, axis=0)
        m = jnp.max(x, axis=0, keepdims=True)
        if has_sink:
            m = jnp.maximum(m, sink_ref[...])
        p = jnp.exp2(x - m)
        acc = _dot(_with_ones_rows(jnp.concatenate(vts, axis=1)), p.astype(BF16))
        if has_sink:
            rows = lax.broadcasted_iota(jnp.int32, acc.shape, 0)
            acc = acc + jnp.where(rows >= HEAD_DIM, jnp.exp2(sink_ref[...] - m), 0.0)
        o = _finish_group(acc)
        if mixed:
            gates = 1.0 / (1.0 + jnp.exp(-gl_ref[tile, :]))
            oc = oc_ref[tile, :].astype(F32)
            osel = os_ref[tile, :].astype(F32)
            o = jnp.concatenate(
                [gates[:, 3 * r:3 * r + 1] * oc[:, _head(r)] + gates[:, 3 * r + 1:3 * r + 2] * osel[:, _head(r)]
                 + gates[:, 3 * r + 2:3 * r + 3] * o[:, _head(r)] for r in range(HPG)], axis=1)
        o_ref[tile, :] = (o * _silu(z_ref[tile, :].astype(F32))).astype(o_ref.dtype)


def _banded_gqa(h, qvt, batch, seq, kblk, vtblk, zblk, tbl, sinks=None, mix=None, name="banded"):
    step = TILES_PER_STEP * LANES
    nq = seq // step
    nd = tbl.shape[1] - 1
    rows_of_group = lambda blk: pl.BlockSpec((step, GROUP_W), lambda b, g, i: (b * nq + i, blk + g))
    in_specs = [pl.BlockSpec((GROUP_W, step), lambda b, g, i: (g, b * nq + i)),
                pl.BlockSpec((seq, HEAD_DIM), lambda b, g, i: (b, kblk + g)),
                pl.BlockSpec((HEAD_DIM, seq), lambda b, g, i: (vtblk + g, b)),
                pl.BlockSpec((HPG, nd + 1, LANES, LANES), lambda b, g, i: (g, 0, 0, 0)),
                rows_of_group(zblk)]
    args = [qvt, h, qvt, tbl, h]
    if sinks is not None:
        sink_row = jnp.repeat(sinks.astype(F32).reshape(N_KV, HPG) * LOG2E, LANES, axis=1)
        in_specs.append(pl.BlockSpec((None, 1, HPG * LANES), lambda b, g, i: (g, 0, 0)))
        args.append(sink_row.reshape(N_KV, 1, HPG * LANES))
    if mix is not None:
        in_specs += [rows_of_group(0), rows_of_group(0),
                     pl.BlockSpec((step, LANES), lambda b, g, i: (b * nq + i, g))]
        args += list(mix)
    return pl.pallas_call(
        functools.partial(_banded_kernel, nd=nd, has_sink=sinks is not None, mixed=mix is not None),
        grid=(batch, N_KV, nq),
        in_specs=in_specs,
        out_specs=pl.BlockSpec((step, GROUP_W), lambda b, g, i: (b * nq + i, g)),
        out_shape=jax.ShapeDtypeStruct((batch * seq, WIDTH), BF16),
        compiler_params=_cparams("arbitrary", "arbitrary", "arbitrary"),
        name=name,
    )(*args)


def _compress_kernel(kc_ref, vc_ref, pek_ref, w1k_ref, w2k_ref, pev_ref, w1v_ref, w2v_ref,
                     ko_ref, vo_ref, xs_ref, *, nc):
    half = CMP_LEN // 2
    for x_ref, pe_ref, w1_ref, w2_ref, o_ref in ((kc_ref, pek_ref, w1k_ref, w2k_ref, ko_ref),
                                                 (vc_ref, pev_ref, w1v_ref, w2v_ref, vo_ref)):
        xs_ref[...] = x_ref[...].astype(F32)
        first = jnp.zeros((nc, HEAD_DIM), F32)
        second = jnp.zeros((nc, HEAD_DIM), F32)
        for l in range(half):
            xl = xs_ref[pl.ds(l, nc, stride=CMP_STRIDE), :]
            first += _dot((xl + pe_ref[l:l + 1, :]).astype(BF16), w1_ref[l])
            second += _dot((xl + pe_ref[half + l:half + l + 1, :]).astype(BF16), w1_ref[half + l])
        hid = _silu(first + pltpu.roll(second, nc - 1, 0))
        o_ref[...] = _dot(hid.astype(BF16), w2_ref[...]).astype(o_ref.dtype)


def _nsa_compress(h, batch, seq, kcblk, vcblk, pe_k, w1_k, w2_k, pe_v, w1_v, w2_v):
    assert CMP_LEN == 2 * CMP_STRIDE
    nc = seq // CMP_STRIDE
    kv_spec = lambda blk: pl.BlockSpec((seq, HEAD_DIM), lambda b, g: (b, blk + g))
    full = lambda shape: pl.BlockSpec(shape, lambda b, g: (0,) * len(shape))
    out_spec = pl.BlockSpec((None, None, nc, HEAD_DIM), lambda b, g: (b, g, 0, 0))
    out_shape = jax.ShapeDtypeStruct((batch, N_KV, nc, HEAD_DIM), BF16)
    return pl.pallas_call(
        functools.partial(_compress_kernel, nc=nc),
        grid=(batch, N_KV),
        in_specs=[kv_spec(kcblk), kv_spec(vcblk),
                  full((CMP_LEN, HEAD_DIM)), full((CMP_LEN, HEAD_DIM, HEAD_DIM)), full((HEAD_DIM, HEAD_DIM)),
                  full((CMP_LEN, HEAD_DIM)), full((CMP_LEN, HEAD_DIM, HEAD_DIM)), full((HEAD_DIM, HEAD_DIM))],
        out_specs=[out_spec, out_spec],
        out_shape=[out_shape, out_shape],
        scratch_shapes=[pltpu.VMEM((seq, HEAD_DIM), F32)],
        compiler_params=_cparams("arbitrary", "arbitrary"),
        name="nsa_compress",
    )(h, h, pe_k, w1_k.astype(BF16), w2_k.astype(BF16), pe_v, w1_v.astype(BF16), w2_v.astype(BF16))


def _cmp_table(rel_bias, nc):
    u = np.arange(2 * nc)[:, None] - nc
    dist = np.arange(LANES)[None, :] - (u * CMP_STRIDE + CMP_LEN - 1)
    return _rel_table(rel_bias, dist, dist >= 0)


def _inter_t(seq, nc):
    nb = seq // SEL_LEN
    cstart = np.arange(nc) * CMP_STRIDE
    sstart = np.arange(nb) * SEL_LEN
    inter = np.clip(np.minimum(cstart[None, :] + CMP_LEN, sstart[:, None] + SEL_LEN)
                    - np.maximum(cstart[None, :], sstart[:, None]), 0, None) / CMP_LEN
    return inter.astype(np.float32)


def _cmp_select_kernel(qt_ref, kc_ref, vc_ref, tbl_ref, inter_ref, o_ref, sel_ref, *, nb, nc):
    first = pl.program_id(1) * TILES_PER_STEP
    width = TILES_PER_STEP * LANES
    kc = kc_ref[...]
    vc = vc_ref[...]
    per_tile = LANES // CMP_STRIDE
    row0 = [pl.multiple_of(nc - (first + u) * per_tile, per_tile) for u in range(TILES_PER_STEP)]
    outs = []
    p_sum_t = None
    for r in range(HPG):
        tbl = jnp.concatenate([tbl_ref[r, pl.ds(row, nc), :] for row in row0], axis=1)
        st = _dot(kc, qt_ref[r * HEAD_DIM:(r + 1) * HEAD_DIM, :]) + tbl
        pt = jnp.where(tbl > 0.5 * NEG_INF, jnp.exp2(st - jnp.max(st, axis=0, keepdims=True)), 0.0)
        pt = pt / jnp.maximum(jnp.sum(pt, axis=0, keepdims=True), 1e-30)
        p_sum_t = pt if p_sum_t is None else p_sum_t + pt
        outs.append(_dot(pt.T.astype(BF16), vc))
    o_ref[...] = jnp.concatenate(outs, axis=1).astype(o_ref.dtype)

    inter = inter_ref[...]
    imp = sum(_dot(inter, part) for part in _split3(p_sum_t))
    blk = lax.broadcasted_iota(jnp.int32, (nb, width), 0)
    pos = first * LANES + lax.broadcasted_iota(jnp.int32, (nb, width), 1)
    cur = lax.shift_right_logical(pos, int(math.log2(SEL_LEN)))
    allowed = blk * SEL_LEN <= pos
    forced = (blk == 0) | (blk == cur) | (blk == cur - 1)
    imp = jnp.where(allowed, jnp.where(forced, FORCED_SCORE, imp), NEG_INF)
    rank = jnp.zeros((nb, width), F32)
    for c in range(nb):
        row = imp[c:c + 1, :]
        earlier = jnp.where(blk > c, 1.0, 0.0)
        rank += jnp.where(row > imp, 1.0, jnp.where(row == imp, earlier, 0.0))
    sel_t = jnp.where((rank < SEL_TOPK) & (imp > 0.5 * NEG_INF), 1.0, 0.0)
    sel_t = jnp.concatenate([sel_t, jnp.zeros((LANES - nb, width), F32)], axis=0).astype(sel_ref.dtype)
    for u in range(TILES_PER_STEP):
        sel_ref[u] = sel_t[:, u * LANES:(u + 1) * LANES]


def _nsa_cmp_select(qvt, batch, seq, k_cmp, v_cmp, rel_bias):
    step = TILES_PER_STEP * LANES
    nq = seq // step
    nc = seq // CMP_STRIDE
    nb = seq // SEL_LEN
    assert nc == LANES and nb <= LANES
    tbl = _cmp_table(rel_bias, nc)
    inter = jnp.asarray(_inter_t(seq, nc), BF16)
    cmp_spec = pl.BlockSpec((None, None, nc, HEAD_DIM), lambda g, i, b: (b, g, 0, 0))
    return pl.pallas_call(
        functools.partial(_cmp_select_kernel, nb=nb, nc=nc),
        grid=(N_KV, nq, batch),
        in_specs=[pl.BlockSpec((GROUP_W, step), lambda g, i, b: (g, b * nq + i)),
                  cmp_spec, cmp_spec,
                  pl.BlockSpec((HPG, 2 * nc, LANES), lambda g, i, b: (g, 0, 0)),
                  pl.BlockSpec((nb, nc), lambda g, i, b: (0, 0))],
        out_specs=[pl.BlockSpec((step, GROUP_W), lambda g, i, b: (b * nq + i, g)),
                   pl.BlockSpec((None, None, TILES_PER_STEP, LANES, LANES), lambda g, i, b: (b, g, i, 0, 0))],
        out_shape=[jax.ShapeDtypeStruct((batch * seq, WIDTH), BF16),
                   jax.ShapeDtypeStruct((batch, N_KV, seq // LANES, LANES, LANES), BF16)],
        compiler_params=_cparams("arbitrary", "arbitrary", "arbitrary"),
        name="nsa_cmp_select",
    )(qvt, k_cmp, v_cmp, tbl, inter)


SEL_TILES = 2
SEL_KEYS = SEL_TILES * LANES


def _expand_np(seq):
    j = np.arange(LANES)[None, None, :]
    key = np.arange(seq // SEL_KEYS)[:, None, None] * SEL_KEYS + np.arange(SEL_KEYS)[None, :, None]
    return (j == key // SEL_LEN).astype(np.float32)


def _sel_attn_kernel(qt_ref, k_ref, vt_ref, sel_ref, tbl_ref, exp_ref, o_ref):
    last = pl.program_id(2)
    lanes = HPG * LANES
    beyond = tbl_ref.shape[1] - 1
    far = beyond - 1
    tiles = range(SEL_TILES)
    qt_aug = []
    for u in tiles:
        blocked = ((1.0 - sel_ref[u].astype(F32)) * NEG_INF).astype(BF16)
        qt_aug.append(jnp.concatenate([_group_queries(qt_ref[:, u * LANES:(u + 1) * LANES]),
                                       jnp.concatenate([blocked] * HPG, axis=1)], axis=0))

    def scores(u, kt):
        start = pl.multiple_of(kt * SEL_KEYS, SEL_KEYS)
        k_aug = jnp.concatenate([k_ref[pl.ds(start, SEL_KEYS), :], exp_ref[kt]], axis=1)
        return _dot(k_aug, qt_aug[u])

    def body(kt, carry):
        start = pl.multiple_of(kt * SEL_KEYS, SEL_KEYS)
        out = []
        for u in tiles:
            s, m, acc = carry[u]
            s_next = scores(u, jnp.minimum(kt + 1, last))
            behind = [(last - kt) * SEL_TILES + u - c for c in tiles]
            x = s + jnp.concatenate(
                [_group_table(tbl_ref, jnp.where(d >= 0, jnp.minimum(d, far), beyond)) for d in behind], axis=0)
            out.append((s_next,) + _online_update(x, vt_ref[:, pl.ds(start, SEL_KEYS)], m, acc))
        return tuple(out)

    init = tuple((scores(u, 0), jnp.full((1, lanes), M_INIT, F32),
                  jnp.zeros((HEAD_DIM + BF16_ROWS, lanes), F32)) for u in tiles)
    final = lax.fori_loop(0, last + 1, body, init)
    for u in tiles:
        o_ref[u * LANES:(u + 1) * LANES, :] = _finish_group(final[u][2]).astype(o_ref.dtype)


def _nsa_sel_attn(h, qvt, batch, seq, kblk, vtblk, sel, rel_bias):
    nq = seq // SEL_KEYS
    tbl = _band_tables(rel_bias, 3, None, True)
    expand = jnp.asarray(_expand_np(seq), BF16)
    return pl.pallas_call(
        _sel_attn_kernel,
        grid=(batch, N_KV, nq),
        in_specs=[pl.BlockSpec((GROUP_W, SEL_KEYS), lambda b, g, i: (g, b * nq + i)),
                  pl.BlockSpec((seq, HEAD_DIM), lambda b, g, i: (b, kblk + g)),
                  pl.BlockSpec((HEAD_DIM, seq), lambda b, g, i: (vtblk + g, b)),
                  pl.BlockSpec((None, None, SEL_TILES, LANES, LANES), lambda b, g, i: (b, g, i, 0, 0)),
                  pl.BlockSpec((HPG, 4, LANES, LANES), lambda b, g, i: (g, 0, 0, 0)),
                  pl.BlockSpec((seq // SEL_KEYS, SEL_KEYS, LANES), lambda b, g, i: (0, 0, 0))],
        out_specs=pl.BlockSpec((SEL_KEYS, GROUP_W), lambda b, g, i: (b * nq + i, g)),
        out_shape=jax.ShapeDtypeStruct((batch * seq, WIDTH), BF16),
        compiler_params=_cparams("arbitrary", "arbitrary", "arbitrary"),
        name="nsa_sel_attn",
    )(qvt, h, qvt, sel, tbl, expand)


HEADS_PER_STEP = 4
STEP_W = HEADS_PER_STEP * HEAD_DIM


def _pipelined_heads(i, scores, consume, init):
    heads = range(HEADS_PER_STEP)
    ahead = [scores(j, jnp.maximum(i - 1, 0)) for j in heads]
    state = [consume(j, scores(j, i), i, *init, True) for j in heads]

    def body(step, carry):
        kb = i - step
        return tuple((scores(j, jnp.maximum(kb - 1, 0)),) + consume(j, carry[j][0], kb, *carry[j][1:], False)
                     for j in heads)

    carry = lax.fori_loop(1, i + 1, body, tuple((ahead[j],) + state[j] for j in heads))
    return [c[1:] for c in carry]


def _sb_kernel(q_ref, k_ref, v_ref, z_ref, o_ref, *, t):
    i = pl.program_id(2)
    qry = lax.broadcasted_iota(jnp.int32, (t, t), 0)
    key = lax.broadcasted_iota(jnp.int32, (t, t), 1)
    strict = key < qry
    later = jnp.where(qry >= key, 1.0, 0.0).astype(BF16)

    def scores(j, kb):
        start = pl.multiple_of(kb * t, t)
        return _dot_nt(q_ref[:, _head(j)], k_ref[pl.ds(start, t), _head(j)])

    def consume(j, y, kb, run, acc, diag):
        start = pl.multiple_of(kb * t, t)
        neg = -y
        soft = jnp.log(1.0 + jnp.exp2(jnp.minimum(y, neg))) * LOG2E
        log_keep = jnp.minimum(neg, 0.0) - soft
        if diag:
            log_keep = jnp.where(strict, log_keep, 0.0)
        keep = _dot(log_keep.astype(BF16), later)
        a = jnp.exp2(y + keep + run)
        if diag:
            a = jnp.where(strict, a, 0.0)
        acc = acc + _dot(a.astype(BF16), v_ref[pl.ds(start, t), _head(j)])
        return run + keep[:, 0:1], acc

    final = _pipelined_heads(i, scores, consume, (jnp.zeros((t, 1), F32), jnp.zeros((t, HEAD_DIM), F32)))
    o = jnp.concatenate([acc for _, acc in final], axis=1)
    o_ref[...] = (o * _silu(z_ref[...].astype(F32))).astype(o_ref.dtype)


def _sb_call(h, batch, seq, t):
    nq = seq // t
    per_w = WIDTH // STEP_W
    rows = lambda part: pl.BlockSpec((t, STEP_W), lambda b, hp, i: (b * nq + i, part * per_w + hp))
    whole = lambda part: pl.BlockSpec((seq, STEP_W), lambda b, hp, i: (b, part * per_w + hp))
    return pl.pallas_call(
        functools.partial(_sb_kernel, t=t),
        grid=(batch, per_w, nq),
        in_specs=[rows(0), whole(1), whole(2), rows(3)],
        out_specs=rows(0),
        out_shape=jax.ShapeDtypeStruct((batch * seq, WIDTH), BF16),
        compiler_params=_cparams("arbitrary", "arbitrary", "arbitrary"),
        name="stick_breaking",
    )(h, h, h, h)


def _full_attn_call(kernel, h, qvt, batch, seq, t, extra_specs, extra_args, name):
    nq = seq // t
    per_w = WIDTH // STEP_W
    in_specs = [pl.BlockSpec((STEP_W, t), lambda b, hp, i: (hp, b * nq + i)),
                pl.BlockSpec((seq, STEP_W), lambda b, hp, i: (b, hp)),
                pl.BlockSpec((STEP_W, seq), lambda b, hp, i: (per_w + hp, b)),
                pl.BlockSpec((t, STEP_W), lambda b, hp, i: (b * nq + i, per_w + hp))]
    return pl.pallas_call(
        functools.partial(kernel, t=t),
        grid=(batch, per_w, nq),
        in_specs=in_specs + extra_specs,
        out_specs=pl.BlockSpec((t, STEP_W), lambda b, hp, i: (b * nq + i, hp)),
        out_shape=jax.ShapeDtypeStruct((batch * seq, WIDTH), BF16),
        compiler_params=_cparams("arbitrary", "arbitrary", "arbitrary"),
        name=name,
    )(qvt, h, qvt, h, *extra_args)


KEY_PARTS = 3


def _fox_prep_kernel(fl_ref, bias_ref, ct_ref, ck_ref, *, seq):
    row = lax.broadcasted_iota(jnp.int32, (LANES, LANES), 0)
    col = lax.broadcasted_iota(jnp.int32, (LANES, LANES), 1)
    upto = jnp.where(row <= col, 1.0, 0.0).astype(BF16)
    downto = jnp.where(col <= row, 1.0, 0.0).astype(BF16)
    place = [jnp.where((col == row + j * N_HEADS) & (row < N_HEADS), 1.0, 0.0).astype(BF16)
             for j in range(KEY_PARTS)]
    carry_t = jnp.zeros((LANES, 1), F32)
    carry = jnp.zeros((1, LANES), F32)
    for blk in range(seq // LANES):
        x = fl_ref[blk * LANES:(blk + 1) * LANES, :] + bias_ref[...]
        log_f = jnp.minimum(x, 0.0) - jnp.log1p(jnp.exp(-jnp.abs(x)))
        cs_t = sum(_dot(part, upto) for part in _split3(log_f.T)) + carry_t
        ct_ref[:, blk * LANES:(blk + 1) * LANES] = cs_t * LOG2E
        carry_t = cs_t[:, LANES - 1:LANES]
        cs = sum(_dot(downto, part) for part in _split3(log_f)) + carry
        carry = cs[LANES - 1:LANES, :]
        parts = _split3(-(cs * LOG2E))
        ck_ref[blk * LANES:(blk + 1) * LANES, :] = sum(
            _dot(part, sel) for part, sel in zip(parts, place)).astype(ck_ref.dtype)


def _fox_prep(fl, fgate_bias, batch, seq):
    assert KEY_PARTS * N_HEADS <= LANES
    bias = jnp.zeros((1, LANES), F32).at[0, :N_HEADS].set(fgate_bias.astype(F32))
    return pl.pallas_call(
        functools.partial(_fox_prep_kernel, seq=seq),
        grid=(batch,),
        in_specs=[pl.BlockSpec((seq, LANES), lambda b: (b, 0)),
                  pl.BlockSpec((1, LANES), lambda b: (0, 0))],
        out_specs=[pl.BlockSpec((None, LANES, seq), lambda b: (b, 0, 0)),
                   pl.BlockSpec((seq, LANES), lambda b: (b, 0))],
        out_shape=[jax.ShapeDtypeStruct((batch, LANES, seq), F32),
                   jax.ShapeDtypeStruct((batch * seq, LANES), BF16)],
        compiler_params=_cparams("arbitrary"),
        name="fox_prep",
    )(fl, bias)


def _fox_kernel(qt_ref, k_ref, vt_ref, z_ref, crow_ref, ck_ref, o_ref, *, t):
    first_head = pl.program_id(1) * HEADS_PER_STEP
    i = pl.program_id(2)
    key = lax.broadcasted_iota(jnp.int32, (t, t), 0)
    qry = lax.broadcasted_iota(jnp.int32, (t, t), 1)
    causal = key <= qry
    row = lax.broadcasted_iota(jnp.int32, (LANES, t), 0)
    qt_aug = [jnp.concatenate(
        [qt_ref[_head(j), :],
         jnp.where((row % N_HEADS == first_head + j) & (row < KEY_PARTS * N_HEADS), 1.0, 0.0).astype(BF16)], axis=0)
        for j in range(HEADS_PER_STEP)]

    def scores(j, kb):
        start = pl.multiple_of(kb * t, t)
        k_aug = jnp.concatenate([k_ref[pl.ds(start, t), _head(j)], ck_ref[pl.ds(start, t), :]], axis=1)
        return _dot(k_aug, qt_aug[j])

    def consume(j, x, kb, m, acc, diag):
        start = pl.multiple_of(kb * t, t)
        c_q = crow_ref[j, i]
        if diag:
            x = jnp.where(causal, x, NEG_INF)
        m_new = jnp.maximum(m, jnp.max(x, axis=0, keepdims=True) + c_q)
        p = jnp.exp2(x + (c_q - m_new))
        vt = _with_ones_rows(vt_ref[_head(j), pl.ds(start, t)])
        return m_new, jnp.exp2(m - m_new) * acc + _dot(vt, p.astype(BF16))

    final = _pipelined_heads(i, scores, consume,
                             (jnp.full((1, t), M_INIT, F32), jnp.zeros((HEAD_DIM + BF16_ROWS, t), F32)))
    o = jnp.concatenate([(acc[:HEAD_DIM] / jnp.maximum(acc[HEAD_DIM:HEAD_DIM + 1], 1e-30)).T for _, acc in final],
                        axis=1)
    o_ref[...] = (o * _silu(z_ref[...].astype(F32))).astype(o_ref.dtype)


def _w_cols(w, *ranges):
    return jnp.concatenate([w[:, a:b] for a, b in ranges], axis=1).astype(BF16)


def _pad_cols(w, n):
    return jnp.pad(w, ((0, 0), (0, n - w.shape[1]))).astype(BF16)


def _w_rows(w, *ranges):
    cols = [w[:, :WIDTH] * QSCALE] + [w[:, a:b] for a, b in ranges]
    return jnp.concatenate(cols, axis=1).astype(BF16).T


def _project(xb, w_main, name):
    return _matmul(xb, w_main, BF16, 1024, _feature_tile(w_main.shape[1]), name)


def _layer_nsa(xb, batch, seq, rel_bias, w_in, cmp_pe_k, cmp_w1_k, cmp_w2_k, cmp_pe_v, cmp_w1_v, cmp_w2_v):
    kc0, ks0, vs0, kw0, vw0, gate0 = (WIDTH + j * KV_WIDTH for j in (0, 2, 3, 4, 5, 6))
    gate1 = gate0 + 3 * N_HEADS
    h = _project(xb, _w_cols(w_in, (kc0, vs0), (kw0, vw0), (gate1, w_in.shape[1])), "proj_nsa")
    qvt = _matmul_nt(_w_rows(w_in, (vs0, kw0), (vw0, gate0)), xb, "proj_nsa_t")
    per_group = 3 * HPG
    w_gate = w_in[:, gate0:gate1].reshape(-1, N_KV, per_group)
    w_gate = jnp.pad(w_gate, ((0, 0), (0, 0), (0, LANES - per_group))).reshape(-1, N_KV * LANES)
    gl = _matmul(xb, w_gate.astype(BF16), F32, 512, N_KV * LANES, "proj_nsa_gates")
    blk = lambda j: j * N_KV
    vrow = WIDTH // HEAD_DIM
    k_cmp, v_cmp = _nsa_compress(h, batch, seq, blk(0), blk(1), cmp_pe_k, cmp_w1_k, cmp_w2_k,
                                 cmp_pe_v, cmp_w1_v, cmp_w2_v)
    o_cmp, sel = _nsa_cmp_select(qvt, batch, seq, k_cmp, v_cmp, rel_bias)
    o_sel = _nsa_sel_attn(h, qvt, batch, seq, blk(2), vrow, sel, rel_bias)
    nd = NSA_WINDOW // LANES + 1
    return _banded_gqa(h, qvt, batch, seq, blk(3), vrow + N_KV, 4 * KV_WIDTH // GROUP_W,
                       _band_tables(rel_bias, nd, NSA_WINDOW, True), mix=(o_cmp, o_sel, gl), name="nsa_window_mix")


def _layer_swa(xb, batch, seq, rel_bias, w_in, sinks):
    v0, z0 = WIDTH + KV_WIDTH, WIDTH + 2 * KV_WIDTH
    h = _project(xb, _w_cols(w_in, (WIDTH, v0), (z0, w_in.shape[1])), "proj_swa")
    qvt = _matmul_nt(_w_rows(w_in, (v0, z0)), xb, "proj_swa_t")
    nd = SWA_WINDOW // LANES + 1
    return _banded_gqa(h, qvt, batch, seq, 0, WIDTH // HEAD_DIM, KV_WIDTH // GROUP_W,
                       _band_tables(rel_bias, nd, SWA_WINDOW, True), sinks=sinks, name="swa")


def _layer_sb(xb, batch, seq, w_in, t=256):
    w = jnp.concatenate([w_in[:, :WIDTH] * QSCALE, w_in[:, WIDTH:]], axis=1).astype(BF16)
    return _sb_call(_project(xb, w, "proj_sb"), batch, seq, t)


def _layer_fox(xb, batch, seq, w_in, fgate_bias, t=256):
    f0 = 3 * WIDTH
    f1 = f0 + N_HEADS
    h = _project(xb, _w_cols(w_in, (WIDTH, 2 * WIDTH), (f1, w_in.shape[1])), "proj_fox")
    qvt = _matmul_nt(_w_rows(w_in, (2 * WIDTH, f0)), xb, "proj_fox_t")
    fl = _matmul(xb, _pad_cols(w_in[:, f0:f1], LANES), F32, 512, LANES, "proj_fox_gates")
    ct, ck = _fox_prep(fl, fgate_bias, batch, seq)
    nq = seq // t
    extra_specs = [pl.BlockSpec((None, HEADS_PER_STEP, nq, 1, t), lambda b, hp, i: (b, hp, 0, 0, 0)),
                   pl.BlockSpec((seq, LANES), lambda b, hp, i: (b, 0))]
    return _full_attn_call(_fox_kernel, h, qvt, batch, seq, t, extra_specs,
                           [ct.reshape(batch, LANES, nq, 1, t), ck], "forgetting")


def kernel(x, rel_bias, w_in_a, w_out_a, ln_g_a, ln_b_a, cmp_pe_k, cmp_w1_k, cmp_w2_k, cmp_pe_v, cmp_w1_v, cmp_w2_v, w_in_b, w_out_b, ln_g_b, ln_b_b, sinks_b, w_in_c, w_out_c, ln_g_c, ln_b_c, w_in_d, w_out_d, ln_g_d, ln_b_d, fgate_bias_d):
    batch, seq, d_model = x.shape
    xf = x.reshape(batch * seq, d_model)
    xb = xf.astype(BF16)
    og = _layer_nsa(xb, batch, seq, rel_bias, w_in_a, cmp_pe_k, cmp_w1_k, cmp_w2_k, cmp_pe_v, cmp_w1_v, cmp_w2_v)
    xf, xb = _out_proj_ln(og, w_out_a, xf, ln_g_a, ln_b_a)
    og = _layer_swa(xb, batch, seq, rel_bias, w_in_b, sinks_b)
    xf, xb = _out_proj_ln(og, w_out_b, xf, ln_g_b, ln_b_b)
    og = _layer_sb(xb, batch, seq, w_in_c)
    xf, xb = _out_proj_ln(og, w_out_c, xf, ln_g_c, ln_b_c)
    og = _layer_fox(xb, batch, seq, w_in_d, fgate_bias_d)
    xf, xb = _out_proj_ln(og, w_out_d, xf, ln_g_d, ln_b_d)
    return xf.reshape(batch, seq, d_model)
```

```python
import functools
import math

import numpy as np
import jax
import jax.numpy as jnp
from jax import lax
from jax.experimental import pallas as pl
from jax.experimental.pallas import tpu as pltpu

F32 = jnp.float32
BF16 = jnp.bfloat16

N_HEADS = 16
HEAD_DIM = 128
N_KV = 4
HPG = N_HEADS // N_KV
WIDTH = N_HEADS * HEAD_DIM
KV_WIDTH = N_KV * HEAD_DIM
GROUP_W = HPG * HEAD_DIM
REL_BUCKETS = 32
REL_MAX_DIST = 128
CMP_LEN = 32
CMP_STRIDE = 16
SEL_LEN = 64
SEL_TOPK = 8
NSA_WINDOW = 512
SWA_WINDOW = 128
DEPTH = 4
DN_ALPHA = (2 * DEPTH) ** 0.25
LN_EPS = 1e-5
NEG_INF = -1e30
FORCED_SCORE = 1e4
SCALE = HEAD_DIM ** -0.5
LOG2E = math.log2(math.e)
QSCALE = SCALE * LOG2E

LANES = 128
BF16_ROWS = 16
TILES_PER_STEP = 4
M_INIT = -1e29
VMEM_LIMIT = 56 * 1024 * 1024


def _cparams(*sem):
    return pltpu.CompilerParams(dimension_semantics=sem, vmem_limit_bytes=VMEM_LIMIT)


def _silu(z):
    return z / (1.0 + jnp.exp(-z))


def _dot_nt(a, b):
    return lax.dot_general(a, b, (((1,), (1,)), ((), ())), preferred_element_type=F32)


def _dot(a, b):
    return jnp.dot(a, b, preferred_element_type=F32)


def _head(j):
    return slice(j * HEAD_DIM, (j + 1) * HEAD_DIM)


def _group_queries(qt):
    return jnp.concatenate([qt[r * HEAD_DIM:(r + 1) * HEAD_DIM] for r in range(HPG)], axis=1)


def _split3(x):
    a1 = x.astype(BF16)
    r1 = x - a1.astype(F32)
    a2 = r1.astype(BF16)
    a3 = (r1 - a2.astype(F32)).astype(BF16)
    return a1, a2, a3


def _mm_kernel(x_ref, w_ref, o_ref):
    o_ref[...] = _dot(x_ref[...], w_ref[...]).astype(o_ref.dtype)


def _mm_t_kernel(x_ref, w_ref, o_ref):
    y = lax.dot_general(w_ref[...], x_ref[...], (((0,), (1,)), ((), ())), preferred_element_type=F32)
    o_ref[...] = y.astype(o_ref.dtype)


def _matmul_t(x, w, name, tm=1024):
    m, k = x.shape
    n = w.shape[1]
    tn = _feature_tile(n)
    assert m % tm == 0 and n % tn == 0
    return pl.pallas_call(
        _mm_t_kernel,
        grid=(n // tn, m // tm),
        in_specs=[pl.BlockSpec((tm, k), lambda j, i: (i, 0)),
                  pl.BlockSpec((k, tn), lambda j, i: (0, j))],
        out_specs=pl.BlockSpec((tn, tm), lambda j, i: (j, i)),
        out_shape=jax.ShapeDtypeStruct((n, m), BF16),
        compiler_params=_cparams("arbitrary", "arbitrary"),
        name=name,
    )(x, w)


def _feature_tile(n):
    return next(t for t in (1024, 1280, 1536, 768, 512, 256, 128) if n % t == 0)


def _matmul(x, w, out_dtype, tm, tn, name):
    m, k = x.shape
    n = w.shape[1]
    assert m % tm == 0 and n % tn == 0
    return pl.pallas_call(
        _mm_kernel,
        grid=(n // tn, m // tm),
        in_specs=[pl.BlockSpec((tm, k), lambda j, i: (i, 0)),
                  pl.BlockSpec((k, tn), lambda j, i: (0, j))],
        out_specs=pl.BlockSpec((tm, tn), lambda j, i: (i, j)),
        out_shape=jax.ShapeDtypeStruct((m, n), out_dtype),
        compiler_params=_cparams("arbitrary", "arbitrary"),
        name=name,
    )(x, w)


def _outln_kernel(og_ref, w_ref, x_ref, g_ref, b_ref, xo_ref, xb_ref):
    y = _dot(og_ref[...], w_ref[...])
    t = DN_ALPHA * x_ref[...] + y
    mu = jnp.mean(t, axis=-1, keepdims=True)
    d = t - mu
    var = jnp.mean(d * d, axis=-1, keepdims=True)
    out = d * lax.rsqrt(var + LN_EPS) * g_ref[...] + b_ref[...]
    xo_ref[...] = out
    xb_ref[...] = out.astype(BF16)


def _out_proj_ln(og, w_out, x, ln_g, ln_b, tm=512):
    m, k = og.shape
    n = w_out.shape[1]
    return pl.pallas_call(
        _outln_kernel,
        grid=(m // tm,),
        in_specs=[pl.BlockSpec((tm, k), lambda i: (i, 0)),
                  pl.BlockSpec((k, n), lambda i: (0, 0)),
                  pl.BlockSpec((tm, n), lambda i: (i, 0)),
                  pl.BlockSpec((1, n), lambda i: (0, 0)),
                  pl.BlockSpec((1, n), lambda i: (0, 0))],
        out_specs=[pl.BlockSpec((tm, n), lambda i: (i, 0)),
                   pl.BlockSpec((tm, n), lambda i: (i, 0))],
        out_shape=[jax.ShapeDtypeStruct((m, n), F32), jax.ShapeDtypeStruct((m, n), BF16)],
        compiler_params=_cparams("arbitrary"),
        name="out_proj_ln",
    )(og, w_out.astype(BF16), x, ln_g.reshape(1, n), ln_b.reshape(1, n))


def _bucket_np(dist):
    max_exact = REL_BUCKETS // 2
    ratio = np.maximum(dist, max_exact).astype(np.float32) / max_exact
    large = max_exact + (np.log(ratio) / math.log(REL_MAX_DIST / max_exact)
                         * (REL_BUCKETS - max_exact)).astype(np.int32)
    return np.where(dist < max_exact, dist, np.minimum(large, REL_BUCKETS - 1))


def _rel_table(rel_bias, dist, ok):
    bucket = _bucket_np(np.maximum(dist, 0)).reshape(-1)
    onehot_t = np.zeros((REL_BUCKETS, bucket.size), np.float32)
    onehot_t[bucket, np.arange(bucket.size)] = 1.0
    t = jnp.dot(rel_bias.T.astype(F32) * LOG2E, jnp.asarray(onehot_t), precision=lax.Precision.HIGHEST)
    blocked = np.where(ok.reshape(-1), 0.0, NEG_INF).astype(np.float32)
    return (t + blocked[None, :]).reshape((N_HEADS,) + dist.shape)


def _band_tables(rel_bias, nd, window, with_blocked=False):
    key = np.arange(LANES)[:, None]
    qry = np.arange(LANES)[None, :]
    dist = np.stack([LANES * d + qry - key for d in range(nd)])
    ok = dist >= 0
    if window is not None:
        ok &= dist < window
    if with_blocked:
        dist = np.concatenate([dist, np.zeros((1, LANES, LANES), dist.dtype)])
        ok = np.concatenate([ok, np.zeros((1, LANES, LANES), bool)])
    return _rel_table(rel_bias, dist, ok)


def _group_table(tbl_ref, idx):
    return jnp.concatenate([tbl_ref[r, idx] for r in range(HPG)], axis=1)


def _with_ones_rows(vt):
    return jnp.concatenate([vt, jnp.ones((BF16_ROWS, vt.shape[1]), BF16)], axis=0)


def _finish_group(acc):
    o_t = acc[:HEAD_DIM] / jnp.maximum(acc[HEAD_DIM:HEAD_DIM + 1], 1e-30)
    return jnp.concatenate([o_t[:, r * LANES:(r + 1) * LANES].T for r in range(HPG)], axis=1)


def _online_update(x, vt, m, acc):
    m_new = jnp.maximum(m, jnp.max(x, axis=0, keepdims=True))
    p = jnp.exp2(x - m_new)
    acc = jnp.exp2(m - m_new) * acc + _dot(_with_ones_rows(vt), p.astype(BF16))
    return m_new, acc


def _banded_kernel(*refs, nd, has_sink, mixed):
    refs = list(refs)
    qt_ref, k_ref, vt_ref, tbl_ref, z_ref = refs[:5]
    rest = refs[5:]
    sink_ref = rest.pop(0) if has_sink else None
    oc_ref, os_ref, gl_ref = (rest.pop(0), rest.pop(0), rest.pop(0)) if mixed else (None, None, None)
    (o_ref,) = rest
    for u in range(TILES_PER_STEP):
        i = pl.program_id(2) * TILES_PER_STEP + u
        tile = slice(u * LANES, (u + 1) * LANES)
        ks, vts, tbls = [], [], []
        for d in range(nd):
            kb = i - d
            start = pl.multiple_of(jnp.maximum(kb, 0) * LANES, LANES)
            ks.append(k_ref[pl.ds(start, LANES), :])
            vts.append(vt_ref[:, pl.ds(start, LANES)])
            tbls.append(_group_table(tbl_ref, jnp.where(kb >= 0, d, nd)))
        x = _dot(jnp.concatenate(ks, axis=0), _group_queries(qt_ref[:, tile])) + jnp.concatenate(tbls, axis=0)
        m = jnp.max(x, axis=0, keepdims=True)
        if has_sink:
            m = jnp.maximum(m, sink_ref[...])
        p = jnp.exp2(x - m)
        acc = _dot(_with_ones_rows(jnp.concatenate(vts, axis=1)), p.astype(BF16))
        if has_sink:
            rows = lax.broadcasted_iota(jnp.int32, acc.shape, 0)
            acc = acc + jnp.where(rows >= HEAD_DIM, jnp.exp2(sink_ref[...] - m), 0.0)
        o = _finish_group(acc)
        if mixed:
            gates = 1.0 / (1.0 + jnp.exp(-gl_ref[tile, :]))
            oc = oc_ref[tile, :].astype(F32)
            osel = os_ref[tile, :].astype(F32)
            o = jnp.concatenate(
                [gates[:, 3 * r:3 * r + 1] * oc[:, _head(r)] + gates[:, 3 * r + 1:3 * r + 2] * osel[:, _head(r)]
                 + gates[:, 3 * r + 2:3 * r + 3] * o[:, _head(r)] for r in range(HPG)], axis=1)
        o_ref[tile, :] = (o * _silu(z_ref[tile, :].astype(F32))).astype(o_ref.dtype)


def _banded_gqa(h, qvt, batch, seq, kblk, vtblk, zblk, tbl, sinks=None, mix=None, name="banded"):
    step = TILES_PER_STEP * LANES
    nq = seq // step
    nd = tbl.shape[1] - 1
    rows_of_group = lambda blk: pl.BlockSpec((step, GROUP_W), lambda b, g, i: (b * nq + i, blk + g))
    in_specs = [pl.BlockSpec((GROUP_W, step), lambda b, g, i: (g, b * nq + i)),
                pl.BlockSpec((seq, HEAD_DIM), lambda b, g, i: (b, kblk + g)),
                pl.BlockSpec((HEAD_DIM, seq), lambda b, g, i: (vtblk + g, b)),
                pl.BlockSpec((HPG, nd + 1, LANES, LANES), lambda b, g, i: (g, 0, 0, 0)),
                rows_of_group(zblk)]
    args = [qvt, h, qvt, tbl, h]
    if sinks is not None:
        sink_row = jnp.repeat(sinks.astype(F32).reshape(N_KV, HPG) * LOG2E, LANES, axis=1)
        in_specs.append(pl.BlockSpec((None, 1, HPG * LANES), lambda b, g, i: (g, 0, 0)))
        args.append(sink_row.reshape(N_KV, 1, HPG * LANES))
    if mix is not None:
        in_specs += [rows_of_group(0), rows_of_group(0),
                     pl.BlockSpec((step, LANES), lambda b, g, i: (b * nq + i, g))]
        args += list(mix)
    return pl.pallas_call(
        functools.partial(_banded_kernel, nd=nd, has_sink=sinks is not None, mixed=mix is not None),
        grid=(batch, N_KV, nq),
        in_specs=in_specs,
        out_specs=pl.BlockSpec((step, GROUP_W), lambda b, g, i: (b * nq + i, g)),
        out_shape=jax.ShapeDtypeStruct((batch * seq, WIDTH), BF16),
        compiler_params=_cparams("arbitrary", "arbitrary", "arbitrary"),
        name=name,
    )(*args)


def _compress_kernel(kc_ref, vc_ref, pek_ref, w1k_ref, w2k_ref, pev_ref, w1v_ref, w2v_ref,
                     ko_ref, vo_ref, xs_ref, *, nc):
    half = CMP_LEN // 2
    for x_ref, pe_ref, w1_ref, w2_ref, o_ref in ((kc_ref, pek_ref, w1k_ref, w2k_ref, ko_ref),
                                                 (vc_ref, pev_ref, w1v_ref, w2v_ref, vo_ref)):
        xs_ref[...] = x_ref[...].astype(F32)
        first = jnp.zeros((nc, HEAD_DIM), F32)
        second = jnp.zeros((nc, HEAD_DIM), F32)
        for l in range(half):
            xl = xs_ref[pl.ds(l, nc, stride=CMP_STRIDE), :]
            first += _dot((xl + pe_ref[l:l + 1, :]).astype(BF16), w1_ref[l])
            second += _dot((xl + pe_ref[half + l:half + l + 1, :]).astype(BF16), w1_ref[half + l])
        hid = _silu(first + pltpu.roll(second, nc - 1, 0))
        o_ref[...] = _dot(hid.astype(BF16), w2_ref[...]).astype(o_ref.dtype)


def _nsa_compress(h, batch, seq, kcblk, vcblk, pe_k, w1_k, w2_k, pe_v, w1_v, w2_v):
    assert CMP_LEN == 2 * CMP_STRIDE
    nc = seq // CMP_STRIDE
    kv_spec = lambda blk: pl.BlockSpec((seq, HEAD_DIM), lambda b, g: (b, blk + g))
    full = lambda shape: pl.BlockSpec(shape, lambda b, g: (0,) * len(shape))
    out_spec = pl.BlockSpec((None, None, nc, HEAD_DIM), lambda b, g: (b, g, 0, 0))
    out_shape = jax.ShapeDtypeStruct((batch, N_KV, nc, HEAD_DIM), BF16)
    return pl.pallas_call(
        functools.partial(_compress_kernel, nc=nc),
        grid=(batch, N_KV),
        in_specs=[kv_spec(kcblk), kv_spec(vcblk),
                  full((CMP_LEN, HEAD_DIM)), full((CMP_LEN, HEAD_DIM, HEAD_DIM)), full((HEAD_DIM, HEAD_DIM)),
                  full((CMP_LEN, HEAD_DIM)), full((CMP_LEN, HEAD_DIM, HEAD_DIM)), full((HEAD_DIM, HEAD_DIM))],
        out_specs=[out_spec, out_spec],
        out_shape=[out_shape, out_shape],
        scratch_shapes=[pltpu.VMEM((seq, HEAD_DIM), F32)],
        compiler_params=_cparams("arbitrary", "arbitrary"),
        name="nsa_compress",
    )(h, h, pe_k, w1_k.astype(BF16), w2_k.astype(BF16), pe_v, w1_v.astype(BF16), w2_v.astype(BF16))


def _cmp_table(rel_bias, nc):
    u = np.arange(2 * nc)[:, None] - nc
    dist = np.arange(LANES)[None, :] - (u * CMP_STRIDE + CMP_LEN - 1)
    return _rel_table(rel_bias, dist, dist >= 0)


def _inter_t(seq, nc):
    nb = seq // SEL_LEN
    cstart = np.arange(nc) * CMP_STRIDE
    sstart = np.arange(nb) * SEL_LEN
    inter = np.clip(np.minimum(cstart[None, :] + CMP_LEN, sstart[:, None] + SEL_LEN)
                    - np.maximum(cstart[None, :], sstart[:, None]), 0, None) / CMP_LEN
    return inter.astype(np.float32)


def _cmp_select_kernel(qt_ref, kc_ref, vc_ref, tbl_ref, inter_ref, o_ref, sel_ref, *, nb, nc):
    first = pl.program_id(1) * TILES_PER_STEP
    width = TILES_PER_STEP * LANES
    kc = kc_ref[...]
    vc = vc_ref[...]
    per_tile = LANES // CMP_STRIDE
    row0 = [pl.multiple_of(nc - (first + u) * per_tile, per_tile) for u in range(TILES_PER_STEP)]
    outs = []
    p_sum_t = None
    for r in range(HPG):
        tbl = jnp.concatenate([tbl_ref[r, pl.ds(row, nc), :] for row in row0], axis=1)
        st = _dot(kc, qt_ref[r * HEAD_DIM:(r + 1) * HEAD_DIM, :]) + tbl
        pt = jnp.where(tbl > 0.5 * NEG_INF, jnp.exp2(st - jnp.max(st, axis=0, keepdims=True)), 0.0)
        pt = pt / jnp.maximum(jnp.sum(pt, axis=0, keepdims=True), 1e-30)
        p_sum_t = pt if p_sum_t is None else p_sum_t + pt
        outs.append(_dot(pt.T.astype(BF16), vc))
    o_ref[...] = jnp.concatenate(outs, axis=1).astype(o_ref.dtype)

    inter = inter_ref[...]
    imp = sum(_dot(inter, part) for part in _split3(p_sum_t))
    blk = lax.broadcasted_iota(jnp.int32, (nb, width), 0)
    pos = first * LANES + lax.broadcasted_iota(jnp.int32, (nb, width), 1)
    cur = lax.shift_right_logical(pos, int(math.log2(SEL_LEN)))
    allowed = blk * SEL_LEN <= pos
    forced = (blk == 0) | (blk == cur) | (blk == cur - 1)
    imp = jnp.where(allowed, jnp.where(forced, FORCED_SCORE, imp), NEG_INF)
    rank = jnp.zeros((nb, width), F32)
    for c in range(nb):
        row = imp[c:c + 1, :]
        earlier = jnp.where(blk > c, 1.0, 0.0)
        rank += jnp.where(row > imp, 1.0, jnp.where(row == imp, earlier, 0.0))
    sel_t = jnp.where((rank < SEL_TOPK) & (imp > 0.5 * NEG_INF), 1.0, 0.0)
    sel_t = jnp.concatenate([sel_t, jnp.zeros((LANES - nb, width), F32)], axis=0).astype(sel_ref.dtype)
    for u in range(TILES_PER_STEP):
        sel_ref[u] = sel_t[:, u * LANES:(u + 1) * LANES]


def _nsa_cmp_select(qvt, batch, seq, k_cmp, v_cmp, rel_bias):
    step = TILES_PER_STEP * LANES
    nq = seq // step
    nc = seq // CMP_STRIDE
    nb = seq // SEL_LEN
    assert nc == LANES and nb <= LANES
    tbl = _cmp_table(rel_bias, nc)
    inter = jnp.asarray(_inter_t(seq, nc), BF16)
    cmp_spec = pl.BlockSpec((None, None, nc, HEAD_DIM), lambda g, i, b: (b, g, 0, 0))
    return pl.pallas_call(
        functools.partial(_cmp_select_kernel, nb=nb, nc=nc),
        grid=(N_KV, nq, batch),
        in_specs=[pl.BlockSpec((GROUP_W, step), lambda g, i, b: (g, b * nq + i)),
                  cmp_spec, cmp_spec,
                  pl.BlockSpec((HPG, 2 * nc, LANES), lambda g, i, b: (g, 0, 0)),
                  pl.BlockSpec((nb, nc), lambda g, i, b: (0, 0))],
        out_specs=[pl.BlockSpec((step, GROUP_W), lambda g, i, b: (b * nq + i, g)),
                   pl.BlockSpec((None, None, TILES_PER_STEP, LANES, LANES), lambda g, i, b: (b, g, i, 0, 0))],
        out_shape=[jax.ShapeDtypeStruct((batch * seq, WIDTH), BF16),
                   jax.ShapeDtypeStruct((batch, N_KV, seq // LANES, LANES, LANES), BF16)],
        compiler_params=_cparams("arbitrary", "arbitrary", "arbitrary"),
        name="nsa_cmp_select",
    )(qvt, k_cmp, v_cmp, tbl, inter)


SEL_TILES = 2
SEL_KEYS = SEL_TILES * LANES


def _expand_np(seq):
    j = np.arange(LANES)[None, None, :]
    key = np.arange(seq // SEL_KEYS)[:, None, None] * SEL_KEYS + np.arange(SEL_KEYS)[None, :, None]
    return (j == key // SEL_LEN).astype(np.float32)


def _sel_attn_kernel(qt_ref, k_ref, vt_ref, sel_ref, tbl_ref, exp_ref, o_ref):
    last = pl.program_id(2)
    lanes = HPG * LANES
    beyond = tbl_ref.shape[1] - 1
    far = beyond - 1
    tiles = range(SEL_TILES)
    qt_aug = []
    for u in tiles:
        blocked = ((1.0 - sel_ref[u].astype(F32)) * NEG_INF).astype(BF16)
        qt_aug.append(jnp.concatenate([_group_queries(qt_ref[:, u * LANES:(u + 1) * LANES]),
                                       jnp.concatenate([blocked] * HPG, axis=1)], axis=0))

    def scores(u, kt):
        start = pl.multiple_of(kt * SEL_KEYS, SEL_KEYS)
        k_aug = jnp.concatenate([k_ref[pl.ds(start, SEL_KEYS), :], exp_ref[kt]], axis=1)
        return _dot(k_aug, qt_aug[u])

    def body(kt, carry):
        start = pl.multiple_of(kt * SEL_KEYS, SEL_KEYS)
        out = []
        for u in tiles:
            s, m, acc = carry[u]
            s_next = scores(u, jnp.minimum(kt + 1, last))
            behind = [(last - kt) * SEL_TILES + u - c for c in tiles]
            x = s + jnp.concatenate(
                [_group_table(tbl_ref, jnp.where(d >= 0, jnp.minimum(d, far), beyond)) for d in behind], axis=0)
            out.append((s_next,) + _online_update(x, vt_ref[:, pl.ds(start, SEL_KEYS)], m, acc))
        return tuple(out)

    init = tuple((scores(u, 0), jnp.full((1, lanes), M_INIT, F32),
                  jnp.zeros((HEAD_DIM + BF16_ROWS, lanes), F32)) for u in tiles)
    final = lax.fori_loop(0, last + 1, body, init)
    for u in tiles:
        o_ref[u * LANES:(u + 1) * LANES, :] = _finish_group(final[u][2]).astype(o_ref.dtype)


def _nsa_sel_attn(h, qvt, batch, seq, kblk, vtblk, sel, rel_bias):
    nq = seq // SEL_KEYS
    tbl = _band_tables(rel_bias, 3, None, True)
    expand = jnp.asarray(_expand_np(seq), BF16)
    return pl.pallas_call(
        _sel_attn_kernel,
        grid=(batch, N_KV, nq),
        in_specs=[pl.BlockSpec((GROUP_W, SEL_KEYS), lambda b, g, i: (g, b * nq + i)),
                  pl.BlockSpec((seq, HEAD_DIM), lambda b, g, i: (b, kblk + g)),
                  pl.BlockSpec((HEAD_DIM, seq), lambda b, g, i: (vtblk + g, b)),
                  pl.BlockSpec((None, None, SEL_TILES, LANES, LANES), lambda b, g, i: (b, g, i, 0, 0)),
                  pl.BlockSpec((HPG, 4, LANES, LANES), lambda b, g, i: (g, 0, 0, 0)),
                  pl.BlockSpec((seq // SEL_KEYS, SEL_KEYS, LANES), lambda b, g, i: (0, 0, 0))],
        out_specs=pl.BlockSpec((SEL_KEYS, GROUP_W), lambda b, g, i: (b * nq + i, g)),
        out_shape=jax.ShapeDtypeStruct((batch * seq, WIDTH), BF16),
        compiler_params=_cparams("arbitrary", "arbitrary", "arbitrary"),
        name="nsa_sel_attn",
    )(qvt, h, qvt, sel, tbl, expand)


HEADS_PER_STEP = 4
STEP_W = HEADS_PER_STEP * HEAD_DIM


def _pipelined_heads(i, scores, consume, init):
    heads = range(HEADS_PER_STEP)
    ahead = [scores(j, jnp.maximum(i - 1, 0)) for j in heads]
    state = [consume(j, scores(j, i), i, *init, True) for j in heads]

    def body(step, carry):
        kb = i - step
        return tuple((scores(j, jnp.maximum(kb - 1, 0)),) + consume(j, carry[j][0], kb, *carry[j][1:], False)
                     for j in heads)

    carry = lax.fori_loop(1, i + 1, body, tuple((ahead[j],) + state[j] for j in heads))
    return [c[1:] for c in carry]


def _sb_kernel(q_ref, k_ref, v_ref, z_ref, o_ref, *, t):
    i = pl.program_id(2)
    qry = lax.broadcasted_iota(jnp.int32, (t, t), 0)
    key = lax.broadcasted_iota(jnp.int32, (t, t), 1)
    strict = key < qry
    later = jnp.where(qry >= key, 1.0, 0.0).astype(BF16)

    def scores(j, kb):
        start = pl.multiple_of(kb * t, t)
        return _dot_nt(q_ref[:, _head(j)], k_ref[pl.ds(start, t), _head(j)])

    def consume(j, y, kb, run, acc, diag):
        start = pl.multiple_of(kb * t, t)
        neg = -y
        soft = jnp.log(1.0 + jnp.exp2(jnp.minimum(y, neg))) * LOG2E
        log_keep = jnp.minimum(neg, 0.0) - soft
        if diag:
            log_keep = jnp.where(strict, log_keep, 0.0)
        keep = _dot(log_keep.astype(BF16), later)
        a = jnp.exp2(y + keep + run)
        if diag:
            a = jnp.where(strict, a, 0.0)
        acc = acc + _dot(a.astype(BF16), v_ref[pl.ds(start, t), _head(j)])
        return run + keep[:, 0:1], acc

    final = _pipelined_heads(i, scores, consume, (jnp.zeros((t, 1), F32), jnp.zeros((t, HEAD_DIM), F32)))
    o = jnp.concatenate([acc for _, acc in final], axis=1)
    o_ref[...] = (o * _silu(z_ref[...].astype(F32))).astype(o_ref.dtype)


def _sb_call(h, batch, seq, t):
    nq = seq // t
    per_w = WIDTH // STEP_W
    rows = lambda part: pl.BlockSpec((t, STEP_W), lambda b, hp, i: (b * nq + i, part * per_w + hp))
    whole = lambda part: pl.BlockSpec((seq, STEP_W), lambda b, hp, i: (b, part * per_w + hp))
    return pl.pallas_call(
        functools.partial(_sb_kernel, t=t),
        grid=(batch, per_w, nq),
        in_specs=[rows(0), whole(1), whole(2), rows(3)],
        out_specs=rows(0),
        out_shape=jax.ShapeDtypeStruct((batch * seq, WIDTH), BF16),
        compiler_params=_cparams("arbitrary", "arbitrary", "arbitrary"),
        name="stick_breaking",
    )(h, h, h, h)


def _full_attn_call(kernel, h, qvt, batch, seq, t, extra_specs, extra_args, name):
    nq = seq // t
    per_w = WIDTH // STEP_W
    in_specs = [pl.BlockSpec((STEP_W, t), lambda b, hp, i: (hp, b * nq + i)),
                pl.BlockSpec((seq, STEP_W), lambda b, hp, i: (b, hp)),
                pl.BlockSpec((STEP_W, seq), lambda b, hp, i: (per_w + hp, b)),
                pl.BlockSpec((t, STEP_W), lambda b, hp, i: (b * nq + i, per_w + hp))]
    return pl.pallas_call(
        functools.partial(kernel, t=t),
        grid=(batch, per_w, nq),
        in_specs=in_specs + extra_specs,
        out_specs=pl.BlockSpec((t, STEP_W), lambda b, hp, i: (b * nq + i, hp)),
        out_shape=jax.ShapeDtypeStruct((batch * seq, WIDTH), BF16),
        compiler_params=_cparams("arbitrary", "arbitrary", "arbitrary"),
        name=name,
    )(qvt, h, qvt, h, *extra_args)


KEY_PARTS = 3


def _fox_prep_kernel(fl_ref, bias_ref, ct_ref, ck_ref, *, seq):
    row = lax.broadcasted_iota(jnp.int32, (LANES, LANES), 0)
    col = lax.broadcasted_iota(jnp.int32, (LANES, LANES), 1)
    upto = jnp.where(row <= col, 1.0, 0.0).astype(BF16)
    downto = jnp.where(col <= row, 1.0, 0.0).astype(BF16)
    place = [jnp.where((col == row + j * N_HEADS) & (row < N_HEADS), 1.0, 0.0).astype(BF16)
             for j in range(KEY_PARTS)]
    carry_t = jnp.zeros((LANES, 1), F32)
    carry = jnp.zeros((1, LANES), F32)
    for blk in range(seq // LANES):
        x = fl_ref[blk * LANES:(blk + 1) * LANES, :] + bias_ref[...]
        log_f = jnp.minimum(x, 0.0) - jnp.log1p(jnp.exp(-jnp.abs(x)))
        cs_t = sum(_dot(part, upto) for part in _split3(log_f.T)) + carry_t
        ct_ref[:, blk * LANES:(blk + 1) * LANES] = cs_t * LOG2E
        carry_t = cs_t[:, LANES - 1:LANES]
        cs = sum(_dot(downto, part) for part in _split3(log_f)) + carry
        carry = cs[LANES - 1:LANES, :]
        parts = _split3(-(cs * LOG2E))
        ck_ref[blk * LANES:(blk + 1) * LANES, :] = sum(
            _dot(part, sel) for part, sel in zip(parts, place)).astype(ck_ref.dtype)


def _fox_prep(fl, fgate_bias, batch, seq):
    assert KEY_PARTS * N_HEADS <= LANES
    bias = jnp.zeros((1, LANES), F32).at[0, :N_HEADS].set(fgate_bias.astype(F32))
    return pl.pallas_call(
        functools.partial(_fox_prep_kernel, seq=seq),
        grid=(batch,),
        in_specs=[pl.BlockSpec((seq, LANES), lambda b: (b, 0)),
                  pl.BlockSpec((1, LANES), lambda b: (0, 0))],
        out_specs=[pl.BlockSpec((None, LANES, seq), lambda b: (b, 0, 0)),
                   pl.BlockSpec((seq, LANES), lambda b: (b, 0))],
        out_shape=[jax.ShapeDtypeStruct((batch, LANES, seq), F32),
                   jax.ShapeDtypeStruct((batch * seq, LANES), BF16)],
        compiler_params=_cparams("arbitrary"),
        name="fox_prep",
    )(fl, bias)


def _fox_kernel(qt_ref, k_ref, vt_ref, z_ref, crow_ref, ck_ref, o_ref, *, t):
    first_head = pl.program_id(1) * HEADS_PER_STEP
    i = pl.program_id(2)
    key = lax.broadcasted_iota(jnp.int32, (t, t), 0)
    qry = lax.broadcasted_iota(jnp.int32, (t, t), 1)
    causal = key <= qry
    row = lax.broadcasted_iota(jnp.int32, (LANES, t), 0)
    qt_aug = [jnp.concatenate(
        [qt_ref[_head(j), :],
         jnp.where((row % N_HEADS == first_head + j) & (row < KEY_PARTS * N_HEADS), 1.0, 0.0).astype(BF16)], axis=0)
        for j in range(HEADS_PER_STEP)]

    def scores(j, kb):
        start = pl.multiple_of(kb * t, t)
        k_aug = jnp.concatenate([k_ref[pl.ds(start, t), _head(j)], ck_ref[pl.ds(start, t), :]], axis=1)
        return _dot(k_aug, qt_aug[j])

    def consume(j, x, kb, m, acc, diag):
        start = pl.multiple_of(kb * t, t)
        c_q = crow_ref[j, i]
        if diag:
            x = jnp.where(causal, x, NEG_INF)
        m_new = jnp.maximum(m, jnp.max(x, axis=0, keepdims=True) + c_q)
        p = jnp.exp2(x + (c_q - m_new))
        vt = _with_ones_rows(vt_ref[_head(j), pl.ds(start, t)])
        return m_new, jnp.exp2(m - m_new) * acc + _dot(vt, p.astype(BF16))

    final = _pipelined_heads(i, scores, consume,
                             (jnp.full((1, t), M_INIT, F32), jnp.zeros((HEAD_DIM + BF16_ROWS, t), F32)))
    o = jnp.concatenate([(acc[:HEAD_DIM] / jnp.maximum(acc[HEAD_DIM:HEAD_DIM + 1], 1e-30)).T for _, acc in final],
                        axis=1)
    o_ref[...] = (o * _silu(z_ref[...].astype(F32))).astype(o_ref.dtype)


def _w_cols(w, *ranges):
    return jnp.concatenate([w[:, a:b] for a, b in ranges], axis=1).astype(BF16)


def _pad_cols(w, n):
    return jnp.pad(w, ((0, 0), (0, n - w.shape[1]))).astype(BF16)


def _w_q_and(w, *ranges):
    cols = [w[:, :WIDTH] * QSCALE] + [w[:, a:b] for a, b in ranges]
    return jnp.concatenate(cols, axis=1).astype(BF16)


def _project(xb, w_main, name):
    return _matmul(xb, w_main, BF16, 1024, _feature_tile(w_main.shape[1]), name)


def _layer_nsa(xb, batch, seq, rel_bias, w_in, cmp_pe_k, cmp_w1_k, cmp_w2_k, cmp_pe_v, cmp_w1_v, cmp_w2_v):
    kc0, ks0, vs0, kw0, vw0, gate0 = (WIDTH + j * KV_WIDTH for j in (0, 2, 3, 4, 5, 6))
    gate1 = gate0 + 3 * N_HEADS
    h = _project(xb, _w_cols(w_in, (kc0, vs0), (kw0, vw0), (gate1, w_in.shape[1])), "proj_nsa")
    qvt = _matmul_t(xb, _w_q_and(w_in, (vs0, kw0), (vw0, gate0)), "proj_nsa_t")
    per_group = 3 * HPG
    w_gate = w_in[:, gate0:gate1].reshape(-1, N_KV, per_group)
    w_gate = jnp.pad(w_gate, ((0, 0), (0, 0), (0, LANES - per_group))).reshape(-1, N_KV * LANES)
    gl = _matmul(xb, w_gate.astype(BF16), F32, 512, N_KV * LANES, "proj_nsa_gates")
    blk = lambda j: j * N_KV
    vrow = WIDTH // HEAD_DIM
    k_cmp, v_cmp = _nsa_compress(h, batch, seq, blk(0), blk(1), cmp_pe_k, cmp_w1_k, cmp_w2_k,
                                 cmp_pe_v, cmp_w1_v, cmp_w2_v)
    o_cmp, sel = _nsa_cmp_select(qvt, batch, seq, k_cmp, v_cmp, rel_bias)
    o_sel = _nsa_sel_attn(h, qvt, batch, seq, blk(2), vrow, sel, rel_bias)
    nd = NSA_WINDOW // LANES + 1
    return _banded_gqa(h, qvt, batch, seq, blk(3), vrow + N_KV, 4 * KV_WIDTH // GROUP_W,
                       _band_tables(rel_bias, nd, NSA_WINDOW, True), mix=(o_cmp, o_sel, gl), name="nsa_window_mix")


def _layer_swa(xb, batch, seq, rel_bias, w_in, sinks):
    v0, z0 = WIDTH + KV_WIDTH, WIDTH + 2 * KV_WIDTH
    h = _project(xb, _w_cols(w_in, (WIDTH, v0), (z0, w_in.shape[1])), "proj_swa")
    qvt = _matmul_t(xb, _w_q_and(w_in, (v0, z0)), "proj_swa_t")
    nd = SWA_WINDOW // LANES + 1
    return _banded_gqa(h, qvt, batch, seq, 0, WIDTH // HEAD_DIM, KV_WIDTH // GROUP_W,
                       _band_tables(rel_bias, nd, SWA_WINDOW, True), sinks=sinks, name="swa")


def _layer_sb(xb, batch, seq, w_in, t=256):
    w = jnp.concatenate([w_in[:, :WIDTH] * QSCALE, w_in[:, WIDTH:]], axis=1).astype(BF16)
    return _sb_call(_project(xb, w, "proj_sb"), batch, seq, t)


def _layer_fox(xb, batch, seq, w_in, fgate_bias, t=256):
    f0 = 3 * WIDTH
    f1 = f0 + N_HEADS
    h = _project(xb, _w_cols(w_in, (WIDTH, 2 * WIDTH), (f1, w_in.shape[1])), "proj_fox")
    qvt = _matmul_t(xb, _w_q_and(w_in, (2 * WIDTH, f0)), "proj_fox_t")
    fl = _matmul(xb, _pad_cols(w_in[:, f0:f1], LANES), F32, 512, LANES, "proj_fox_gates")
    ct, ck = _fox_prep(fl, fgate_bias, batch, seq)
    nq = seq // t
    extra_specs = [pl.BlockSpec((None, HEADS_PER_STEP, nq, 1, t), lambda b, hp, i: (b, hp, 0, 0, 0)),
                   pl.BlockSpec((seq, LANES), lambda b, hp, i: (b, 0))]
    return _full_attn_call(_fox_kernel, h, qvt, batch, seq, t, extra_specs,
                           [ct.reshape(batch, LANES, nq, 1, t), ck], "forgetting")


def kernel(x, rel_bias, w_in_a, w_out_a, ln_g_a, ln_b_a, cmp_pe_k, cmp_w1_k, cmp_w2_k, cmp_pe_v, cmp_w1_v, cmp_w2_v, w_in_b, w_out_b, ln_g_b, ln_b_b, sinks_b, w_in_c, w_out_c, ln_g_c, ln_b_c, w_in_d, w_out_d, ln_g_d, ln_b_d, fgate_bias_d):
    batch, seq, d_model = x.shape
    xf = x.reshape(batch * seq, d_model)
    xb = xf.astype(BF16)
    og = _layer_nsa(xb, batch, seq, rel_bias, w_in_a, cmp_pe_k, cmp_w1_k, cmp_w2_k, cmp_pe_v, cmp_w1_v, cmp_w2_v)
    xf, xb = _out_proj_ln(og, w_out_a, xf, ln_g_a, ln_b_a)
    og = _layer_swa(xb, batch, seq, rel_bias, w_in_b, sinks_b)
    xf, xb = _out_proj_ln(og, w_out_b, xf, ln_g_b, ln_b_b)
    og = _layer_sb(xb, batch, seq, w_in_c)
    xf, xb = _out_proj_ln(og, w_out_c, xf, ln_g_c, ln_b_c)
    og = _layer_fox(xb, batch, seq, w_in_d, fgate_bias_d)
    xf, xb = _out_proj_ln(og, w_out_d, xf, ln_g_d, ln_b_d)
    return xf.reshape(batch, seq, d_model)
```

```python
import functools
import math

import numpy as np
import jax
import jax.numpy as jnp
from jax import lax
from jax.experimental import pallas as pl
from jax.experimental.pallas import tpu as pltpu

F32 = jnp.float32
BF16 = jnp.bfloat16

N_HEADS = 16
HEAD_DIM = 128
N_KV = 4
HPG = N_HEADS // N_KV
WIDTH = N_HEADS * HEAD_DIM
KV_WIDTH = N_KV * HEAD_DIM
GROUP_W = HPG * HEAD_DIM
REL_BUCKETS = 32
REL_MAX_DIST = 128
CMP_LEN = 32
CMP_STRIDE = 16
SEL_LEN = 64
SEL_TOPK = 8
NSA_WINDOW = 512
SWA_WINDOW = 128
DEPTH = 4
DN_ALPHA = (2 * DEPTH) ** 0.25
LN_EPS = 1e-5
NEG_INF = -1e30
FORCED_SCORE = 1e4
SCALE = HEAD_DIM ** -0.5
LOG2E = math.log2(math.e)
QSCALE = SCALE * LOG2E

LANES = 128
BF16_ROWS = 16
F32_ROWS = 8
TILES_PER_STEP = 4
M_INIT = -1e29
VMEM_LIMIT = 56 * 1024 * 1024


def _cparams(*sem):
    return pltpu.CompilerParams(dimension_semantics=sem, vmem_limit_bytes=VMEM_LIMIT)


def _silu(z):
    return z / (1.0 + jnp.exp(-z))


def _dot_nt(a, b):
    return lax.dot_general(a, b, (((1,), (1,)), ((), ())), preferred_element_type=F32)


def _dot(a, b):
    return jnp.dot(a, b, preferred_element_type=F32)


def _head(j):
    return slice(j * HEAD_DIM, (j + 1) * HEAD_DIM)


def _group_queries(qt):
    return jnp.concatenate([qt[r * HEAD_DIM:(r + 1) * HEAD_DIM] for r in range(HPG)], axis=1)


def _split3(x):
    a1 = x.astype(BF16)
    r1 = x - a1.astype(F32)
    a2 = r1.astype(BF16)
    a3 = (r1 - a2.astype(F32)).astype(BF16)
    return a1, a2, a3


def _proj_kernel(x_ref, w_ref, o_ref, *, w_dim, transposed_out):
    x, w = x_ref[...], w_ref[...]
    if transposed_out:
        y = lax.dot_general(w, x, (((w_dim,), (1,)), ((), ())), preferred_element_type=F32)
    else:
        y = lax.dot_general(x, w, (((1,), (w_dim,)), ((), ())), preferred_element_type=F32)
    o_ref[...] = y.astype(o_ref.dtype)


def _feature_tile(n):
    return next(t for t in (1024, 1280, 1536, 768, 512, 256, 128) if n % t == 0)


def _projection(x, w, out_dtype, name, *, features_first=False, transposed_out=False, tm=1024):
    m, k = x.shape
    n = w.shape[0] if features_first else w.shape[1]
    tn = _feature_tile(n)
    assert m % tm == 0
    w_spec = (pl.BlockSpec((tn, k), lambda j, i: (j, 0)) if features_first
              else pl.BlockSpec((k, tn), lambda j, i: (0, j)))
    if transposed_out:
        out_spec, out_shape = pl.BlockSpec((tn, tm), lambda j, i: (j, i)), (n, m)
    else:
        out_spec, out_shape = pl.BlockSpec((tm, tn), lambda j, i: (i, j)), (m, n)
    return pl.pallas_call(
        functools.partial(_proj_kernel, w_dim=1 if features_first else 0, transposed_out=transposed_out),
        grid=(n // tn, m // tm),
        in_specs=[pl.BlockSpec((tm, k), lambda j, i: (i, 0)), w_spec],
        out_specs=out_spec,
        out_shape=jax.ShapeDtypeStruct(out_shape, out_dtype),
        compiler_params=_cparams("arbitrary", "arbitrary"),
        name=name,
    )(x, w)


def _outln_kernel(og_ref, w_ref, x_ref, g_ref, b_ref, xo_ref, xb_ref):
    y = _dot(og_ref[...], w_ref[...])
    t = DN_ALPHA * x_ref[...] + y
    mu = jnp.mean(t, axis=-1, keepdims=True)
    d = t - mu
    var = jnp.mean(d * d, axis=-1, keepdims=True)
    out = d * lax.rsqrt(var + LN_EPS) * g_ref[...] + b_ref[...]
    xo_ref[...] = out
    xb_ref[...] = out.astype(BF16)


def _out_proj_ln(og, w_out, x, ln_g, ln_b, tm=512):
    m, k = og.shape
    n = w_out.shape[1]
    return pl.pallas_call(
        _outln_kernel,
        grid=(m // tm,),
        in_specs=[pl.BlockSpec((tm, k), lambda i: (i, 0)),
                  pl.BlockSpec((k, n), lambda i: (0, 0)),
                  pl.BlockSpec((tm, n), lambda i: (i, 0)),
                  pl.BlockSpec((1, n), lambda i: (0, 0)),
                  pl.BlockSpec((1, n), lambda i: (0, 0))],
        out_specs=[pl.BlockSpec((tm, n), lambda i: (i, 0)),
                   pl.BlockSpec((tm, n), lambda i: (i, 0))],
        out_shape=[jax.ShapeDtypeStruct((m, n), F32), jax.ShapeDtypeStruct((m, n), BF16)],
        compiler_params=_cparams("arbitrary"),
        name="out_proj_ln",
    )(og, w_out.astype(BF16), x, ln_g.reshape(1, n), ln_b.reshape(1, n))


def _bucket_np(dist):
    max_exact = REL_BUCKETS // 2
    ratio = np.maximum(dist, max_exact).astype(np.float32) / max_exact
    large = max_exact + (np.log(ratio) / math.log(REL_MAX_DIST / max_exact)
                         * (REL_BUCKETS - max_exact)).astype(np.int32)
    return np.where(dist < max_exact, dist, np.minimum(large, REL_BUCKETS - 1))


def _rel_table(rel_bias, dist, ok):
    bucket = _bucket_np(np.maximum(dist, 0)).reshape(-1)
    onehot_t = np.zeros((REL_BUCKETS, bucket.size), np.float32)
    onehot_t[bucket, np.arange(bucket.size)] = 1.0
    t = jnp.dot(rel_bias.T.astype(F32) * LOG2E, jnp.asarray(onehot_t), precision=lax.Precision.HIGHEST)
    blocked = np.where(ok.reshape(-1), 0.0, NEG_INF).astype(np.float32)
    return (t + blocked[None, :]).reshape((N_HEADS,) + dist.shape)


def _band_tables(rel_bias, nd, window, with_blocked=False):
    key = np.arange(LANES)[:, None]
    qry = np.arange(LANES)[None, :]
    dist = np.stack([LANES * d + qry - key for d in range(nd)])
    ok = dist >= 0
    if window is not None:
        ok &= dist < window
    if with_blocked:
        dist = np.concatenate([dist, np.zeros((1, LANES, LANES), dist.dtype)])
        ok = np.concatenate([ok, np.zeros((1, LANES, LANES), bool)])
    return _rel_table(rel_bias, dist, ok)


def _group_table(tbl_ref, idx):
    return jnp.concatenate([tbl_ref[r, idx] for r in range(HPG)], axis=1)


def _with_ones_rows(vt):
    return jnp.concatenate([vt, jnp.ones((BF16_ROWS, vt.shape[1]), BF16)], axis=0)


def _finish_group(acc):
    o_t = acc[:HEAD_DIM] / jnp.maximum(acc[HEAD_DIM:HEAD_DIM + 1], 1e-30)
    return jnp.concatenate([o_t[:, r * LANES:(r + 1) * LANES].T for r in range(HPG)], axis=1)


def _online_update(x, vt, m, acc):
    m_new = jnp.maximum(m, jnp.max(x, axis=0, keepdims=True))
    p = jnp.exp2(x - m_new)
    acc = jnp.exp2(m - m_new) * acc + _dot(_with_ones_rows(vt), p.astype(BF16))
    return m_new, acc


def _banded_kernel(*refs, nd, has_sink, mixed):
    refs = list(refs)
    qt_ref, k_ref, vt_ref, tbl_ref, z_ref = refs[:5]
    rest = refs[5:]
    sink_ref = rest.pop(0) if has_sink else None
    oc_ref, os_ref, gl_ref = (rest.pop(0), rest.pop(0), rest.pop(0)) if mixed else (None, None, None)
    (o_ref,) = rest
    for u in range(TILES_PER_STEP):
        i = pl.program_id(2) * TILES_PER_STEP + u
        tile = slice(u * LANES, (u + 1) * LANES)
        ks, vts, tbls = [], [], []
        for d in range(nd):
            kb = i - d
            start = pl.multiple_of(jnp.maximum(kb, 0) * LANES, LANES)
            ks.append(k_ref[pl.ds(start, LANES), :])
            vts.append(vt_ref[:, pl.ds(start, LANES)])
            tbls.append(_group_table(tbl_ref, jnp.where(kb >= 0, d, nd)))
        x = _dot(jnp.concatenate(ks, axis=0), _group_queries(qt_ref[:, tile])) + jnp.concatenate(tbls, axis=0)
        m = jnp.max(x, axis=0, keepdims=True)
        if has_sink:
            m = jnp.maximum(m, sink_ref[...])
        p = jnp.exp2(x - m)
        acc = _dot(_with_ones_rows(jnp.concatenate(vts, axis=1)), p.astype(BF16))
        if has_sink:
            rows = lax.broadcasted_iota(jnp.int32, acc.shape, 0)
            acc = acc + jnp.where(rows >= HEAD_DIM, jnp.exp2(sink_ref[...] - m), 0.0)
        o = _finish_group(acc)
        if mixed:
            gates = 1.0 / (1.0 + jnp.exp(-gl_ref[tile, :]))
            oc = oc_ref[tile, :].astype(F32)
            osel = os_ref[tile, :].astype(F32)
            o = jnp.concatenate(
                [gates[:, 3 * r:3 * r + 1] * oc[:, _head(r)] + gates[:, 3 * r + 1:3 * r + 2] * osel[:, _head(r)]
                 + gates[:, 3 * r + 2:3 * r + 3] * o[:, _head(r)] for r in range(HPG)], axis=1)
        o_ref[tile, :] = (o * _silu(z_ref[tile, :].astype(F32))).astype(o_ref.dtype)


def _banded_gqa(h, qvt, batch, seq, kblk, vtblk, zblk, tbl, sinks=None, mix=None, name="banded"):
    step = TILES_PER_STEP * LANES
    nq = seq // step
    nd = tbl.shape[1] - 1
    rows_of_group = lambda blk: pl.BlockSpec((step, GROUP_W), lambda b, g, i: (b * nq + i, blk + g))
    in_specs = [pl.BlockSpec((GROUP_W, step), lambda b, g, i: (g, b * nq + i)),
                pl.BlockSpec((seq, HEAD_DIM), lambda b, g, i: (b, kblk + g)),
                pl.BlockSpec((HEAD_DIM, seq), lambda b, g, i: (vtblk + g, b)),
                pl.BlockSpec((HPG, nd + 1, LANES, LANES), lambda b, g, i: (g, 0, 0, 0)),
                rows_of_group(zblk)]
    args = [qvt, h, qvt, tbl, h]
    if sinks is not None:
        sink_row = jnp.repeat(sinks.astype(F32).reshape(N_KV, HPG) * LOG2E, LANES, axis=1)
        in_specs.append(pl.BlockSpec((None, 1, HPG * LANES), lambda b, g, i: (g, 0, 0)))
        args.append(sink_row.reshape(N_KV, 1, HPG * LANES))
    if mix is not None:
        in_specs += [rows_of_group(0), rows_of_group(0),
                     pl.BlockSpec((step, LANES), lambda b, g, i: (b * nq + i, g))]
        args += list(mix)
    return pl.pallas_call(
        functools.partial(_banded_kernel, nd=nd, has_sink=sinks is not None, mixed=mix is not None),
        grid=(batch, N_KV, nq),
        in_specs=in_specs,
        out_specs=pl.BlockSpec((step, GROUP_W), lambda b, g, i: (b * nq + i, g)),
        out_shape=jax.ShapeDtypeStruct((batch * seq, WIDTH), BF16),
        compiler_params=_cparams("arbitrary", "arbitrary", "arbitrary"),
        name=name,
    )(*args)


def _compress_kernel(kc_ref, vc_ref, pek_ref, w1k_ref, w2k_ref, pev_ref, w1v_ref, w2v_ref,
                     ko_ref, vo_ref, xs_ref, *, nc):
    half = CMP_LEN // 2
    for x_ref, pe_ref, w1_ref, w2_ref, o_ref in ((kc_ref, pek_ref, w1k_ref, w2k_ref, ko_ref),
                                                 (vc_ref, pev_ref, w1v_ref, w2v_ref, vo_ref)):
        xs_ref[...] = x_ref[...].astype(F32)
        first = jnp.zeros((nc, HEAD_DIM), F32)
        second = jnp.zeros((nc, HEAD_DIM), F32)
        for l in range(half):
            xl = xs_ref[pl.ds(l, nc, stride=CMP_STRIDE), :]
            first += _dot((xl + pe_ref[l:l + 1, :]).astype(BF16), w1_ref[l])
            second += _dot((xl + pe_ref[half + l:half + l + 1, :]).astype(BF16), w1_ref[half + l])
        hid = _silu(first + pltpu.roll(second, nc - 1, 0))
        o_ref[...] = _dot(hid.astype(BF16), w2_ref[...]).astype(o_ref.dtype)


def _nsa_compress(h, batch, seq, kcblk, vcblk, pe_k, w1_k, w2_k, pe_v, w1_v, w2_v):
    assert CMP_LEN == 2 * CMP_STRIDE
    nc = seq // CMP_STRIDE
    kv_spec = lambda blk: pl.BlockSpec((seq, HEAD_DIM), lambda b, g: (b, blk + g))
    full = lambda shape: pl.BlockSpec(shape, lambda b, g: (0,) * len(shape))
    out_spec = pl.BlockSpec((None, None, nc, HEAD_DIM), lambda b, g: (b, g, 0, 0))
    out_shape = jax.ShapeDtypeStruct((batch, N_KV, nc, HEAD_DIM), BF16)
    return pl.pallas_call(
        functools.partial(_compress_kernel, nc=nc),
        grid=(batch, N_KV),
        in_specs=[kv_spec(kcblk), kv_spec(vcblk),
                  full((CMP_LEN, HEAD_DIM)), full((CMP_LEN, HEAD_DIM, HEAD_DIM)), full((HEAD_DIM, HEAD_DIM)),
                  full((CMP_LEN, HEAD_DIM)), full((CMP_LEN, HEAD_DIM, HEAD_DIM)), full((HEAD_DIM, HEAD_DIM))],
        out_specs=[out_spec, out_spec],
        out_shape=[out_shape, out_shape],
        scratch_shapes=[pltpu.VMEM((seq, HEAD_DIM), F32)],
        compiler_params=_cparams("arbitrary", "arbitrary"),
        name="nsa_compress",
    )(h, h, pe_k, w1_k.astype(BF16), w2_k.astype(BF16), pe_v, w1_v.astype(BF16), w2_v.astype(BF16))


def _cmp_table(rel_bias, nc):
    u = np.arange(2 * nc)[:, None] - nc
    dist = np.arange(LANES)[None, :] - (u * CMP_STRIDE + CMP_LEN - 1)
    return _rel_table(rel_bias, dist, dist >= 0)


def _inter_t(seq, nc):
    nb = seq // SEL_LEN
    cstart = np.arange(nc) * CMP_STRIDE
    sstart = np.arange(nb) * SEL_LEN
    inter = np.clip(np.minimum(cstart[None, :] + CMP_LEN, sstart[:, None] + SEL_LEN)
                    - np.maximum(cstart[None, :], sstart[:, None]), 0, None) / CMP_LEN
    return inter.astype(np.float32)


def _cmp_select_kernel(qt_ref, kc_ref, vc_ref, tbl_ref, inter_ref, o_ref, sel_ref, *, nb, nc):
    first = pl.program_id(1) * TILES_PER_STEP
    width = TILES_PER_STEP * LANES
    kc = kc_ref[...]
    vc = vc_ref[...]
    per_tile = LANES // CMP_STRIDE
    row0 = [pl.multiple_of(nc - (first + u) * per_tile, per_tile) for u in range(TILES_PER_STEP)]
    outs = []
    p_sum_t = None
    for r in range(HPG):
        tbl = jnp.concatenate([tbl_ref[r, pl.ds(row, nc), :] for row in row0], axis=1)
        st = _dot(kc, qt_ref[r * HEAD_DIM:(r + 1) * HEAD_DIM, :]) + tbl
        pt = jnp.where(tbl > 0.5 * NEG_INF, jnp.exp2(st - jnp.max(st, axis=0, keepdims=True)), 0.0)
        pt = pt / jnp.maximum(jnp.sum(pt, axis=0, keepdims=True), 1e-30)
        p_sum_t = pt if p_sum_t is None else p_sum_t + pt
        outs.append(_dot(pt.T.astype(BF16), vc))
    o_ref[...] = jnp.concatenate(outs, axis=1).astype(o_ref.dtype)

    inter = inter_ref[...]
    imp = sum(_dot(inter, part) for part in _split3(p_sum_t))
    blk = lax.broadcasted_iota(jnp.int32, (nb, width), 0)
    pos = first * LANES + lax.broadcasted_iota(jnp.int32, (nb, width), 1)
    cur = lax.shift_right_logical(pos, int(math.log2(SEL_LEN)))
    allowed = blk * SEL_LEN <= pos
    forced = (blk == 0) | (blk == cur) | (blk == cur - 1)
    imp = jnp.where(allowed, jnp.where(forced, FORCED_SCORE, imp), NEG_INF)
    rank = jnp.zeros((nb, width), F32)
    for c in range(nb):
        row = imp[c:c + 1, :]
        earlier = jnp.where(blk > c, 1.0, 0.0)
        rank += jnp.where(row > imp, 1.0, jnp.where(row == imp, earlier, 0.0))
    sel_t = jnp.where((rank < SEL_TOPK) & (imp > 0.5 * NEG_INF), 1.0, 0.0)
    sel_t = jnp.concatenate([sel_t, jnp.zeros((LANES - nb, width), F32)], axis=0).astype(sel_ref.dtype)
    for u in range(TILES_PER_STEP):
        sel_ref[u] = sel_t[:, u * LANES:(u + 1) * LANES]


def _nsa_cmp_select(qvt, batch, seq, k_cmp, v_cmp, rel_bias):
    step = TILES_PER_STEP * LANES
    nq = seq // step
    nc = seq // CMP_STRIDE
    nb = seq // SEL_LEN
    assert nc == LANES and nb <= LANES
    tbl = _cmp_table(rel_bias, nc)
    inter = jnp.asarray(_inter_t(seq, nc), BF16)
    cmp_spec = pl.BlockSpec((None, None, nc, HEAD_DIM), lambda g, i, b: (b, g, 0, 0))
    return pl.pallas_call(
        functools.partial(_cmp_select_kernel, nb=nb, nc=nc),
        grid=(N_KV, nq, batch),
        in_specs=[pl.BlockSpec((GROUP_W, step), lambda g, i, b: (g, b * nq + i)),
                  cmp_spec, cmp_spec,
                  pl.BlockSpec((HPG, 2 * nc, LANES), lambda g, i, b: (g, 0, 0)),
                  pl.BlockSpec((nb, nc), lambda g, i, b: (0, 0))],
        out_specs=[pl.BlockSpec((step, GROUP_W), lambda g, i, b: (b * nq + i, g)),
                   pl.BlockSpec((None, None, TILES_PER_STEP, LANES, LANES), lambda g, i, b: (b, g, i, 0, 0))],
        out_shape=[jax.ShapeDtypeStruct((batch * seq, WIDTH), BF16),
                   jax.ShapeDtypeStruct((batch, N_KV, seq // LANES, LANES, LANES), BF16)],
        compiler_params=_cparams("arbitrary", "arbitrary", "arbitrary"),
        name="nsa_cmp_select",
    )(qvt, k_cmp, v_cmp, tbl, inter)


SEL_TILES = 2
SEL_KEYS = SEL_TILES * LANES


def _expand_np(seq):
    j = np.arange(LANES)[None, None, :]
    key = np.arange(seq // SEL_KEYS)[:, None, None] * SEL_KEYS + np.arange(SEL_KEYS)[None, :, None]
    return (j == key // SEL_LEN).astype(np.float32)


def _sel_attn_kernel(qt_ref, k_ref, vt_ref, sel_ref, tbl_ref, exp_ref, o_ref):
    last = pl.program_id(2)
    lanes = HPG * LANES
    beyond = tbl_ref.shape[1] - 1
    far = beyond - 1
    tiles = range(SEL_TILES)
    qt_aug = []
    for u in tiles:
        blocked = ((1.0 - sel_ref[u].astype(F32)) * NEG_INF).astype(BF16)
        qt_aug.append(jnp.concatenate([_group_queries(qt_ref[:, u * LANES:(u + 1) * LANES]),
                                       jnp.concatenate([blocked] * HPG, axis=1)], axis=0))

    def scores(u, kt):
        start = pl.multiple_of(kt * SEL_KEYS, SEL_KEYS)
        k_aug = jnp.concatenate([k_ref[pl.ds(start, SEL_KEYS), :], exp_ref[kt]], axis=1)
        return _dot(k_aug, qt_aug[u])

    def body(kt, carry):
        start = pl.multiple_of(kt * SEL_KEYS, SEL_KEYS)
        out = []
        for u in tiles:
            s, m, acc = carry[u]
            s_next = scores(u, jnp.minimum(kt + 1, last))
            behind = [(last - kt) * SEL_TILES + u - c for c in tiles]
            x = s + jnp.concatenate(
                [_group_table(tbl_ref, jnp.where(d >= 0, jnp.minimum(d, far), beyond)) for d in behind], axis=0)
            out.append((s_next,) + _online_update(x, vt_ref[:, pl.ds(start, SEL_KEYS)], m, acc))
        return tuple(out)

    init = tuple((scores(u, 0), jnp.full((1, lanes), M_INIT, F32),
                  jnp.zeros((HEAD_DIM + BF16_ROWS, lanes), F32)) for u in tiles)
    final = lax.fori_loop(0, last + 1, body, init)
    for u in tiles:
        o_ref[u * LANES:(u + 1) * LANES, :] = _finish_group(final[u][2]).astype(o_ref.dtype)


def _nsa_sel_attn(h, qvt, batch, seq, kblk, vtblk, sel, rel_bias):
    nq = seq // SEL_KEYS
    tbl = _band_tables(rel_bias, 3, None, True)
    expand = jnp.asarray(_expand_np(seq), BF16)
    return pl.pallas_call(
        _sel_attn_kernel,
        grid=(batch, N_KV, nq),
        in_specs=[pl.BlockSpec((GROUP_W, SEL_KEYS), lambda b, g, i: (g, b * nq + i)),
                  pl.BlockSpec((seq, HEAD_DIM), lambda b, g, i: (b, kblk + g)),
                  pl.BlockSpec((HEAD_DIM, seq), lambda b, g, i: (vtblk + g, b)),
                  pl.BlockSpec((None, None, SEL_TILES, LANES, LANES), lambda b, g, i: (b, g, i, 0, 0)),
                  pl.BlockSpec((HPG, 4, LANES, LANES), lambda b, g, i: (g, 0, 0, 0)),
                  pl.BlockSpec((seq // SEL_KEYS, SEL_KEYS, LANES), lambda b, g, i: (0, 0, 0))],
        out_specs=pl.BlockSpec((SEL_KEYS, GROUP_W), lambda b, g, i: (b * nq + i, g)),
        out_shape=jax.ShapeDtypeStruct((batch * seq, WIDTH), BF16),
        compiler_params=_cparams("arbitrary", "arbitrary", "arbitrary"),
        name="nsa_sel_attn",
    )(qvt, h, qvt, sel, tbl, expand)


HEADS_PER_STEP = 4
STEP_W = HEADS_PER_STEP * HEAD_DIM


def _pipelined_heads(i, scores, consume, init):
    heads = range(HEADS_PER_STEP)
    ahead = [scores(j, jnp.maximum(i - 1, 0)) for j in heads]
    state = [consume(j, scores(j, i), i, *init, True) for j in heads]

    def body(step, carry):
        kb = i - step
        return tuple((scores(j, jnp.maximum(kb - 1, 0)),) + consume(j, carry[j][0], kb, *carry[j][1:], False)
                     for j in heads)

    carry = lax.fori_loop(1, i + 1, body, tuple((ahead[j],) + state[j] for j in heads))
    return [c[1:] for c in carry]


def _sb_kernel(q_ref, k_ref, v_ref, z_ref, o_ref, *, t):
    i = pl.program_id(2)
    qry = lax.broadcasted_iota(jnp.int32, (t, t), 0)
    key = lax.broadcasted_iota(jnp.int32, (t, t), 1)
    strict = key < qry
    later = jnp.where(qry >= key, 1.0, 0.0).astype(BF16)

    def scores(j, kb):
        start = pl.multiple_of(kb * t, t)
        return _dot_nt(q_ref[:, _head(j)], k_ref[pl.ds(start, t), _head(j)])

    def consume(j, y, kb, run, acc, diag):
        start = pl.multiple_of(kb * t, t)
        neg = -y
        soft = jnp.log(1.0 + jnp.exp2(jnp.minimum(y, neg))) * LOG2E
        log_keep = jnp.minimum(neg, 0.0) - soft
        if diag:
            log_keep = jnp.where(strict, log_keep, 0.0)
        keep = _dot(log_keep.astype(BF16), later)
        a = jnp.exp2(y + keep + run)
        if diag:
            a = jnp.where(strict, a, 0.0)
        acc = acc + _dot(a.astype(BF16), v_ref[pl.ds(start, t), _head(j)])
        return run + keep[:, 0:1], acc

    final = _pipelined_heads(i, scores, consume, (jnp.zeros((t, 1), F32), jnp.zeros((t, HEAD_DIM), F32)))
    o = jnp.concatenate([acc for _, acc in final], axis=1)
    o_ref[...] = (o * _silu(z_ref[...].astype(F32))).astype(o_ref.dtype)


def _sb_call(h, batch, seq, t):
    nq = seq // t
    per_w = WIDTH // STEP_W
    rows = lambda part: pl.BlockSpec((t, STEP_W), lambda b, hp, i: (b * nq + i, part * per_w + hp))
    whole = lambda part: pl.BlockSpec((seq, STEP_W), lambda b, hp, i: (b, part * per_w + hp))
    return pl.pallas_call(
        functools.partial(_sb_kernel, t=t),
        grid=(batch, per_w, nq),
        in_specs=[rows(0), whole(1), whole(2), rows(3)],
        out_specs=rows(0),
        out_shape=jax.ShapeDtypeStruct((batch * seq, WIDTH), BF16),
        compiler_params=_cparams("arbitrary", "arbitrary", "arbitrary"),
        name="stick_breaking",
    )(h, h, h, h)


def _full_attn_call(kernel, h, qvt, batch, seq, t, extra_specs, extra_args, name):
    nq = seq // t
    per_w = WIDTH // STEP_W
    in_specs = [pl.BlockSpec((STEP_W, t), lambda b, hp, i: (hp, b * nq + i)),
                pl.BlockSpec((seq, STEP_W), lambda b, hp, i: (b, hp)),
                pl.BlockSpec((STEP_W, seq), lambda b, hp, i: (per_w + hp, b)),
                pl.BlockSpec((t, STEP_W), lambda b, hp, i: (b * nq + i, per_w + hp))]
    return pl.pallas_call(
        functools.partial(kernel, t=t),
        grid=(batch, per_w, nq),
        in_specs=in_specs + extra_specs,
        out_specs=pl.BlockSpec((t, STEP_W), lambda b, hp, i: (b * nq + i, hp)),
        out_shape=jax.ShapeDtypeStruct((batch * seq, WIDTH), BF16),
        compiler_params=_cparams("arbitrary", "arbitrary", "arbitrary"),
        name=name,
    )(qvt, h, qvt, h, *extra_args)


KEY_PARTS = 3


def _fox_prep_kernel(fl_ref, bias_ref, ct_ref, ck_ref, *, seq):
    row = lax.broadcasted_iota(jnp.int32, (LANES, LANES), 0)
    col = lax.broadcasted_iota(jnp.int32, (LANES, LANES), 1)
    upto = jnp.where(row <= col, 1.0, 0.0).astype(BF16)
    downto = jnp.where(col <= row, 1.0, 0.0).astype(BF16)
    place = [jnp.where((col == row + j * N_HEADS) & (row < N_HEADS), 1.0, 0.0).astype(BF16)
             for j in range(KEY_PARTS)]
    carry_t = jnp.zeros((LANES, 1), F32)
    carry = jnp.zeros((1, LANES), F32)
    for blk in range(seq // LANES):
        x = fl_ref[blk * LANES:(blk + 1) * LANES, :] + bias_ref[...]
        log_f = jnp.minimum(x, 0.0) - jnp.log1p(jnp.exp(-jnp.abs(x)))
        cs_t = sum(_dot(part, upto) for part in _split3(log_f.T)) + carry_t
        ct_ref[:, blk * LANES:(blk + 1) * LANES] = cs_t * LOG2E
        carry_t = cs_t[:, LANES - 1:LANES]
        cs = sum(_dot(downto, part) for part in _split3(log_f)) + carry
        carry = cs[LANES - 1:LANES, :]
        parts = _split3(-(cs * LOG2E))
        ck_ref[blk * LANES:(blk + 1) * LANES, :] = sum(
            _dot(part, sel) for part, sel in zip(parts, place)).astype(ck_ref.dtype)


def _fox_prep(fl, fgate_bias, batch, seq):
    assert KEY_PARTS * N_HEADS <= LANES
    bias = jnp.zeros((1, LANES), F32).at[0, :N_HEADS].set(fgate_bias.astype(F32))
    return pl.pallas_call(
        functools.partial(_fox_prep_kernel, seq=seq),
        grid=(batch,),
        in_specs=[pl.BlockSpec((seq, LANES), lambda b: (b, 0)),
                  pl.BlockSpec((1, LANES), lambda b: (0, 0))],
        out_specs=[pl.BlockSpec((None, LANES, seq), lambda b: (b, 0, 0)),
                   pl.BlockSpec((seq, LANES), lambda b: (b, 0))],
        out_shape=[jax.ShapeDtypeStruct((batch, LANES, seq), F32),
                   jax.ShapeDtypeStruct((batch * seq, LANES), BF16)],
        compiler_params=_cparams("arbitrary"),
        name="fox_prep",
    )(fl, bias)


def _fox_kernel(qt_ref, k_ref, vt_ref, z_ref, crow_ref, ck_ref, o_ref, *, t):
    first_head = pl.program_id(1) * HEADS_PER_STEP
    i = pl.program_id(2)
    key = lax.broadcasted_iota(jnp.int32, (t, t), 0)
    qry = lax.broadcasted_iota(jnp.int32, (t, t), 1)
    causal = key <= qry
    row = lax.broadcasted_iota(jnp.int32, (LANES, t), 0)
    qt_aug = [jnp.concatenate(
        [qt_ref[_head(j), :],
         jnp.where((row % N_HEADS == first_head + j) & (row < KEY_PARTS * N_HEADS), 1.0, 0.0).astype(BF16)], axis=0)
        for j in range(HEADS_PER_STEP)]

    def scores(j, kb):
        start = pl.multiple_of(kb * t, t)
        k_aug = jnp.concatenate([k_ref[pl.ds(start, t), _head(j)], ck_ref[pl.ds(start, t), :]], axis=1)
        return _dot(k_aug, qt_aug[j])

    def consume(j, x, kb, m, acc, diag):
        start = pl.multiple_of(kb * t, t)
        c_q = crow_ref[pl.ds((first_head + j) % F32_ROWS, 1), pl.ds(pl.multiple_of(i * t, t), t)]
        if diag:
            x = jnp.where(causal, x, NEG_INF)
        m_new = jnp.maximum(m, jnp.max(x, axis=0, keepdims=True) + c_q)
        p = jnp.exp2(x + (c_q - m_new))
        vt = _with_ones_rows(vt_ref[_head(j), pl.ds(start, t)])
        return m_new, jnp.exp2(m - m_new) * acc + _dot(vt, p.astype(BF16))

    final = _pipelined_heads(i, scores, consume,
                             (jnp.full((1, t), M_INIT, F32), jnp.zeros((HEAD_DIM + BF16_ROWS, t), F32)))
    o = jnp.concatenate([(acc[:HEAD_DIM] / jnp.maximum(acc[HEAD_DIM:HEAD_DIM + 1], 1e-30)).T for _, acc in final],
                        axis=1)
    o_ref[...] = (o * _silu(z_ref[...].astype(F32))).astype(o_ref.dtype)


class _Weight:
    def __init__(self, w):
        self.features_first = w.shape[1] % LANES != 0
        self.src = w.T if self.features_first else w

    def pick(self, *ranges, scale_first=None, group_pad=None):
        axis = 0 if self.features_first else 1
        parts = [lax.slice_in_dim(self.src, a, b, axis=axis) for a, b in ranges]
        if scale_first is not None:
            parts[0] = parts[0] * scale_first
        out = jnp.concatenate(parts, axis=axis)
        if group_pad is not None:
            groups, width = group_pad
            out = jnp.moveaxis(out, axis, 0)
            out = out.reshape(groups, -1, out.shape[1])
            out = jnp.pad(out, ((0, 0), (0, width - out.shape[1]), (0, 0))).reshape(groups * width, -1)
            out = jnp.moveaxis(out, 0, axis)
        return out.astype(BF16)

    def project(self, xb, operand, name, out_dtype=BF16, transposed_out=False):
        return _projection(xb, operand, out_dtype, name, features_first=self.features_first,
                           transposed_out=transposed_out)


def _layer_nsa(xb, batch, seq, rel_bias, w_in, cmp_pe_k, cmp_w1_k, cmp_w2_k, cmp_pe_v, cmp_w1_v, cmp_w2_v):
    kc0, ks0, vs0, kw0, vw0, gate0 = (WIDTH + j * KV_WIDTH for j in (0, 2, 3, 4, 5, 6))
    gate1 = gate0 + 3 * N_HEADS
    w = _Weight(w_in)
    h = w.project(xb, w.pick((kc0, vs0), (kw0, vw0), (gate1, w_in.shape[1])), "proj_nsa")
    qvt = w.project(xb, w.pick((0, WIDTH), (vs0, kw0), (vw0, gate0), scale_first=QSCALE), "proj_nsa_t",
                    transposed_out=True)
    gl = w.project(xb, w.pick((gate0, gate1), group_pad=(N_KV, LANES)), "proj_nsa_gates", out_dtype=F32)
    blk = lambda j: j * N_KV
    vrow = WIDTH // HEAD_DIM
    k_cmp, v_cmp = _nsa_compress(h, batch, seq, blk(0), blk(1), cmp_pe_k, cmp_w1_k, cmp_w2_k,
                                 cmp_pe_v, cmp_w1_v, cmp_w2_v)
    o_cmp, sel = _nsa_cmp_select(qvt, batch, seq, k_cmp, v_cmp, rel_bias)
    o_sel = _nsa_sel_attn(h, qvt, batch, seq, blk(2), vrow, sel, rel_bias)
    nd = NSA_WINDOW // LANES + 1
    return _banded_gqa(h, qvt, batch, seq, blk(3), vrow + N_KV, 4 * KV_WIDTH // GROUP_W,
                       _band_tables(rel_bias, nd, NSA_WINDOW, True), mix=(o_cmp, o_sel, gl), name="nsa_window_mix")


def _layer_swa(xb, batch, seq, rel_bias, w_in, sinks):
    v0, z0 = WIDTH + KV_WIDTH, WIDTH + 2 * KV_WIDTH
    w = _Weight(w_in)
    h = w.project(xb, w.pick((WIDTH, v0), (z0, w_in.shape[1])), "proj_swa")
    qvt = w.project(xb, w.pick((0, WIDTH), (v0, z0), scale_first=QSCALE), "proj_swa_t",
                    transposed_out=True)
    nd = SWA_WINDOW // LANES + 1
    return _banded_gqa(h, qvt, batch, seq, 0, WIDTH // HEAD_DIM, KV_WIDTH // GROUP_W,
                       _band_tables(rel_bias, nd, SWA_WINDOW, True), sinks=sinks, name="swa")


def _layer_sb(xb, batch, seq, w_in, t=256):
    w = _Weight(w_in)
    h = w.project(xb, w.pick((0, WIDTH), (WIDTH, w_in.shape[1]), scale_first=QSCALE), "proj_sb")
    return _sb_call(h, batch, seq, t)


def _layer_fox(xb, batch, seq, w_in, fgate_bias, t=256):
    f0 = 3 * WIDTH
    f1 = f0 + N_HEADS
    w = _Weight(w_in)
    h = w.project(xb, w.pick((WIDTH, 2 * WIDTH), (f1, w_in.shape[1])), "proj_fox")
    qvt = w.project(xb, w.pick((0, WIDTH), (2 * WIDTH, f0), scale_first=QSCALE), "proj_fox_t",
                    transposed_out=True)
    fl = w.project(xb, w.pick((f0, f1), group_pad=(1, LANES)), "proj_fox_gates", out_dtype=F32)
    ct, ck = _fox_prep(fl, fgate_bias, batch, seq)
    assert F32_ROWS % HEADS_PER_STEP == 0
    extra_specs = [pl.BlockSpec((None, F32_ROWS, seq), lambda b, hp, i: (b, hp * HEADS_PER_STEP // F32_ROWS, 0)),
                   pl.BlockSpec((seq, LANES), lambda b, hp, i: (b, 0))]
    return _full_attn_call(_fox_kernel, h, qvt, batch, seq, t, extra_specs, [ct, ck], "forgetting")


def kernel(x, rel_bias, w_in_a, w_out_a, ln_g_a, ln_b_a, cmp_pe_k, cmp_w1_k, cmp_w2_k, cmp_pe_v, cmp_w1_v, cmp_w2_v, w_in_b, w_out_b, ln_g_b, ln_b_b, sinks_b, w_in_c, w_out_c, ln_g_c, ln_b_c, w_in_d, w_out_d, ln_g_d, ln_b_d, fgate_bias_d):
    batch, seq, d_model = x.shape
    xf = x.reshape(batch * seq, d_model)
    xb = xf.astype(BF16)
    og = _layer_nsa(xb, batch, seq, rel_bias, w_in_a, cmp_pe_k, cmp_w1_k, cmp_w2_k, cmp_pe_v, cmp_w1_v, cmp_w2_v)
    xf, xb = _out_proj_ln(og, w_out_a, xf, ln_g_a, ln_b_a)
    og = _layer_swa(xb, batch, seq, rel_bias, w_in_b, sinks_b)
    xf, xb = _out_proj_ln(og, w_out_b, xf, ln_g_b, ln_b_b)
    og = _layer_sb(xb, batch, seq, w_in_c)
    xf, xb = _out_proj_ln(og, w_out_c, xf, ln_g_c, ln_b_c)
    og = _layer_fox(xb, batch, seq, w_in_d, fgate_bias_d)
    xf, xb = _out_proj_ln(og, w_out_d, xf, ln_g_d, ln_b_d)
    return xf.reshape(batch, seq, d_model)
```

```python
import functools
import math

import numpy as np
import jax
import jax.numpy as jnp
from jax import lax
from jax.experimental import pallas as pl
from jax.experimental.pallas import tpu as pltpu

F32 = jnp.float32
BF16 = jnp.bfloat16

N_HEADS = 16
HEAD_DIM = 128
N_KV = 4
HPG = N_HEADS // N_KV
WIDTH = N_HEADS * HEAD_DIM
KV_WIDTH = N_KV * HEAD_DIM
GROUP_W = HPG * HEAD_DIM
REL_BUCKETS = 32
REL_MAX_DIST = 128
CMP_LEN = 32
CMP_STRIDE = 16
SEL_LEN = 64
SEL_TOPK = 8
NSA_WINDOW = 512
SWA_WINDOW = 128
DEPTH = 4
DN_ALPHA = (2 * DEPTH) ** 0.25
LN_EPS = 1e-5
NEG_INF = -1e30
FORCED_SCORE = 1e4
SCALE = HEAD_DIM ** -0.5
LOG2E = math.log2(math.e)
QSCALE = SCALE * LOG2E

LANES = 128
BF16_ROWS = 16
F32_ROWS = 8
LN_ROWS = 128
TILES_PER_STEP = 8
M_INIT = -1e29
VMEM_LIMIT = 56 * 1024 * 1024


def _cparams(*sem):
    return pltpu.CompilerParams(dimension_semantics=sem, vmem_limit_bytes=VMEM_LIMIT)


def _silu(z):
    return z / (1.0 + jnp.exp(-z))


def _dot_nt(a, b):
    return lax.dot_general(a, b, (((1,), (1,)), ((), ())), preferred_element_type=F32)


def _dot(a, b):
    return jnp.dot(a, b, preferred_element_type=F32)


def _head(j):
    return slice(j * HEAD_DIM, (j + 1) * HEAD_DIM)


def _group_queries(qt):
    return jnp.concatenate([qt[r * HEAD_DIM:(r + 1) * HEAD_DIM] for r in range(HPG)], axis=1)


def _split3(x):
    a1 = x.astype(BF16)
    r1 = x - a1.astype(F32)
    a2 = r1.astype(BF16)
    a3 = (r1 - a2.astype(F32)).astype(BF16)
    return a1, a2, a3


def _proj_kernel(x_ref, w_ref, o_ref, *, w_dim, transposed_out):
    x, w = x_ref[...], w_ref[...]
    if transposed_out:
        y = lax.dot_general(w, x, (((w_dim,), (1,)), ((), ())), preferred_element_type=F32)
    else:
        y = lax.dot_general(x, w, (((1,), (w_dim,)), ((), ())), preferred_element_type=F32)
    o_ref[...] = y.astype(o_ref.dtype)


def _feature_tile(n):
    return next(t for t in (1024, 1280, 1536, 768, 512, 256, 128) if n % t == 0)


def _projection(x, w, out_dtype, name, *, features_first=False, transposed_out=False, tm=1024):
    m, k = x.shape
    n = w.shape[0] if features_first else w.shape[1]
    tn = _feature_tile(n)
    assert m % tm == 0
    w_spec = (pl.BlockSpec((tn, k), lambda j, i: (j, 0)) if features_first
              else pl.BlockSpec((k, tn), lambda j, i: (0, j)))
    if transposed_out:
        out_spec, out_shape = pl.BlockSpec((tn, tm), lambda j, i: (j, i)), (n, m)
    else:
        out_spec, out_shape = pl.BlockSpec((tm, tn), lambda j, i: (i, j)), (m, n)
    return pl.pallas_call(
        functools.partial(_proj_kernel, w_dim=1 if features_first else 0, transposed_out=transposed_out),
        grid=(n // tn, m // tm),
        in_specs=[pl.BlockSpec((tm, k), lambda j, i: (i, 0)), w_spec],
        out_specs=out_spec,
        out_shape=jax.ShapeDtypeStruct(out_shape, out_dtype),
        compiler_params=_cparams("arbitrary", "arbitrary"),
        name=name,
    )(x, w)


def _outln_kernel(og_ref, w_ref, x_ref, g_ref, b_ref, xo_ref, xb_ref):
    for c in range(og_ref.shape[0] // LN_ROWS):
        rows = slice(c * LN_ROWS, (c + 1) * LN_ROWS)
        t = DN_ALPHA * x_ref[rows, :] + _dot(og_ref[rows, :], w_ref[...])
        mu = jnp.mean(t, axis=-1, keepdims=True)
        d = t - mu
        var = jnp.mean(d * d, axis=-1, keepdims=True)
        out = d * lax.rsqrt(var + LN_EPS) * g_ref[...] + b_ref[...]
        xo_ref[rows, :] = out
        xb_ref[rows, :] = out.astype(BF16)


def _out_proj_ln(og, w_out, x, ln_g, ln_b, tm=512):
    m, k = og.shape
    n = w_out.shape[1]
    return pl.pallas_call(
        _outln_kernel,
        grid=(m // tm,),
        in_specs=[pl.BlockSpec((tm, k), lambda i: (i, 0)),
                  pl.BlockSpec((k, n), lambda i: (0, 0)),
                  pl.BlockSpec((tm, n), lambda i: (i, 0)),
                  pl.BlockSpec((1, n), lambda i: (0, 0)),
                  pl.BlockSpec((1, n), lambda i: (0, 0))],
        out_specs=[pl.BlockSpec((tm, n), lambda i: (i, 0)),
                   pl.BlockSpec((tm, n), lambda i: (i, 0))],
        out_shape=[jax.ShapeDtypeStruct((m, n), F32), jax.ShapeDtypeStruct((m, n), BF16)],
        compiler_params=_cparams("arbitrary"),
        name="out_proj_ln",
    )(og, w_out.astype(BF16), x, ln_g.reshape(1, n), ln_b.reshape(1, n))


def _bucket_np(dist):
    max_exact = REL_BUCKETS // 2
    ratio = np.maximum(dist, max_exact).astype(np.float32) / max_exact
    large = max_exact + (np.log(ratio) / math.log(REL_MAX_DIST / max_exact)
                         * (REL_BUCKETS - max_exact)).astype(np.int32)
    return np.where(dist < max_exact, dist, np.minimum(large, REL_BUCKETS - 1))


def _rel_table(rel_bias, dist, ok):
    bucket = _bucket_np(np.maximum(dist, 0)).reshape(-1)
    onehot_t = np.zeros((REL_BUCKETS, bucket.size), np.float32)
    onehot_t[bucket, np.arange(bucket.size)] = 1.0
    t = jnp.dot(rel_bias.T.astype(F32) * LOG2E, jnp.asarray(onehot_t), precision=lax.Precision.HIGHEST)
    blocked = np.where(ok.reshape(-1), 0.0, NEG_INF).astype(np.float32)
    return (t + blocked[None, :]).reshape((N_HEADS,) + dist.shape)


def _band_tables(rel_bias, nd, window, with_blocked=False):
    key = np.arange(LANES)[:, None]
    qry = np.arange(LANES)[None, :]
    dist = np.stack([LANES * d + qry - key for d in range(nd)])
    ok = dist >= 0
    if window is not None:
        ok &= dist < window
    if with_blocked:
        dist = np.concatenate([dist, np.zeros((1, LANES, LANES), dist.dtype)])
        ok = np.concatenate([ok, np.zeros((1, LANES, LANES), bool)])
    return _rel_table(rel_bias, dist, ok)


def _group_table(tbl_ref, idx):
    return jnp.concatenate([tbl_ref[r, idx] for r in range(HPG)], axis=1)


def _with_ones_rows(vt):
    return jnp.concatenate([vt, jnp.ones((BF16_ROWS, vt.shape[1]), BF16)], axis=0)


def _finish_group(acc):
    o_t = acc[:HEAD_DIM] / jnp.maximum(acc[HEAD_DIM:HEAD_DIM + 1], 1e-30)
    return jnp.concatenate([o_t[:, r * LANES:(r + 1) * LANES].T for r in range(HPG)], axis=1)


def _online_update(x, vt, m, acc):
    m_new = jnp.maximum(m, jnp.max(x, axis=0, keepdims=True))
    p = jnp.exp2(x - m_new)
    acc = jnp.exp2(m - m_new) * acc + _dot(_with_ones_rows(vt), p.astype(BF16))
    return m_new, acc


def _banded_kernel(*refs, nd, has_sink, mixed):
    refs = list(refs)
    qt_ref, k_ref, vt_ref, tbl_ref, z_ref = refs[:5]
    rest = refs[5:]
    sink_ref = rest.pop(0) if has_sink else None
    oc_ref, os_ref, gl_ref = (rest.pop(0), rest.pop(0), rest.pop(0)) if mixed else (None, None, None)
    (o_ref,) = rest
    for u in range(TILES_PER_STEP):
        i = pl.program_id(2) * TILES_PER_STEP + u
        tile = slice(u * LANES, (u + 1) * LANES)
        ks, vts, tbls = [], [], []
        for d in range(nd):
            kb = i - d
            start = pl.multiple_of(jnp.maximum(kb, 0) * LANES, LANES)
            ks.append(k_ref[pl.ds(start, LANES), :])
            vts.append(vt_ref[:, pl.ds(start, LANES)])
            tbls.append(_group_table(tbl_ref, jnp.where(kb >= 0, d, nd)))
        x = _dot(jnp.concatenate(ks, axis=0), _group_queries(qt_ref[:, tile])) + jnp.concatenate(tbls, axis=0)
        m = jnp.max(x, axis=0, keepdims=True)
        if has_sink:
            m = jnp.maximum(m, sink_ref[...])
        p = jnp.exp2(x - m)
        acc = _dot(_with_ones_rows(jnp.concatenate(vts, axis=1)), p.astype(BF16))
        if has_sink:
            rows = lax.broadcasted_iota(jnp.int32, acc.shape, 0)
            acc = acc + jnp.where(rows >= HEAD_DIM, jnp.exp2(sink_ref[...] - m), 0.0)
        o = _finish_group(acc)
        if mixed:
            gates = 1.0 / (1.0 + jnp.exp(-gl_ref[tile, :]))
            oc = oc_ref[tile, :].astype(F32)
            osel = os_ref[tile, :].astype(F32)
            o = jnp.concatenate(
                [gates[:, 3 * r:3 * r + 1] * oc[:, _head(r)] + gates[:, 3 * r + 1:3 * r + 2] * osel[:, _head(r)]
                 + gates[:, 3 * r + 2:3 * r + 3] * o[:, _head(r)] for r in range(HPG)], axis=1)
        o_ref[tile, :] = (o * _silu(z_ref[tile, :].astype(F32))).astype(o_ref.dtype)


def _banded_gqa(h, qvt, batch, seq, kblk, vtblk, zblk, tbl, sinks=None, mix=None, name="banded"):
    step = TILES_PER_STEP * LANES
    nq = seq // step
    nd = tbl.shape[1] - 1
    rows_of_group = lambda blk: pl.BlockSpec((step, GROUP_W), lambda b, g, i: (b * nq + i, blk + g))
    in_specs = [pl.BlockSpec((GROUP_W, step), lambda b, g, i: (g, b * nq + i)),
                pl.BlockSpec((seq, HEAD_DIM), lambda b, g, i: (b, kblk + g)),
                pl.BlockSpec((HEAD_DIM, seq), lambda b, g, i: (vtblk + g, b)),
                pl.BlockSpec((HPG, nd + 1, LANES, LANES), lambda b, g, i: (g, 0, 0, 0)),
                rows_of_group(zblk)]
    args = [qvt, h, qvt, tbl, h]
    if sinks is not None:
        sink_row = jnp.repeat(sinks.astype(F32).reshape(N_KV, HPG) * LOG2E, LANES, axis=1)
        in_specs.append(pl.BlockSpec((None, 1, HPG * LANES), lambda b, g, i: (g, 0, 0)))
        args.append(sink_row.reshape(N_KV, 1, HPG * LANES))
    if mix is not None:
        in_specs += [rows_of_group(0), rows_of_group(0),
                     pl.BlockSpec((step, LANES), lambda b, g, i: (b * nq + i, g))]
        args += list(mix)
    return pl.pallas_call(
        functools.partial(_banded_kernel, nd=nd, has_sink=sinks is not None, mixed=mix is not None),
        grid=(batch, N_KV, nq),
        in_specs=in_specs,
        out_specs=pl.BlockSpec((step, GROUP_W), lambda b, g, i: (b * nq + i, g)),
        out_shape=jax.ShapeDtypeStruct((batch * seq, WIDTH), BF16),
        compiler_params=_cparams("arbitrary", "arbitrary", "arbitrary"),
        name=name,
    )(*args)


def _compress_kernel(kc_ref, vc_ref, pek_ref, w1k_ref, w2k_ref, pev_ref, w1v_ref, w2v_ref,
                     ko_ref, vo_ref, xs_ref, *, nc):
    half = CMP_LEN // 2
    for x_ref, pe_ref, w1_ref, w2_ref, o_ref in ((kc_ref, pek_ref, w1k_ref, w2k_ref, ko_ref),
                                                 (vc_ref, pev_ref, w1v_ref, w2v_ref, vo_ref)):
        xs_ref[...] = x_ref[...].astype(F32)
        first = jnp.zeros((nc, HEAD_DIM), F32)
        second = jnp.zeros((nc, HEAD_DIM), F32)
        for l in range(half):
            xl = xs_ref[pl.ds(l, nc, stride=CMP_STRIDE), :]
            first += _dot((xl + pe_ref[l:l + 1, :]).astype(BF16), w1_ref[l])
            second += _dot((xl + pe_ref[half + l:half + l + 1, :]).astype(BF16), w1_ref[half + l])
        hid = _silu(first + pltpu.roll(second, nc - 1, 0))
        o_ref[...] = _dot(hid.astype(BF16), w2_ref[...]).astype(o_ref.dtype)


def _nsa_compress(h, batch, seq, kcblk, vcblk, pe_k, w1_k, w2_k, pe_v, w1_v, w2_v):
    assert CMP_LEN == 2 * CMP_STRIDE
    nc = seq // CMP_STRIDE
    kv_spec = lambda blk: pl.BlockSpec((seq, HEAD_DIM), lambda b, g: (b, blk + g))
    full = lambda shape: pl.BlockSpec(shape, lambda b, g: (0,) * len(shape))
    out_spec = pl.BlockSpec((None, None, nc, HEAD_DIM), lambda b, g: (b, g, 0, 0))
    out_shape = jax.ShapeDtypeStruct((batch, N_KV, nc, HEAD_DIM), BF16)
    return pl.pallas_call(
        functools.partial(_compress_kernel, nc=nc),
        grid=(batch, N_KV),
        in_specs=[kv_spec(kcblk), kv_spec(vcblk),
                  full((CMP_LEN, HEAD_DIM)), full((CMP_LEN, HEAD_DIM, HEAD_DIM)), full((HEAD_DIM, HEAD_DIM)),
                  full((CMP_LEN, HEAD_DIM)), full((CMP_LEN, HEAD_DIM, HEAD_DIM)), full((HEAD_DIM, HEAD_DIM))],
        out_specs=[out_spec, out_spec],
        out_shape=[out_shape, out_shape],
        scratch_shapes=[pltpu.VMEM((seq, HEAD_DIM), F32)],
        compiler_params=_cparams("arbitrary", "arbitrary"),
        name="nsa_compress",
    )(h, h, pe_k, w1_k.astype(BF16), w2_k.astype(BF16), pe_v, w1_v.astype(BF16), w2_v.astype(BF16))


def _cmp_table(rel_bias, nc):
    u = np.arange(2 * nc)[:, None] - nc
    dist = np.arange(LANES)[None, :] - (u * CMP_STRIDE + CMP_LEN - 1)
    return _rel_table(rel_bias, dist, dist >= 0)


def _inter_t(seq, nc):
    nb = seq // SEL_LEN
    cstart = np.arange(nc) * CMP_STRIDE
    sstart = np.arange(nb) * SEL_LEN
    inter = np.clip(np.minimum(cstart[None, :] + CMP_LEN, sstart[:, None] + SEL_LEN)
                    - np.maximum(cstart[None, :], sstart[:, None]), 0, None) / CMP_LEN
    return inter.astype(np.float32)


def _cmp_select_kernel(qt_ref, kc_ref, vc_ref, tbl_ref, inter_ref, o_ref, sel_ref, *, nb, nc):
    first = pl.program_id(1) * TILES_PER_STEP
    width = TILES_PER_STEP * LANES
    kc = kc_ref[...]
    vc = vc_ref[...]
    per_tile = LANES // CMP_STRIDE
    row0 = [pl.multiple_of(nc - (first + u) * per_tile, per_tile) for u in range(TILES_PER_STEP)]
    outs = []
    p_sum_t = None
    for r in range(HPG):
        tbl = jnp.concatenate([tbl_ref[r, pl.ds(row, nc), :] for row in row0], axis=1)
        st = _dot(kc, qt_ref[r * HEAD_DIM:(r + 1) * HEAD_DIM, :]) + tbl
        pt = jnp.where(tbl > 0.5 * NEG_INF, jnp.exp2(st - jnp.max(st, axis=0, keepdims=True)), 0.0)
        pt = pt / jnp.maximum(jnp.sum(pt, axis=0, keepdims=True), 1e-30)
        p_sum_t = pt if p_sum_t is None else p_sum_t + pt
        outs.append(_dot(pt.T.astype(BF16), vc))
    o_ref[...] = jnp.concatenate(outs, axis=1).astype(o_ref.dtype)

    inter = inter_ref[...]
    imp = sum(_dot(inter, part) for part in _split3(p_sum_t))
    blk = lax.broadcasted_iota(jnp.int32, (nb, width), 0)
    pos = first * LANES + lax.broadcasted_iota(jnp.int32, (nb, width), 1)
    cur = lax.shift_right_logical(pos, int(math.log2(SEL_LEN)))
    allowed = blk * SEL_LEN <= pos
    forced = (blk == 0) | (blk == cur) | (blk == cur - 1)
    imp = jnp.where(allowed, jnp.where(forced, FORCED_SCORE, imp), NEG_INF)
    rank = jnp.zeros((nb, width), F32)
    for c in range(nb):
        row = imp[c:c + 1, :]
        earlier = jnp.where(blk > c, 1.0, 0.0)
        rank += jnp.where(row > imp, 1.0, jnp.where(row == imp, earlier, 0.0))
    sel_t = jnp.where((rank < SEL_TOPK) & (imp > 0.5 * NEG_INF), 1.0, 0.0)
    sel_t = jnp.concatenate([sel_t, jnp.zeros((LANES - nb, width), F32)], axis=0).astype(sel_ref.dtype)
    for u in range(TILES_PER_STEP):
        sel_ref[u] = sel_t[:, u * LANES:(u + 1) * LANES]


def _nsa_cmp_select(qvt, batch, seq, k_cmp, v_cmp, rel_bias):
    step = TILES_PER_STEP * LANES
    nq = seq // step
    nc = seq // CMP_STRIDE
    nb = seq // SEL_LEN
    assert nc == LANES and nb <= LANES
    tbl = _cmp_table(rel_bias, nc)
    inter = jnp.asarray(_inter_t(seq, nc), BF16)
    cmp_spec = pl.BlockSpec((None, None, nc, HEAD_DIM), lambda g, i, b: (b, g, 0, 0))
    return pl.pallas_call(
        functools.partial(_cmp_select_kernel, nb=nb, nc=nc),
        grid=(N_KV, nq, batch),
        in_specs=[pl.BlockSpec((GROUP_W, step), lambda g, i, b: (g, b * nq + i)),
                  cmp_spec, cmp_spec,
                  pl.BlockSpec((HPG, 2 * nc, LANES), lambda g, i, b: (g, 0, 0)),
                  pl.BlockSpec((nb, nc), lambda g, i, b: (0, 0))],
        out_specs=[pl.BlockSpec((step, GROUP_W), lambda g, i, b: (b * nq + i, g)),
                   pl.BlockSpec((None, None, TILES_PER_STEP, LANES, LANES), lambda g, i, b: (b, g, i, 0, 0))],
        out_shape=[jax.ShapeDtypeStruct((batch * seq, WIDTH), BF16),
                   jax.ShapeDtypeStruct((batch, N_KV, seq // LANES, LANES, LANES), BF16)],
        compiler_params=_cparams("arbitrary", "arbitrary", "arbitrary"),
        name="nsa_cmp_select",
    )(qvt, k_cmp, v_cmp, tbl, inter)


SEL_TILES = 2
SEL_KEYS = SEL_TILES * LANES


def _expand_np(seq):
    j = np.arange(LANES)[None, None, :]
    key = np.arange(seq // SEL_KEYS)[:, None, None] * SEL_KEYS + np.arange(SEL_KEYS)[None, :, None]
    return (j == key // SEL_LEN).astype(np.float32)


def _sel_attn_kernel(qt_ref, k_ref, vt_ref, sel_ref, tbl_ref, exp_ref, o_ref):
    last = pl.program_id(2)
    lanes = HPG * LANES
    beyond = tbl_ref.shape[1] - 1
    far = beyond - 1
    tiles = range(SEL_TILES)
    qt_aug = []
    for u in tiles:
        blocked = ((1.0 - sel_ref[u].astype(F32)) * NEG_INF).astype(BF16)
        qt_aug.append(jnp.concatenate([_group_queries(qt_ref[:, u * LANES:(u + 1) * LANES]),
                                       jnp.concatenate([blocked] * HPG, axis=1)], axis=0))

    def scores(u, kt):
        start = pl.multiple_of(kt * SEL_KEYS, SEL_KEYS)
        k_aug = jnp.concatenate([k_ref[pl.ds(start, SEL_KEYS), :], exp_ref[kt]], axis=1)
        return _dot(k_aug, qt_aug[u])

    def body(kt, carry):
        start = pl.multiple_of(kt * SEL_KEYS, SEL_KEYS)
        out = []
        for u in tiles:
            s, m, acc = carry[u]
            s_next = scores(u, jnp.minimum(kt + 1, last))
            behind = [(last - kt) * SEL_TILES + u - c for c in tiles]
            x = s + jnp.concatenate(
                [_group_table(tbl_ref, jnp.where(d >= 0, jnp.minimum(d, far), beyond)) for d in behind], axis=0)
            out.append((s_next,) + _online_update(x, vt_ref[:, pl.ds(start, SEL_KEYS)], m, acc))
        return tuple(out)

    init = tuple((scores(u, 0), jnp.full((1, lanes), M_INIT, F32),
                  jnp.zeros((HEAD_DIM + BF16_ROWS, lanes), F32)) for u in tiles)
    final = lax.fori_loop(0, last + 1, body, init)
    for u in tiles:
        o_ref[u * LANES:(u + 1) * LANES, :] = _finish_group(final[u][2]).astype(o_ref.dtype)


def _nsa_sel_attn(h, qvt, batch, seq, kblk, vtblk, sel, rel_bias):
    nq = seq // SEL_KEYS
    tbl = _band_tables(rel_bias, 3, None, True)
    expand = jnp.asarray(_expand_np(seq), BF16)
    return pl.pallas_call(
        _sel_attn_kernel,
        grid=(batch, N_KV, nq),
        in_specs=[pl.BlockSpec((GROUP_W, SEL_KEYS), lambda b, g, i: (g, b * nq + i)),
                  pl.BlockSpec((seq, HEAD_DIM), lambda b, g, i: (b, kblk + g)),
                  pl.BlockSpec((HEAD_DIM, seq), lambda b, g, i: (vtblk + g, b)),
                  pl.BlockSpec((None, None, SEL_TILES, LANES, LANES), lambda b, g, i: (b, g, i, 0, 0)),
                  pl.BlockSpec((HPG, 4, LANES, LANES), lambda b, g, i: (g, 0, 0, 0)),
                  pl.BlockSpec((seq // SEL_KEYS, SEL_KEYS, LANES), lambda b, g, i: (0, 0, 0))],
        out_specs=pl.BlockSpec((SEL_KEYS, GROUP_W), lambda b, g, i: (b * nq + i, g)),
        out_shape=jax.ShapeDtypeStruct((batch * seq, WIDTH), BF16),
        compiler_params=_cparams("arbitrary", "arbitrary", "arbitrary"),
        name="nsa_sel_attn",
    )(qvt, h, qvt, sel, tbl, expand)


HEADS_PER_STEP = 8
STEP_W = HEADS_PER_STEP * HEAD_DIM


def _pipelined_heads(i, scores, consume, init):
    heads = range(HEADS_PER_STEP)
    ahead = [scores(j, jnp.maximum(i - 1, 0)) for j in heads]
    state = [consume(j, scores(j, i), i, *init, True) for j in heads]

    def body(step, carry):
        kb = i - step
        return tuple((scores(j, jnp.maximum(kb - 1, 0)),) + consume(j, carry[j][0], kb, *carry[j][1:], False)
                     for j in heads)

    carry = lax.fori_loop(1, i + 1, body, tuple((ahead[j],) + state[j] for j in heads))
    return [c[1:] for c in carry]


def _sb_kernel(q_ref, k_ref, v_ref, z_ref, o_ref, *, t):
    i = pl.program_id(2)
    qry = lax.broadcasted_iota(jnp.int32, (t, t), 0)
    key = lax.broadcasted_iota(jnp.int32, (t, t), 1)
    strict = key < qry
    later = jnp.where(qry >= key, 1.0, 0.0).astype(BF16)

    def scores(j, kb):
        start = pl.multiple_of(kb * t, t)
        return _dot_nt(q_ref[:, _head(j)], k_ref[pl.ds(start, t), _head(j)])

    def consume(j, y, kb, run, acc, diag):
        start = pl.multiple_of(kb * t, t)
        neg = -y
        soft = jnp.log(1.0 + jnp.exp2(jnp.minimum(y, neg))) * LOG2E
        log_keep = jnp.minimum(neg, 0.0) - soft
        if diag:
            log_keep = jnp.where(strict, log_keep, 0.0)
        keep = _dot(log_keep.astype(BF16), later)
        a = jnp.exp2(y + keep + run)
        if diag:
            a = jnp.where(strict, a, 0.0)
        acc = acc + _dot(a.astype(BF16), v_ref[pl.ds(start, t), _head(j)])
        return run + keep[:, 0:1], acc

    final = _pipelined_heads(i, scores, consume, (jnp.zeros((t, 1), F32), jnp.zeros((t, HEAD_DIM), F32)))
    o = jnp.concatenate([acc for _, acc in final], axis=1)
    o_ref[...] = (o * _silu(z_ref[...].astype(F32))).astype(o_ref.dtype)


def _sb_call(h, batch, seq, t):
    nq = seq // t
    per_w = WIDTH // STEP_W
    rows = lambda part: pl.BlockSpec((t, STEP_W), lambda b, hp, i: (b * nq + i, part * per_w + hp))
    whole = lambda part: pl.BlockSpec((seq, STEP_W), lambda b, hp, i: (b, part * per_w + hp))
    return pl.pallas_call(
        functools.partial(_sb_kernel, t=t),
        grid=(batch, per_w, nq),
        in_specs=[rows(0), whole(1), whole(2), rows(3)],
        out_specs=rows(0),
        out_shape=jax.ShapeDtypeStruct((batch * seq, WIDTH), BF16),
        compiler_params=_cparams("arbitrary", "arbitrary", "arbitrary"),
        name="stick_breaking",
    )(h, h, h, h)


def _full_attn_call(kernel, h, qvt, batch, seq, t, extra_specs, extra_args, name):
    nq = seq // t
    per_w = WIDTH // STEP_W
    in_specs = [pl.BlockSpec((STEP_W, t), lambda b, hp, i: (hp, b * nq + i)),
                pl.BlockSpec((seq, STEP_W), lambda b, hp, i: (b, hp)),
                pl.BlockSpec((STEP_W, seq), lambda b, hp, i: (per_w + hp, b)),
                pl.BlockSpec((t, STEP_W), lambda b, hp, i: (b * nq + i, per_w + hp))]
    return pl.pallas_call(
        functools.partial(kernel, t=t),
        grid=(batch, per_w, nq),
        in_specs=in_specs + extra_specs,
        out_specs=pl.BlockSpec((t, STEP_W), lambda b, hp, i: (b * nq + i, hp)),
        out_shape=jax.ShapeDtypeStruct((batch * seq, WIDTH), BF16),
        compiler_params=_cparams("arbitrary", "arbitrary", "arbitrary"),
        name=name,
    )(qvt, h, qvt, h, *extra_args)


KEY_PARTS = 3


def _fox_prep_kernel(fl_ref, bias_ref, ct_ref, ck_ref, *, seq):
    row = lax.broadcasted_iota(jnp.int32, (LANES, LANES), 0)
    col = lax.broadcasted_iota(jnp.int32, (LANES, LANES), 1)
    upto = jnp.where(row <= col, 1.0, 0.0).astype(BF16)
    downto = jnp.where(col <= row, 1.0, 0.0).astype(BF16)
    place = [jnp.where((col == row + j * N_HEADS) & (row < N_HEADS), 1.0, 0.0).astype(BF16)
             for j in range(KEY_PARTS)]
    carry_t = jnp.zeros((LANES, 1), F32)
    carry = jnp.zeros((1, LANES), F32)
    for blk in range(seq // LANES):
        x = fl_ref[blk * LANES:(blk + 1) * LANES, :] + bias_ref[...]
        log_f = jnp.minimum(x, 0.0) - jnp.log1p(jnp.exp(-jnp.abs(x)))
        cs_t = sum(_dot(part, upto) for part in _split3(log_f.T)) + carry_t
        ct_ref[:, blk * LANES:(blk + 1) * LANES] = cs_t * LOG2E
        carry_t = cs_t[:, LANES - 1:LANES]
        cs = sum(_dot(downto, part) for part in _split3(log_f)) + carry
        carry = cs[LANES - 1:LANES, :]
        parts = _split3(-(cs * LOG2E))
        ck_ref[blk * LANES:(blk + 1) * LANES, :] = sum(
            _dot(part, sel) for part, sel in zip(parts, place)).astype(ck_ref.dtype)


def _fox_prep(fl, fgate_bias, batch, seq):
    assert KEY_PARTS * N_HEADS <= LANES
    bias = jnp.zeros((1, LANES), F32).at[0, :N_HEADS].set(fgate_bias.astype(F32))
    return pl.pallas_call(
        functools.partial(_fox_prep_kernel, seq=seq),
        grid=(batch,),
        in_specs=[pl.BlockSpec((seq, LANES), lambda b: (b, 0)),
                  pl.BlockSpec((1, LANES), lambda b: (0, 0))],
        out_specs=[pl.BlockSpec((None, LANES, seq), lambda b: (b, 0, 0)),
                   pl.BlockSpec((seq, LANES), lambda b: (b, 0))],
        out_shape=[jax.ShapeDtypeStruct((batch, LANES, seq), F32),
                   jax.ShapeDtypeStruct((batch * seq, LANES), BF16)],
        compiler_params=_cparams("arbitrary"),
        name="fox_prep",
    )(fl, bias)


def _fox_kernel(qt_ref, k_ref, vt_ref, z_ref, crow_ref, ck_ref, o_ref, *, t):
    first_head = pl.program_id(1) * HEADS_PER_STEP
    i = pl.program_id(2)
    key = lax.broadcasted_iota(jnp.int32, (t, t), 0)
    qry = lax.broadcasted_iota(jnp.int32, (t, t), 1)
    causal = key <= qry
    row = lax.broadcasted_iota(jnp.int32, (LANES, t), 0)
    qt_aug = [jnp.concatenate(
        [qt_ref[_head(j), :],
         jnp.where((row % N_HEADS == first_head + j) & (row < KEY_PARTS * N_HEADS), 1.0, 0.0).astype(BF16)], axis=0)
        for j in range(HEADS_PER_STEP)]

    def scores(j, kb):
        start = pl.multiple_of(kb * t, t)
        k_aug = jnp.concatenate([k_ref[pl.ds(start, t), _head(j)], ck_ref[pl.ds(start, t), :]], axis=1)
        return _dot(k_aug, qt_aug[j])

    def consume(j, x, kb, m, acc, diag):
        start = pl.multiple_of(kb * t, t)
        c_q = crow_ref[pl.ds((first_head + j) % F32_ROWS, 1), pl.ds(pl.multiple_of(i * t, t), t)]
        if diag:
            x = jnp.where(causal, x, NEG_INF)
        m_new = jnp.maximum(m, jnp.max(x, axis=0, keepdims=True) + c_q)
        p = jnp.exp2(x + (c_q - m_new))
        vt = _with_ones_rows(vt_ref[_head(j), pl.ds(start, t)])
        return m_new, jnp.exp2(m - m_new) * acc + _dot(vt, p.astype(BF16))

    final = _pipelined_heads(i, scores, consume,
                             (jnp.full((1, t), M_INIT, F32), jnp.zeros((HEAD_DIM + BF16_ROWS, t), F32)))
    o = jnp.concatenate([(acc[:HEAD_DIM] / jnp.maximum(acc[HEAD_DIM:HEAD_DIM + 1], 1e-30)).T for _, acc in final],
                        axis=1)
    o_ref[...] = (o * _silu(z_ref[...].astype(F32))).astype(o_ref.dtype)


class _Weight:
    def __init__(self, w):
        self.features_first = w.shape[1] % LANES != 0
        self.src = w.T if self.features_first else w

    def pick(self, *ranges, scale_first=None, group_pad=None):
        axis = 0 if self.features_first else 1
        parts = [lax.slice_in_dim(self.src, a, b, axis=axis) for a, b in ranges]
        if scale_first is not None:
            parts[0] = parts[0] * scale_first
        out = jnp.concatenate(parts, axis=axis)
        if group_pad is not None:
            groups, width = group_pad
            out = jnp.moveaxis(out, axis, 0)
            out = out.reshape(groups, -1, out.shape[1])
            out = jnp.pad(out, ((0, 0), (0, width - out.shape[1]), (0, 0))).reshape(groups * width, -1)
            out = jnp.moveaxis(out, 0, axis)
        return out.astype(BF16)

    def project(self, xb, operand, name, out_dtype=BF16, transposed_out=False):
        return _projection(xb, operand, out_dtype, name, features_first=self.features_first,
                           transposed_out=transposed_out)


def _layer_nsa(xb, batch, seq, rel_bias, w_in, cmp_pe_k, cmp_w1_k, cmp_w2_k, cmp_pe_v, cmp_w1_v, cmp_w2_v):
    kc0, ks0, vs0, kw0, vw0, gate0 = (WIDTH + j * KV_WIDTH for j in (0, 2, 3, 4, 5, 6))
    gate1 = gate0 + 3 * N_HEADS
    w = _Weight(w_in)
    h = w.project(xb, w.pick((kc0, vs0), (kw0, vw0), (gate1, w_in.shape[1])), "proj_nsa")
    qvt = w.project(xb, w.pick((0, WIDTH), (vs0, kw0), (vw0, gate0), scale_first=QSCALE), "proj_nsa_t",
                    transposed_out=True)
    gl = w.project(xb, w.pick((gate0, gate1), group_pad=(N_KV, LANES)), "proj_nsa_gates", out_dtype=F32)
    blk = lambda j: j * N_KV
    vrow = WIDTH // HEAD_DIM
    k_cmp, v_cmp = _nsa_compress(h, batch, seq, blk(0), blk(1), cmp_pe_k, cmp_w1_k, cmp_w2_k,
                                 cmp_pe_v, cmp_w1_v, cmp_w2_v)
    o_cmp, sel = _nsa_cmp_select(qvt, batch, seq, k_cmp, v_cmp, rel_bias)
    o_sel = _nsa_sel_attn(h, qvt, batch, seq, blk(2), vrow, sel, rel_bias)
    nd = NSA_WINDOW // LANES + 1
    return _banded_gqa(h, qvt, batch, seq, blk(3), vrow + N_KV, 4 * KV_WIDTH // GROUP_W,
                       _band_tables(rel_bias, nd, NSA_WINDOW, True), mix=(o_cmp, o_sel, gl), name="nsa_window_mix")


def _layer_swa(xb, batch, seq, rel_bias, w_in, sinks):
    v0, z0 = WIDTH + KV_WIDTH, WIDTH + 2 * KV_WIDTH
    w = _Weight(w_in)
    h = w.project(xb, w.pick((WIDTH, v0), (z0, w_in.shape[1])), "proj_swa")
    qvt = w.project(xb, w.pick((0, WIDTH), (v0, z0), scale_first=QSCALE), "proj_swa_t",
                    transposed_out=True)
    nd = SWA_WINDOW // LANES + 1
    return _banded_gqa(h, qvt, batch, seq, 0, WIDTH // HEAD_DIM, KV_WIDTH // GROUP_W,
                       _band_tables(rel_bias, nd, SWA_WINDOW, True), sinks=sinks, name="swa")


def _layer_sb(xb, batch, seq, w_in, t=256):
    w = _Weight(w_in)
    h = w.project(xb, w.pick((0, WIDTH), (WIDTH, w_in.shape[1]), scale_first=QSCALE), "proj_sb")
    return _sb_call(h, batch, seq, t)


def _layer_fox(xb, batch, seq, w_in, fgate_bias, t=256):
    f0 = 3 * WIDTH
    f1 = f0 + N_HEADS
    w = _Weight(w_in)
    h = w.project(xb, w.pick((WIDTH, 2 * WIDTH), (f1, w_in.shape[1])), "proj_fox")
    qvt = w.project(xb, w.pick((0, WIDTH), (2 * WIDTH, f0), scale_first=QSCALE), "proj_fox_t",
                    transposed_out=True)
    fl = w.project(xb, w.pick((f0, f1), group_pad=(1, LANES)), "proj_fox_gates", out_dtype=F32)
    ct, ck = _fox_prep(fl, fgate_bias, batch, seq)
    assert F32_ROWS % HEADS_PER_STEP == 0
    extra_specs = [pl.BlockSpec((None, F32_ROWS, seq), lambda b, hp, i: (b, hp * HEADS_PER_STEP // F32_ROWS, 0)),
                   pl.BlockSpec((seq, LANES), lambda b, hp, i: (b, 0))]
    return _full_attn_call(_fox_kernel, h, qvt, batch, seq, t, extra_specs, [ct, ck], "forgetting")


def kernel(x, rel_bias, w_in_a, w_out_a, ln_g_a, ln_b_a, cmp_pe_k, cmp_w1_k, cmp_w2_k, cmp_pe_v, cmp_w1_v, cmp_w2_v, w_in_b, w_out_b, ln_g_b, ln_b_b, sinks_b, w_in_c, w_out_c, ln_g_c, ln_b_c, w_in_d, w_out_d, ln_g_d, ln_b_d, fgate_bias_d):
    batch, seq, d_model = x.shape
    xf = x.reshape(batch * seq, d_model)
    xb = xf.astype(BF16)
    og = _layer_nsa(xb, batch, seq, rel_bias, w_in_a, cmp_pe_k, cmp_w1_k, cmp_w2_k, cmp_pe_v, cmp_w1_v, cmp_w2_v)
    xf, xb = _out_proj_ln(og, w_out_a, xf, ln_g_a, ln_b_a)
    og = _layer_swa(xb, batch, seq, rel_bias, w_in_b, sinks_b)
    xf, xb = _out_proj_ln(og, w_out_b, xf, ln_g_b, ln_b_b)
    og = _layer_sb(xb, batch, seq, w_in_c)
    xf, xb = _out_proj_ln(og, w_out_c, xf, ln_g_c, ln_b_c)
    og = _layer_fox(xb, batch, seq, w_in_d, fgate_bias_d)
    xf, xb = _out_proj_ln(og, w_out_d, xf, ln_g_d, ln_b_d)
    return xf.reshape(batch, seq, d_model)
```

```python
import functools
import math

import numpy as np
import jax
import jax.numpy as jnp
from jax import lax
from jax.experimental import pallas as pl
from jax.experimental.pallas import tpu as pltpu

F32 = jnp.float32
BF16 = jnp.bfloat16

N_HEADS = 16
HEAD_DIM = 128
N_KV = 4
HPG = N_HEADS // N_KV
WIDTH = N_HEADS * HEAD_DIM
KV_WIDTH = N_KV * HEAD_DIM
GROUP_W = HPG * HEAD_DIM
REL_BUCKETS = 32
REL_MAX_DIST = 128
CMP_LEN = 32
CMP_STRIDE = 16
SEL_LEN = 64
SEL_TOPK = 8
NSA_WINDOW = 512
SWA_WINDOW = 128
DEPTH = 4
DN_ALPHA = (2 * DEPTH) ** 0.25
LN_EPS = 1e-5
NEG_INF = -1e30
FORCED_SCORE = 1e4
SCALE = HEAD_DIM ** -0.5
LOG2E = math.log2(math.e)
QSCALE = SCALE * LOG2E

LANES = 128
BF16_ROWS = 16
F32_ROWS = 8
LN_ROWS = 128
TILES_PER_STEP = 8
M_INIT = -1e29
VMEM_LIMIT = 56 * 1024 * 1024


def _cparams(*sem):
    return pltpu.CompilerParams(dimension_semantics=sem, vmem_limit_bytes=VMEM_LIMIT)


def _silu(z):
    return z / (1.0 + jnp.exp(-z))


def _dot_nt(a, b):
    return lax.dot_general(a, b, (((1,), (1,)), ((), ())), preferred_element_type=F32)


def _dot(a, b):
    return jnp.dot(a, b, preferred_element_type=F32)


def _head(j):
    return slice(j * HEAD_DIM, (j + 1) * HEAD_DIM)


def _group_queries(qt):
    return jnp.concatenate([qt[r * HEAD_DIM:(r + 1) * HEAD_DIM] for r in range(HPG)], axis=1)


def _split3(x):
    a1 = x.astype(BF16)
    r1 = x - a1.astype(F32)
    a2 = r1.astype(BF16)
    a3 = (r1 - a2.astype(F32)).astype(BF16)
    return a1, a2, a3


def _proj_kernel(x_ref, w_ref, o_ref, *, w_dim, transposed_out):
    x, w = x_ref[...], w_ref[...]
    if transposed_out:
        y = lax.dot_general(w, x, (((w_dim,), (1,)), ((), ())), preferred_element_type=F32)
    else:
        y = lax.dot_general(x, w, (((1,), (w_dim,)), ((), ())), preferred_element_type=F32)
    o_ref[...] = y.astype(o_ref.dtype)


def _feature_tile(n):
    return next(t for t in (1024, 1280, 1536, 768, 512, 256, 128) if n % t == 0)


def _projection(x, w, out_dtype, name, *, features_first=False, transposed_out=False, tm=1024):
    m, k = x.shape
    n = w.shape[0] if features_first else w.shape[1]
    tn = _feature_tile(n)
    assert m % tm == 0
    w_spec = (pl.BlockSpec((tn, k), lambda j, i: (j, 0)) if features_first
              else pl.BlockSpec((k, tn), lambda j, i: (0, j)))
    if transposed_out:
        out_spec, out_shape = pl.BlockSpec((tn, tm), lambda j, i: (j, i)), (n, m)
    else:
        out_spec, out_shape = pl.BlockSpec((tm, tn), lambda j, i: (i, j)), (m, n)
    return pl.pallas_call(
        functools.partial(_proj_kernel, w_dim=1 if features_first else 0, transposed_out=transposed_out),
        grid=(n // tn, m // tm),
        in_specs=[pl.BlockSpec((tm, k), lambda j, i: (i, 0)), w_spec],
        out_specs=out_spec,
        out_shape=jax.ShapeDtypeStruct(out_shape, out_dtype),
        compiler_params=_cparams("arbitrary", "arbitrary"),
        name=name,
    )(x, w)


def _outln_kernel(og_ref, w_ref, x_ref, g_ref, b_ref, xo_ref, xb_ref):
    for c in range(og_ref.shape[0] // LN_ROWS):
        rows = slice(c * LN_ROWS, (c + 1) * LN_ROWS)
        t = DN_ALPHA * x_ref[rows, :] + _dot(og_ref[rows, :], w_ref[...])
        mu = jnp.mean(t, axis=-1, keepdims=True)
        d = t - mu
        var = jnp.mean(d * d, axis=-1, keepdims=True)
        out = d * lax.rsqrt(var + LN_EPS) * g_ref[...] + b_ref[...]
        xo_ref[rows, :] = out
        xb_ref[rows, :] = out.astype(BF16)


def _out_proj_ln(og, w_out, x, ln_g, ln_b, tm=512):
    m, k = og.shape
    n = w_out.shape[1]
    return pl.pallas_call(
        _outln_kernel,
        grid=(m // tm,),
        in_specs=[pl.BlockSpec((tm, k), lambda i: (i, 0)),
                  pl.BlockSpec((k, n), lambda i: (0, 0)),
                  pl.BlockSpec((tm, n), lambda i: (i, 0)),
                  pl.BlockSpec((1, n), lambda i: (0, 0)),
                  pl.BlockSpec((1, n), lambda i: (0, 0))],
        out_specs=[pl.BlockSpec((tm, n), lambda i: (i, 0)),
                   pl.BlockSpec((tm, n), lambda i: (i, 0))],
        out_shape=[jax.ShapeDtypeStruct((m, n), F32), jax.ShapeDtypeStruct((m, n), BF16)],
        compiler_params=_cparams("arbitrary"),
        name="out_proj_ln",
    )(og, w_out.astype(BF16), x, ln_g.reshape(1, n), ln_b.reshape(1, n))


def _bucket_np(dist):
    max_exact = REL_BUCKETS // 2
    ratio = np.maximum(dist, max_exact).astype(np.float32) / max_exact
    large = max_exact + (np.log(ratio) / math.log(REL_MAX_DIST / max_exact)
                         * (REL_BUCKETS - max_exact)).astype(np.int32)
    return np.where(dist < max_exact, dist, np.minimum(large, REL_BUCKETS - 1))


def _rel_table(rel_bias, dist, ok):
    bucket = _bucket_np(np.maximum(dist, 0)).reshape(-1)
    onehot_t = np.zeros((REL_BUCKETS, bucket.size), np.float32)
    onehot_t[bucket, np.arange(bucket.size)] = 1.0
    t = jnp.dot(rel_bias.T.astype(F32) * LOG2E, jnp.asarray(onehot_t), precision=lax.Precision.HIGHEST)
    blocked = np.where(ok.reshape(-1), 0.0, NEG_INF).astype(np.float32)
    return (t + blocked[None, :]).reshape((N_HEADS,) + dist.shape)


def _band_tables(rel_bias, nd, window, with_blocked=False):
    key = np.arange(LANES)[:, None]
    qry = np.arange(LANES)[None, :]
    dist = np.stack([LANES * d + qry - key for d in range(nd)])
    ok = dist >= 0
    if window is not None:
        ok &= dist < window
    if with_blocked:
        dist = np.concatenate([dist, np.zeros((1, LANES, LANES), dist.dtype)])
        ok = np.concatenate([ok, np.zeros((1, LANES, LANES), bool)])
    return _rel_table(rel_bias, dist, ok)


def _group_table(tbl_ref, idx):
    return jnp.concatenate([tbl_ref[r, idx] for r in range(HPG)], axis=1)


def _with_ones_rows(vt):
    return jnp.concatenate([vt, jnp.ones((BF16_ROWS, vt.shape[1]), BF16)], axis=0)


def _finish_group(acc):
    o_t = acc[:HEAD_DIM] / jnp.maximum(acc[HEAD_DIM:HEAD_DIM + 1], 1e-30)
    return jnp.concatenate([o_t[:, r * LANES:(r + 1) * LANES].T for r in range(HPG)], axis=1)


def _online_update(x, vt, m, acc):
    m_new = jnp.maximum(m, jnp.max(x, axis=0, keepdims=True))
    p = jnp.exp2(x - m_new)
    acc = jnp.exp2(m - m_new) * acc + _dot(_with_ones_rows(vt), p.astype(BF16))
    return m_new, acc


def _banded_kernel(*refs, nd, has_sink, mixed):
    refs = list(refs)
    qt_ref, k_ref, vt_ref, tbl_ref, z_ref = refs[:5]
    rest = refs[5:]
    sink_ref = rest.pop(0) if has_sink else None
    oc_ref, os_ref, gl_ref = (rest.pop(0), rest.pop(0), rest.pop(0)) if mixed else (None, None, None)
    (o_ref,) = rest
    for u in range(TILES_PER_STEP):
        i = pl.program_id(2) * TILES_PER_STEP + u
        tile = slice(u * LANES, (u + 1) * LANES)
        ks, vts, tbls = [], [], []
        for d in range(nd):
            kb = i - d
            start = pl.multiple_of(jnp.maximum(kb, 0) * LANES, LANES)
            ks.append(k_ref[pl.ds(start, LANES), :])
            vts.append(vt_ref[:, pl.ds(start, LANES)])
            tbls.append(_group_table(tbl_ref, jnp.where(kb >= 0, d, nd)))
        x = _dot(jnp.concatenate(ks, axis=0), _group_queries(qt_ref[:, tile])) + jnp.concatenate(tbls, axis=0)
        m = jnp.max(x, axis=0, keepdims=True)
        if has_sink:
            m = jnp.maximum(m, sink_ref[...])
        p = jnp.exp2(x - m)
        acc = _dot(_with_ones_rows(jnp.concatenate(vts, axis=1)), p.astype(BF16))
        if has_sink:
            rows = lax.broadcasted_iota(jnp.int32, acc.shape, 0)
            acc = acc + jnp.where(rows >= HEAD_DIM, jnp.exp2(sink_ref[...] - m), 0.0)
        o = _finish_group(acc)
        if mixed:
            gates = 1.0 / (1.0 + jnp.exp(-gl_ref[tile, :]))
            oc = oc_ref[tile, :].astype(F32)
            osel = os_ref[tile, :].astype(F32)
            o = jnp.concatenate(
                [gates[:, 3 * r:3 * r + 1] * oc[:, _head(r)] + gates[:, 3 * r + 1:3 * r + 2] * osel[:, _head(r)]
                 + gates[:, 3 * r + 2:3 * r + 3] * o[:, _head(r)] for r in range(HPG)], axis=1)
        o_ref[tile, :] = (o * _silu(z_ref[tile, :].astype(F32))).astype(o_ref.dtype)


def _banded_gqa(h, qvt, batch, seq, kblk, vtblk, zblk, tbl, sinks=None, mix=None, name="banded"):
    step = TILES_PER_STEP * LANES
    nq = seq // step
    nd = tbl.shape[1] - 1
    rows_of_group = lambda blk: pl.BlockSpec((step, GROUP_W), lambda b, g, i: (b * nq + i, blk + g))
    in_specs = [pl.BlockSpec((GROUP_W, step), lambda b, g, i: (g, b * nq + i)),
                pl.BlockSpec((seq, HEAD_DIM), lambda b, g, i: (b, kblk + g)),
                pl.BlockSpec((HEAD_DIM, seq), lambda b, g, i: (vtblk + g, b)),
                pl.BlockSpec((HPG, nd + 1, LANES, LANES), lambda b, g, i: (g, 0, 0, 0)),
                rows_of_group(zblk)]
    args = [qvt, h, qvt, tbl, h]
    if sinks is not None:
        sink_row = jnp.repeat(sinks.astype(F32).reshape(N_KV, HPG) * LOG2E, LANES, axis=1)
        in_specs.append(pl.BlockSpec((None, 1, HPG * LANES), lambda b, g, i: (g, 0, 0)))
        args.append(sink_row.reshape(N_KV, 1, HPG * LANES))
    if mix is not None:
        in_specs += [rows_of_group(0), rows_of_group(0),
                     pl.BlockSpec((step, LANES), lambda b, g, i: (b * nq + i, g))]
        args += list(mix)
    return pl.pallas_call(
        functools.partial(_banded_kernel, nd=nd, has_sink=sinks is not None, mixed=mix is not None),
        grid=(batch, N_KV, nq),
        in_specs=in_specs,
        out_specs=pl.BlockSpec((step, GROUP_W), lambda b, g, i: (b * nq + i, g)),
        out_shape=jax.ShapeDtypeStruct((batch * seq, WIDTH), BF16),
        compiler_params=_cparams("arbitrary", "arbitrary", "arbitrary"),
        name=name,
    )(*args)


def _compress_kernel(kc_ref, vc_ref, pek_ref, w1k_ref, w2k_ref, pev_ref, w1v_ref, w2v_ref,
                     ko_ref, vo_ref, xs_ref, *, nc):
    half = CMP_LEN // 2
    for x_ref, pe_ref, w1_ref, w2_ref, o_ref in ((kc_ref, pek_ref, w1k_ref, w2k_ref, ko_ref),
                                                 (vc_ref, pev_ref, w1v_ref, w2v_ref, vo_ref)):
        xs_ref[...] = x_ref[...].astype(F32)
        first = jnp.zeros((nc, HEAD_DIM), F32)
        second = jnp.zeros((nc, HEAD_DIM), F32)
        for l in range(half):
            xl = xs_ref[pl.ds(l, nc, stride=CMP_STRIDE), :]
            first += _dot((xl + pe_ref[l:l + 1, :]).astype(BF16), w1_ref[l])
            second += _dot((xl + pe_ref[half + l:half + l + 1, :]).astype(BF16), w1_ref[half + l])
        hid = _silu(first + pltpu.roll(second, nc - 1, 0))
        o_ref[...] = _dot(hid.astype(BF16), w2_ref[...]).astype(o_ref.dtype)


def _nsa_compress(h, batch, seq, kcblk, vcblk, pe_k, w1_k, w2_k, pe_v, w1_v, w2_v):
    assert CMP_LEN == 2 * CMP_STRIDE
    nc = seq // CMP_STRIDE
    kv_spec = lambda blk: pl.BlockSpec((seq, HEAD_DIM), lambda b, g: (b, blk + g))
    full = lambda shape: pl.BlockSpec(shape, lambda b, g: (0,) * len(shape))
    out_spec = pl.BlockSpec((None, None, nc, HEAD_DIM), lambda b, g: (b, g, 0, 0))
    out_shape = jax.ShapeDtypeStruct((batch, N_KV, nc, HEAD_DIM), BF16)
    return pl.pallas_call(
        functools.partial(_compress_kernel, nc=nc),
        grid=(batch, N_KV),
        in_specs=[kv_spec(kcblk), kv_spec(vcblk),
                  full((CMP_LEN, HEAD_DIM)), full((CMP_LEN, HEAD_DIM, HEAD_DIM)), full((HEAD_DIM, HEAD_DIM)),
                  full((CMP_LEN, HEAD_DIM)), full((CMP_LEN, HEAD_DIM, HEAD_DIM)), full((HEAD_DIM, HEAD_DIM))],
        out_specs=[out_spec, out_spec],
        out_shape=[out_shape, out_shape],
        scratch_shapes=[pltpu.VMEM((seq, HEAD_DIM), F32)],
        compiler_params=_cparams("arbitrary", "arbitrary"),
        name="nsa_compress",
    )(h, h, pe_k, w1_k.astype(BF16), w2_k.astype(BF16), pe_v, w1_v.astype(BF16), w2_v.astype(BF16))


def _cmp_table(rel_bias, nc):
    u = np.arange(2 * nc)[:, None] - nc
    dist = np.arange(LANES)[None, :] - (u * CMP_STRIDE + CMP_LEN - 1)
    return _rel_table(rel_bias, dist, dist >= 0)


def _inter_t(seq, nc):
    nb = seq // SEL_LEN
    cstart = np.arange(nc) * CMP_STRIDE
    sstart = np.arange(nb) * SEL_LEN
    inter = np.clip(np.minimum(cstart[None, :] + CMP_LEN, sstart[:, None] + SEL_LEN)
                    - np.maximum(cstart[None, :], sstart[:, None]), 0, None) / CMP_LEN
    return inter.astype(np.float32)


def _cmp_select_kernel(qt_ref, kc_ref, vc_ref, tbl_ref, inter_ref, o_ref, sel_ref, *, nb, nc):
    first = pl.program_id(1) * TILES_PER_STEP
    width = TILES_PER_STEP * LANES
    kc = kc_ref[...]
    vc = vc_ref[...]
    per_tile = LANES // CMP_STRIDE
    row0 = [pl.multiple_of(nc - (first + u) * per_tile, per_tile) for u in range(TILES_PER_STEP)]
    outs = []
    p_sum_t = None
    for r in range(HPG):
        tbl = jnp.concatenate([tbl_ref[r, pl.ds(row, nc), :] for row in row0], axis=1)
        st = _dot(kc, qt_ref[r * HEAD_DIM:(r + 1) * HEAD_DIM, :]) + tbl
        pt = jnp.where(tbl > 0.5 * NEG_INF, jnp.exp2(st - jnp.max(st, axis=0, keepdims=True)), 0.0)
        pt = pt / jnp.maximum(jnp.sum(pt, axis=0, keepdims=True), 1e-30)
        p_sum_t = pt if p_sum_t is None else p_sum_t + pt
        outs.append(_dot(pt.T.astype(BF16), vc))
    o_ref[...] = jnp.concatenate(outs, axis=1).astype(o_ref.dtype)

    inter = inter_ref[...]
    imp = sum(_dot(inter, part) for part in _split3(p_sum_t))
    blk = lax.broadcasted_iota(jnp.int32, (nb, width), 0)
    pos = first * LANES + lax.broadcasted_iota(jnp.int32, (nb, width), 1)
    cur = lax.shift_right_logical(pos, int(math.log2(SEL_LEN)))
    allowed = blk * SEL_LEN <= pos
    forced = (blk == 0) | (blk == cur) | (blk == cur - 1)
    imp = jnp.where(allowed, jnp.where(forced, FORCED_SCORE, imp), NEG_INF)
    rank = jnp.zeros((nb, width), F32)
    for c in range(nb):
        row = imp[c:c + 1, :]
        earlier = jnp.where(blk > c, 1.0, 0.0)
        rank += jnp.where(row > imp, 1.0, jnp.where(row == imp, earlier, 0.0))
    sel_t = jnp.where((rank < SEL_TOPK) & (imp > 0.5 * NEG_INF), 1.0, 0.0)
    sel_t = jnp.concatenate([sel_t, jnp.zeros((LANES - nb, width), F32)], axis=0).astype(sel_ref.dtype)
    for u in range(TILES_PER_STEP):
        sel_ref[u] = sel_t[:, u * LANES:(u + 1) * LANES]


def _nsa_cmp_select(qvt, batch, seq, k_cmp, v_cmp, rel_bias):
    step = TILES_PER_STEP * LANES
    nq = seq // step
    nc = seq // CMP_STRIDE
    nb = seq // SEL_LEN
    assert nc == LANES and nb <= LANES
    tbl = _cmp_table(rel_bias, nc)
    inter = jnp.asarray(_inter_t(seq, nc), BF16)
    cmp_spec = pl.BlockSpec((None, None, nc, HEAD_DIM), lambda g, i, b: (b, g, 0, 0))
    return pl.pallas_call(
        functools.partial(_cmp_select_kernel, nb=nb, nc=nc),
        grid=(N_KV, nq, batch),
        in_specs=[pl.BlockSpec((GROUP_W, step), lambda g, i, b: (g, b * nq + i)),
                  cmp_spec, cmp_spec,
                  pl.BlockSpec((HPG, 2 * nc, LANES), lambda g, i, b: (g, 0, 0)),
                  pl.BlockSpec((nb, nc), lambda g, i, b: (0, 0))],
        out_specs=[pl.BlockSpec((step, GROUP_W), lambda g, i, b: (b * nq + i, g)),
                   pl.BlockSpec((None, None, TILES_PER_STEP, LANES, LANES), lambda g, i, b: (b, g, i, 0, 0))],
        out_shape=[jax.ShapeDtypeStruct((batch * seq, WIDTH), BF16),
                   jax.ShapeDtypeStruct((batch, N_KV, seq // LANES, LANES, LANES), BF16)],
        compiler_params=_cparams("arbitrary", "arbitrary", "arbitrary"),
        name="nsa_cmp_select",
    )(qvt, k_cmp, v_cmp, tbl, inter)


SEL_TILES = 2
SEL_KEYS = SEL_TILES * LANES


def _expand_np(seq):
    j = np.arange(LANES)[None, None, :]
    key = np.arange(seq // SEL_KEYS)[:, None, None] * SEL_KEYS + np.arange(SEL_KEYS)[None, :, None]
    return (j == key // SEL_LEN).astype(np.float32)


def _sel_attn_kernel(qt_ref, k_ref, vt_ref, sel_ref, tbl_ref, exp_ref, o_ref):
    last = pl.program_id(2)
    lanes = HPG * LANES
    beyond = tbl_ref.shape[1] - 1
    far = beyond - 1
    tiles = range(SEL_TILES)
    qt_aug = []
    for u in tiles:
        blocked = ((1.0 - sel_ref[u].astype(F32)) * NEG_INF).astype(BF16)
        qt_aug.append(jnp.concatenate([_group_queries(qt_ref[:, u * LANES:(u + 1) * LANES]),
                                       jnp.concatenate([blocked] * HPG, axis=1)], axis=0))

    def scores(u, kt):
        start = pl.multiple_of(kt * SEL_KEYS, SEL_KEYS)
        k_aug = jnp.concatenate([k_ref[pl.ds(start, SEL_KEYS), :], exp_ref[kt]], axis=1)
        return _dot(k_aug, qt_aug[u])

    def body(kt, carry):
        start = pl.multiple_of(kt * SEL_KEYS, SEL_KEYS)
        out = []
        for u in tiles:
            s, m, acc = carry[u]
            s_next = scores(u, jnp.minimum(kt + 1, last))
            behind = [(last - kt) * SEL_TILES + u - c for c in tiles]
            x = s + jnp.concatenate(
                [_group_table(tbl_ref, jnp.where(d >= 0, jnp.minimum(d, far), beyond)) for d in behind], axis=0)
            out.append((s_next,) + _online_update(x, vt_ref[:, pl.ds(start, SEL_KEYS)], m, acc))
        return tuple(out)

    init = tuple((scores(u, 0), jnp.full((1, lanes), M_INIT, F32),
                  jnp.zeros((HEAD_DIM + BF16_ROWS, lanes), F32)) for u in tiles)
    final = lax.fori_loop(0, last + 1, body, init)
    for u in tiles:
        o_ref[u * LANES:(u + 1) * LANES, :] = _finish_group(final[u][2]).astype(o_ref.dtype)


def _nsa_sel_attn(h, qvt, batch, seq, kblk, vtblk, sel, rel_bias):
    nq = seq // SEL_KEYS
    tbl = _band_tables(rel_bias, 3, None, True)
    expand = jnp.asarray(_expand_np(seq), BF16)
    return pl.pallas_call(
        _sel_attn_kernel,
        grid=(batch, N_KV, nq),
        in_specs=[pl.BlockSpec((GROUP_W, SEL_KEYS), lambda b, g, i: (g, b * nq + i)),
                  pl.BlockSpec((seq, HEAD_DIM), lambda b, g, i: (b, kblk + g)),
                  pl.BlockSpec((HEAD_DIM, seq), lambda b, g, i: (vtblk + g, b)),
                  pl.BlockSpec((None, None, SEL_TILES, LANES, LANES), lambda b, g, i: (b, g, i, 0, 0)),
                  pl.BlockSpec((HPG, 4, LANES, LANES), lambda b, g, i: (g, 0, 0, 0)),
                  pl.BlockSpec((seq // SEL_KEYS, SEL_KEYS, LANES), lambda b, g, i: (0, 0, 0))],
        out_specs=pl.BlockSpec((SEL_KEYS, GROUP_W), lambda b, g, i: (b * nq + i, g)),
        out_shape=jax.ShapeDtypeStruct((batch * seq, WIDTH), BF16),
        compiler_params=_cparams("arbitrary", "arbitrary", "arbitrary"),
        name="nsa_sel_attn",
    )(qvt, h, qvt, sel, tbl, expand)


HEADS_PER_STEP = 8
STEP_W = HEADS_PER_STEP * HEAD_DIM


def _pipelined_heads(i, scores, consume, init):
    heads = range(HEADS_PER_STEP)
    ahead = [scores(j, jnp.maximum(i - 1, 0)) for j in heads]
    state = [consume(j, scores(j, i), i, *init, True) for j in heads]

    def body(step, carry):
        kb = i - step
        return tuple((scores(j, jnp.maximum(kb - 1, 0)),) + consume(j, carry[j][0], kb, *carry[j][1:], False)
                     for j in heads)

    carry = lax.fori_loop(1, i + 1, body, tuple((ahead[j],) + state[j] for j in heads))
    return [c[1:] for c in carry]


SB_HEADS = 4
SB_SPAN = 2
SB_W = SB_HEADS * HEAD_DIM


def _sb_kernel(q_ref, k_ref, v_ref, z_ref, o_ref, *, t):
    i = pl.program_id(2)
    tq = SB_SPAN * t
    qry = lax.broadcasted_iota(jnp.int32, (tq, t), 0)
    key = lax.broadcasted_iota(jnp.int32, (tq, t), 1)
    row = lax.broadcasted_iota(jnp.int32, (t, t), 0)
    col = lax.broadcasted_iota(jnp.int32, (t, t), 1)
    later = jnp.where(row >= col, 1.0, 0.0).astype(BF16)
    heads = range(SB_HEADS)

    def scores(j, kb):
        start = pl.multiple_of(kb * t, t)
        return _dot_nt(q_ref[:, _head(j)], k_ref[pl.ds(start, t), _head(j)])

    def consume(j, y, kb, run, acc, strict=None):
        start = pl.multiple_of(kb * t, t)
        neg = -y
        soft = jnp.log(1.0 + jnp.exp2(jnp.minimum(y, neg))) * LOG2E
        log_keep = jnp.minimum(neg, 0.0) - soft
        if strict is not None:
            log_keep = jnp.where(strict, log_keep, 0.0)
        keep = _dot(log_keep.astype(BF16), later)
        a = jnp.exp2(y + keep + run)
        if strict is not None:
            a = jnp.where(strict, a, 0.0)
        acc = acc + _dot(a.astype(BF16), v_ref[pl.ds(start, t), _head(j)])
        return run + keep[:, 0:1], acc

    top = SB_SPAN * i + SB_SPAN - 1
    state = [(scores(j, top), jnp.zeros((tq, 1), F32), jnp.zeros((tq, HEAD_DIM), F32)) for j in heads]
    for c in reversed(range(SB_SPAN)):
        kb = SB_SPAN * i + c
        strict = key + c * t < qry
        state = [(scores(j, jnp.maximum(kb - 1, 0)),) + consume(j, state[j][0], kb, *state[j][1:], strict)
                 for j in heads]

    def body(step, carry):
        kb = SB_SPAN * i - step
        return tuple((scores(j, jnp.maximum(kb - 1, 0)),) + consume(j, carry[j][0], kb, *carry[j][1:])
                     for j in heads)

    final = lax.fori_loop(1, SB_SPAN * i + 1, body, tuple(state))
    o = jnp.concatenate([acc for _, _, acc in final], axis=1)
    o_ref[...] = (o * _silu(z_ref[...].astype(F32))).astype(o_ref.dtype)


def _sb_call(h, batch, seq, t):
    tq = SB_SPAN * t
    nq = seq // tq
    per_w = WIDTH // SB_W
    rows = lambda part: pl.BlockSpec((tq, SB_W), lambda b, hp, i: (b * nq + i, part * per_w + hp))
    whole = lambda part: pl.BlockSpec((seq, SB_W), lambda b, hp, i: (b, part * per_w + hp))
    return pl.pallas_call(
        functools.partial(_sb_kernel, t=t),
        grid=(batch, per_w, nq),
        in_specs=[rows(0), whole(1), whole(2), rows(3)],
        out_specs=rows(0),
        out_shape=jax.ShapeDtypeStruct((batch * seq, WIDTH), BF16),
        compiler_params=_cparams("arbitrary", "arbitrary", "arbitrary"),
        name="stick_breaking",
    )(h, h, h, h)


def _full_attn_call(kernel, h, qvt, batch, seq, t, extra_specs, extra_args, name):
    nq = seq // t
    per_w = WIDTH // STEP_W
    in_specs = [pl.BlockSpec((STEP_W, t), lambda b, hp, i: (hp, b * nq + i)),
                pl.BlockSpec((seq, STEP_W), lambda b, hp, i: (b, hp)),
                pl.BlockSpec((STEP_W, seq), lambda b, hp, i: (per_w + hp, b)),
                pl.BlockSpec((t, STEP_W), lambda b, hp, i: (b * nq + i, per_w + hp))]
    return pl.pallas_call(
        functools.partial(kernel, t=t),
        grid=(batch, per_w, nq),
        in_specs=in_specs + extra_specs,
        out_specs=pl.BlockSpec((t, STEP_W), lambda b, hp, i: (b * nq + i, hp)),
        out_shape=jax.ShapeDtypeStruct((batch * seq, WIDTH), BF16),
        compiler_params=_cparams("arbitrary", "arbitrary", "arbitrary"),
        name=name,
    )(qvt, h, qvt, h, *extra_args)


KEY_PARTS = 3


def _fox_prep_kernel(fl_ref, bias_ref, ct_ref, ck_ref, *, seq):
    row = lax.broadcasted_iota(jnp.int32, (LANES, LANES), 0)
    col = lax.broadcasted_iota(jnp.int32, (LANES, LANES), 1)
    upto = jnp.where(row <= col, 1.0, 0.0).astype(BF16)
    downto = jnp.where(col <= row, 1.0, 0.0).astype(BF16)
    place = [jnp.where((col == row + j * N_HEADS) & (row < N_HEADS), 1.0, 0.0).astype(BF16)
             for j in range(KEY_PARTS)]
    carry_t = jnp.zeros((LANES, 1), F32)
    carry = jnp.zeros((1, LANES), F32)
    for blk in range(seq // LANES):
        x = fl_ref[blk * LANES:(blk + 1) * LANES, :] + bias_ref[...]
        log_f = jnp.minimum(x, 0.0) - jnp.log1p(jnp.exp(-jnp.abs(x)))
        cs_t = sum(_dot(part, upto) for part in _split3(log_f.T)) + carry_t
        ct_ref[:, blk * LANES:(blk + 1) * LANES] = cs_t * LOG2E
        carry_t = cs_t[:, LANES - 1:LANES]
        cs = sum(_dot(downto, part) for part in _split3(log_f)) + carry
        carry = cs[LANES - 1:LANES, :]
        parts = _split3(-(cs * LOG2E))
        ck_ref[blk * LANES:(blk + 1) * LANES, :] = sum(
            _dot(part, sel) for part, sel in zip(parts, place)).astype(ck_ref.dtype)


def _fox_prep(fl, fgate_bias, batch, seq):
    assert KEY_PARTS * N_HEADS <= LANES
    bias = jnp.zeros((1, LANES), F32).at[0, :N_HEADS].set(fgate_bias.astype(F32))
    return pl.pallas_call(
        functools.partial(_fox_prep_kernel, seq=seq),
        grid=(batch,),
        in_specs=[pl.BlockSpec((seq, LANES), lambda b: (b, 0)),
                  pl.BlockSpec((1, LANES), lambda b: (0, 0))],
        out_specs=[pl.BlockSpec((None, LANES, seq), lambda b: (b, 0, 0)),
                   pl.BlockSpec((seq, LANES), lambda b: (b, 0))],
        out_shape=[jax.ShapeDtypeStruct((batch, LANES, seq), F32),
                   jax.ShapeDtypeStruct((batch * seq, LANES), BF16)],
        compiler_params=_cparams("arbitrary"),
        name="fox_prep",
    )(fl, bias)


def _fox_kernel(qt_ref, k_ref, vt_ref, z_ref, crow_ref, ck_ref, o_ref, *, t):
    first_head = pl.program_id(1) * HEADS_PER_STEP
    i = pl.program_id(2)
    key = lax.broadcasted_iota(jnp.int32, (t, t), 0)
    qry = lax.broadcasted_iota(jnp.int32, (t, t), 1)
    causal = key <= qry
    row = lax.broadcasted_iota(jnp.int32, (LANES, t), 0)
    qt_aug = [jnp.concatenate(
        [qt_ref[_head(j), :],
         jnp.where((row % N_HEADS == first_head + j) & (row < KEY_PARTS * N_HEADS), 1.0, 0.0).astype(BF16)], axis=0)
        for j in range(HEADS_PER_STEP)]

    def scores(j, kb):
        start = pl.multiple_of(kb * t, t)
        k_aug = jnp.concatenate([k_ref[pl.ds(start, t), _head(j)], ck_ref[pl.ds(start, t), :]], axis=1)
        return _dot(k_aug, qt_aug[j])

    def consume(j, x, kb, m, acc, diag):
        start = pl.multiple_of(kb * t, t)
        c_q = crow_ref[pl.ds((first_head + j) % F32_ROWS, 1), pl.ds(pl.multiple_of(i * t, t), t)]
        if diag:
            x = jnp.where(causal, x, NEG_INF)
        m_new = jnp.maximum(m, jnp.max(x, axis=0, keepdims=True) + c_q)
        p = jnp.exp2(x + (c_q - m_new))
        vt = _with_ones_rows(vt_ref[_head(j), pl.ds(start, t)])
        return m_new, jnp.exp2(m - m_new) * acc + _dot(vt, p.astype(BF16))

    final = _pipelined_heads(i, scores, consume,
                             (jnp.full((1, t), M_INIT, F32), jnp.zeros((HEAD_DIM + BF16_ROWS, t), F32)))
    o = jnp.concatenate([(acc[:HEAD_DIM] / jnp.maximum(acc[HEAD_DIM:HEAD_DIM + 1], 1e-30)).T for _, acc in final],
                        axis=1)
    o_ref[...] = (o * _silu(z_ref[...].astype(F32))).astype(o_ref.dtype)


class _Weight:
    def __init__(self, w):
        self.features_first = w.shape[1] % LANES != 0
        self.src = w.T if self.features_first else w

    def pick(self, *ranges, scale_first=None, group_pad=None):
        axis = 0 if self.features_first else 1
        parts = [lax.slice_in_dim(self.src, a, b, axis=axis) for a, b in ranges]
        if scale_first is not None:
            parts[0] = parts[0] * scale_first
        out = jnp.concatenate(parts, axis=axis)
        if group_pad is not None:
            groups, width = group_pad
            out = jnp.moveaxis(out, axis, 0)
            out = out.reshape(groups, -1, out.shape[1])
            out = jnp.pad(out, ((0, 0), (0, width - out.shape[1]), (0, 0))).reshape(groups * width, -1)
            out = jnp.moveaxis(out, 0, axis)
        return out.astype(BF16)

    def project(self, xb, operand, name, out_dtype=BF16, transposed_out=False):
        return _projection(xb, operand, out_dtype, name, features_first=self.features_first,
                           transposed_out=transposed_out)


def _layer_nsa(xb, batch, seq, rel_bias, w_in, cmp_pe_k, cmp_w1_k, cmp_w2_k, cmp_pe_v, cmp_w1_v, cmp_w2_v):
    kc0, ks0, vs0, kw0, vw0, gate0 = (WIDTH + j * KV_WIDTH for j in (0, 2, 3, 4, 5, 6))
    gate1 = gate0 + 3 * N_HEADS
    w = _Weight(w_in)
    h = w.project(xb, w.pick((kc0, vs0), (kw0, vw0), (gate1, w_in.shape[1])), "proj_nsa")
    qvt = w.project(xb, w.pick((0, WIDTH), (vs0, kw0), (vw0, gate0), scale_first=QSCALE), "proj_nsa_t",
                    transposed_out=True)
    gl = w.project(xb, w.pick((gate0, gate1), group_pad=(N_KV, LANES)), "proj_nsa_gates", out_dtype=F32)
    blk = lambda j: j * N_KV
    vrow = WIDTH // HEAD_DIM
    k_cmp, v_cmp = _nsa_compress(h, batch, seq, blk(0), blk(1), cmp_pe_k, cmp_w1_k, cmp_w2_k,
                                 cmp_pe_v, cmp_w1_v, cmp_w2_v)
    o_cmp, sel = _nsa_cmp_select(qvt, batch, seq, k_cmp, v_cmp, rel_bias)
    o_sel = _nsa_sel_attn(h, qvt, batch, seq, blk(2), vrow, sel, rel_bias)
    nd = NSA_WINDOW // LANES + 1
    return _banded_gqa(h, qvt, batch, seq, blk(3), vrow + N_KV, 4 * KV_WIDTH // GROUP_W,
                       _band_tables(rel_bias, nd, NSA_WINDOW, True), mix=(o_cmp, o_sel, gl), name="nsa_window_mix")


def _layer_swa(xb, batch, seq, rel_bias, w_in, sinks):
    v0, z0 = WIDTH + KV_WIDTH, WIDTH + 2 * KV_WIDTH
    w = _Weight(w_in)
    h = w.project(xb, w.pick((WIDTH, v0), (z0, w_in.shape[1])), "proj_swa")
    qvt = w.project(xb, w.pick((0, WIDTH), (v0, z0), scale_first=QSCALE), "proj_swa_t",
                    transposed_out=True)
    nd = SWA_WINDOW // LANES + 1
    return _banded_gqa(h, qvt, batch, seq, 0, WIDTH // HEAD_DIM, KV_WIDTH // GROUP_W,
                       _band_tables(rel_bias, nd, SWA_WINDOW, True), sinks=sinks, name="swa")


def _layer_sb(xb, batch, seq, w_in, t=256):
    w = _Weight(w_in)
    h = w.project(xb, w.pick((0, WIDTH), (WIDTH, w_in.shape[1]), scale_first=QSCALE), "proj_sb")
    return _sb_call(h, batch, seq, t)


def _layer_fox(xb, batch, seq, w_in, fgate_bias, t=256):
    f0 = 3 * WIDTH
    f1 = f0 + N_HEADS
    w = _Weight(w_in)
    h = w.project(xb, w.pick((WIDTH, 2 * WIDTH), (f1, w_in.shape[1])), "proj_fox")
    qvt = w.project(xb, w.pick((0, WIDTH), (2 * WIDTH, f0), scale_first=QSCALE), "proj_fox_t",
                    transposed_out=True)
    fl = w.project(xb, w.pick((f0, f1), group_pad=(1, LANES)), "proj_fox_gates", out_dtype=F32)
    ct, ck = _fox_prep(fl, fgate_bias, batch, seq)
    assert F32_ROWS % HEADS_PER_STEP == 0
    extra_specs = [pl.BlockSpec((None, F32_ROWS, seq), lambda b, hp, i: (b, hp * HEADS_PER_STEP // F32_ROWS, 0)),
                   pl.BlockSpec((seq, LANES), lambda b, hp, i: (b, 0))]
    return _full_attn_call(_fox_kernel, h, qvt, batch, seq, t, extra_specs, [ct, ck], "forgetting")


def kernel(x, rel_bias, w_in_a, w_out_a, ln_g_a, ln_b_a, cmp_pe_k, cmp_w1_k, cmp_w2_k, cmp_pe_v, cmp_w1_v, cmp_w2_v, w_in_b, w_out_b, ln_g_b, ln_b_b, sinks_b, w_in_c, w_out_c, ln_g_c, ln_b_c, w_in_d, w_out_d, ln_g_d, ln_b_d, fgate_bias_d):
    batch, seq, d_model = x.shape
    xf = x.reshape(batch * seq, d_model)
    xb = xf.astype(BF16)
    og = _layer_nsa(xb, batch, seq, rel_bias, w_in_a, cmp_pe_k, cmp_w1_k, cmp_w2_k, cmp_pe_v, cmp_w1_v, cmp_w2_v)
    xf, xb = _out_proj_ln(og, w_out_a, xf, ln_g_a, ln_b_a)
    og = _layer_swa(xb, batch, seq, rel_bias, w_in_b, sinks_b)
    xf, xb = _out_proj_ln(og, w_out_b, xf, ln_g_b, ln_b_b)
    og = _layer_sb(xb, batch, seq, w_in_c)
    xf, xb = _out_proj_ln(og, w_out_c, xf, ln_g_c, ln_b_c)
    og = _layer_fox(xb, batch, seq, w_in_d, fgate_bias_d)
    xf, xb = _out_proj_ln(og, w_out_d, xf, ln_g_d, ln_b_d)
    return xf.reshape(batch, seq, d_model)
```

```python
import functools
import math

import numpy as np
import jax
import jax.numpy as jnp
from jax import lax
from jax.experimental import pallas as pl
from jax.experimental.pallas import tpu as pltpu

F32 = jnp.float32
BF16 = jnp.bfloat16

N_HEADS = 16
HEAD_DIM = 128
N_KV = 4
HPG = N_HEADS // N_KV
WIDTH = N_HEADS * HEAD_DIM
KV_WIDTH = N_KV * HEAD_DIM
GROUP_W = HPG * HEAD_DIM
REL_BUCKETS = 32
REL_MAX_DIST = 128
CMP_LEN = 32
CMP_STRIDE = 16
SEL_LEN = 64
SEL_TOPK = 8
NSA_WINDOW = 512
SWA_WINDOW = 128
DEPTH = 4
DN_ALPHA = (2 * DEPTH) ** 0.25
LN_EPS = 1e-5
NEG_INF = -1e30
FORCED_SCORE = 1e4
SCALE = HEAD_DIM ** -0.5
LOG2E = math.log2(math.e)
QSCALE = SCALE * LOG2E

LANES = 128
BF16_ROWS = 16
F32_ROWS = 8
LN_ROWS = 128
TILES_PER_STEP = 8
M_INIT = -1e29
VMEM_LIMIT = 56 * 1024 * 1024


def _cparams(*sem):
    return pltpu.CompilerParams(dimension_semantics=sem, vmem_limit_bytes=VMEM_LIMIT)


def _silu(z):
    return z / (1.0 + jnp.exp(-z))


def _dot_nt(a, b):
    return lax.dot_general(a, b, (((1,), (1,)), ((), ())), preferred_element_type=F32)


def _dot(a, b):
    return jnp.dot(a, b, preferred_element_type=F32)


def _head(j):
    return slice(j * HEAD_DIM, (j + 1) * HEAD_DIM)


def _group_queries(qt):
    return jnp.concatenate([qt[r * HEAD_DIM:(r + 1) * HEAD_DIM] for r in range(HPG)], axis=1)


def _split3(x):
    a1 = x.astype(BF16)
    r1 = x - a1.astype(F32)
    a2 = r1.astype(BF16)
    a3 = (r1 - a2.astype(F32)).astype(BF16)
    return a1, a2, a3


def _proj_kernel(x_ref, w_ref, o_ref, *, w_dim, transposed_out):
    x, w = x_ref[...], w_ref[...]
    if transposed_out:
        y = lax.dot_general(w, x, (((w_dim,), (1,)), ((), ())), preferred_element_type=F32)
    else:
        y = lax.dot_general(x, w, (((1,), (w_dim,)), ((), ())), preferred_element_type=F32)
    o_ref[...] = y.astype(o_ref.dtype)


def _feature_tile(n):
    return next(t for t in (1024, 1280, 1536, 768, 512, 256, 128) if n % t == 0)


def _projection(x, w, out_dtype, name, *, features_first=False, transposed_out=False, tm=1024):
    m, k = x.shape
    n = w.shape[0] if features_first else w.shape[1]
    tn = _feature_tile(n)
    assert m % tm == 0
    w_spec = (pl.BlockSpec((tn, k), lambda j, i: (j, 0)) if features_first
              else pl.BlockSpec((k, tn), lambda j, i: (0, j)))
    if transposed_out:
        out_spec, out_shape = pl.BlockSpec((tn, tm), lambda j, i: (j, i)), (n, m)
    else:
        out_spec, out_shape = pl.BlockSpec((tm, tn), lambda j, i: (i, j)), (m, n)
    return pl.pallas_call(
        functools.partial(_proj_kernel, w_dim=1 if features_first else 0, transposed_out=transposed_out),
        grid=(n // tn, m // tm),
        in_specs=[pl.BlockSpec((tm, k), lambda j, i: (i, 0)), w_spec],
        out_specs=out_spec,
        out_shape=jax.ShapeDtypeStruct(out_shape, out_dtype),
        compiler_params=_cparams("arbitrary", "arbitrary"),
        name=name,
    )(x, w)


def _outln_kernel(og_ref, w_ref, x_ref, g_ref, b_ref, xo_ref, *maybe_xb_ref):
    for c in range(og_ref.shape[0] // LN_ROWS):
        rows = slice(c * LN_ROWS, (c + 1) * LN_ROWS)
        t = DN_ALPHA * x_ref[rows, :] + _dot(og_ref[rows, :], w_ref[...])
        mu = jnp.mean(t, axis=-1, keepdims=True)
        d = t - mu
        var = jnp.mean(d * d, axis=-1, keepdims=True)
        out = d * lax.rsqrt(var + LN_EPS) * g_ref[...] + b_ref[...]
        xo_ref[rows, :] = out
        for xb_ref in maybe_xb_ref:
            xb_ref[rows, :] = out.astype(BF16)


def _out_proj_ln(og, w_out, x, ln_g, ln_b, with_bf16, tm=512):
    m, k = og.shape
    n = w_out.shape[1]
    rows = pl.BlockSpec((tm, n), lambda i: (i, 0))
    out_dtypes = (F32, BF16) if with_bf16 else (F32,)
    return pl.pallas_call(
        _outln_kernel,
        grid=(m // tm,),
        in_specs=[pl.BlockSpec((tm, k), lambda i: (i, 0)),
                  pl.BlockSpec((k, n), lambda i: (0, 0)),
                  rows,
                  pl.BlockSpec((1, n), lambda i: (0, 0)),
                  pl.BlockSpec((1, n), lambda i: (0, 0))],
        out_specs=[rows] * len(out_dtypes),
        out_shape=[jax.ShapeDtypeStruct((m, n), dt) for dt in out_dtypes],
        compiler_params=_cparams("arbitrary"),
        name="out_proj_ln",
    )(og, w_out.astype(BF16), x, ln_g.reshape(1, n), ln_b.reshape(1, n))


def _bucket_np(dist):
    max_exact = REL_BUCKETS // 2
    ratio = np.maximum(dist, max_exact).astype(np.float32) / max_exact
    large = max_exact + (np.log(ratio) / math.log(REL_MAX_DIST / max_exact)
                         * (REL_BUCKETS - max_exact)).astype(np.int32)
    return np.where(dist < max_exact, dist, np.minimum(large, REL_BUCKETS - 1))


def _rel_table(rel_bias, dist, ok):
    bucket = _bucket_np(np.maximum(dist, 0)).reshape(-1)
    onehot_t = np.zeros((REL_BUCKETS, bucket.size), np.float32)
    onehot_t[bucket, np.arange(bucket.size)] = 1.0
    t = jnp.dot(rel_bias.T.astype(F32) * LOG2E, jnp.asarray(onehot_t), precision=lax.Precision.HIGHEST)
    blocked = np.where(ok.reshape(-1), 0.0, NEG_INF).astype(np.float32)
    return (t + blocked[None, :]).reshape((N_HEADS,) + dist.shape)


def _band_tables(rel_bias, nd, window, with_blocked=False):
    key = np.arange(LANES)[:, None]
    qry = np.arange(LANES)[None, :]
    dist = np.stack([LANES * d + qry - key for d in range(nd)])
    ok = dist >= 0
    if window is not None:
        ok &= dist < window
    if with_blocked:
        dist = np.concatenate([dist, np.zeros((1, LANES, LANES), dist.dtype)])
        ok = np.concatenate([ok, np.zeros((1, LANES, LANES), bool)])
    return _rel_table(rel_bias, dist, ok)


def _group_table(tbl_ref, idx):
    return jnp.concatenate([tbl_ref[r, idx] for r in range(HPG)], axis=1)


def _with_ones_rows(vt):
    return jnp.concatenate([vt, jnp.ones((BF16_ROWS, vt.shape[1]), BF16)], axis=0)


def _finish_group(acc):
    o_t = acc[:HEAD_DIM] / jnp.maximum(acc[HEAD_DIM:HEAD_DIM + 1], 1e-30)
    return jnp.concatenate([o_t[:, r * LANES:(r + 1) * LANES].T for r in range(HPG)], axis=1)


def _online_update(x, vt, m, acc):
    m_new = jnp.maximum(m, jnp.max(x, axis=0, keepdims=True))
    p = jnp.exp2(x - m_new)
    acc = jnp.exp2(m - m_new) * acc + _dot(_with_ones_rows(vt), p.astype(BF16))
    return m_new, acc


def _banded_kernel(*refs, nd, has_sink, mixed):
    refs = list(refs)
    qt_ref, k_ref, vt_ref, tbl_ref, z_ref = refs[:5]
    rest = refs[5:]
    sink_ref = rest.pop(0) if has_sink else None
    oc_ref, os_ref, gl_ref = (rest.pop(0), rest.pop(0), rest.pop(0)) if mixed else (None, None, None)
    (o_ref,) = rest
    for u in range(TILES_PER_STEP):
        i = pl.program_id(2) * TILES_PER_STEP + u
        tile = slice(u * LANES, (u + 1) * LANES)
        ks, vts, tbls = [], [], []
        for d in range(nd):
            kb = i - d
            start = pl.multiple_of(jnp.maximum(kb, 0) * LANES, LANES)
            ks.append(k_ref[pl.ds(start, LANES), :])
            vts.append(vt_ref[:, pl.ds(start, LANES)])
            tbls.append(_group_table(tbl_ref, jnp.where(kb >= 0, d, nd)))
        x = _dot(jnp.concatenate(ks, axis=0), _group_queries(qt_ref[:, tile])) + jnp.concatenate(tbls, axis=0)
        m = jnp.max(x, axis=0, keepdims=True)
        if has_sink:
            m = jnp.maximum(m, sink_ref[...])
        p = jnp.exp2(x - m)
        acc = _dot(_with_ones_rows(jnp.concatenate(vts, axis=1)), p.astype(BF16))
        if has_sink:
            rows = lax.broadcasted_iota(jnp.int32, acc.shape, 0)
            acc = acc + jnp.where(rows >= HEAD_DIM, jnp.exp2(sink_ref[...] - m), 0.0)
        o = _finish_group(acc)
        if mixed:
            gates = 1.0 / (1.0 + jnp.exp(-gl_ref[tile, :]))
            oc = oc_ref[tile, :].astype(F32)
            osel = os_ref[tile, :].astype(F32)
            o = jnp.concatenate(
                [gates[:, 3 * r:3 * r + 1] * oc[:, _head(r)] + gates[:, 3 * r + 1:3 * r + 2] * osel[:, _head(r)]
                 + gates[:, 3 * r + 2:3 * r + 3] * o[:, _head(r)] for r in range(HPG)], axis=1)
        o_ref[tile, :] = (o * _silu(z_ref[tile, :].astype(F32))).astype(o_ref.dtype)


def _banded_gqa(h, qvt, batch, seq, kblk, vtblk, zblk, tbl, sinks=None, mix=None, name="banded"):
    step = TILES_PER_STEP * LANES
    nq = seq // step
    nd = tbl.shape[1] - 1
    rows_of_group = lambda blk: pl.BlockSpec((step, GROUP_W), lambda b, g, i: (b * nq + i, blk + g))
    in_specs = [pl.BlockSpec((GROUP_W, step), lambda b, g, i: (g, b * nq + i)),
                pl.BlockSpec((seq, HEAD_DIM), lambda b, g, i: (b, kblk + g)),
                pl.BlockSpec((HEAD_DIM, seq), lambda b, g, i: (vtblk + g, b)),
                pl.BlockSpec((HPG, nd + 1, LANES, LANES), lambda b, g, i: (g, 0, 0, 0)),
                rows_of_group(zblk)]
    args = [qvt, h, qvt, tbl, h]
    if sinks is not None:
        sink_row = jnp.repeat(sinks.astype(F32).reshape(N_KV, HPG) * LOG2E, LANES, axis=1)
        in_specs.append(pl.BlockSpec((None, 1, HPG * LANES), lambda b, g, i: (g, 0, 0)))
        args.append(sink_row.reshape(N_KV, 1, HPG * LANES))
    if mix is not None:
        in_specs += [rows_of_group(0), rows_of_group(0),
                     pl.BlockSpec((step, LANES), lambda b, g, i: (b * nq + i, g))]
        args += list(mix)
    return pl.pallas_call(
        functools.partial(_banded_kernel, nd=nd, has_sink=sinks is not None, mixed=mix is not None),
        grid=(batch, N_KV, nq),
        in_specs=in_specs,
        out_specs=pl.BlockSpec((step, GROUP_W), lambda b, g, i: (b * nq + i, g)),
        out_shape=jax.ShapeDtypeStruct((batch * seq, WIDTH), BF16),
        compiler_params=_cparams("arbitrary", "arbitrary", "arbitrary"),
        name=name,
    )(*args)


def _compress_kernel(kc_ref, vc_ref, pek_ref, w1k_ref, w2k_ref, pev_ref, w1v_ref, w2v_ref,
                     ko_ref, vo_ref, xs_ref, *, nc):
    half = CMP_LEN // 2
    for x_ref, pe_ref, w1_ref, w2_ref, o_ref in ((kc_ref, pek_ref, w1k_ref, w2k_ref, ko_ref),
                                                 (vc_ref, pev_ref, w1v_ref, w2v_ref, vo_ref)):
        xs_ref[...] = x_ref[...].astype(F32)
        first = jnp.zeros((nc, HEAD_DIM), F32)
        second = jnp.zeros((nc, HEAD_DIM), F32)
        for l in range(half):
            xl = xs_ref[pl.ds(l, nc, stride=CMP_STRIDE), :]
            first += _dot((xl + pe_ref[l:l + 1, :]).astype(BF16), w1_ref[l])
            second += _dot((xl + pe_ref[half + l:half + l + 1, :]).astype(BF16), w1_ref[half + l])
        hid = _silu(first + pltpu.roll(second, nc - 1, 0))
        o_ref[...] = _dot(hid.astype(BF16), w2_ref[...]).astype(o_ref.dtype)


def _nsa_compress(h, batch, seq, kcblk, vcblk, pe_k, w1_k, w2_k, pe_v, w1_v, w2_v):
    assert CMP_LEN == 2 * CMP_STRIDE
    nc = seq // CMP_STRIDE
    kv_spec = lambda blk: pl.BlockSpec((seq, HEAD_DIM), lambda b, g: (b, blk + g))
    full = lambda shape: pl.BlockSpec(shape, lambda b, g: (0,) * len(shape))
    out_spec = pl.BlockSpec((None, None, nc, HEAD_DIM), lambda b, g: (b, g, 0, 0))
    out_shape = jax.ShapeDtypeStruct((batch, N_KV, nc, HEAD_DIM), BF16)
    return pl.pallas_call(
        functools.partial(_compress_kernel, nc=nc),
        grid=(batch, N_KV),
        in_specs=[kv_spec(kcblk), kv_spec(vcblk),
                  full((CMP_LEN, HEAD_DIM)), full((CMP_LEN, HEAD_DIM, HEAD_DIM)), full((HEAD_DIM, HEAD_DIM)),
                  full((CMP_LEN, HEAD_DIM)), full((CMP_LEN, HEAD_DIM, HEAD_DIM)), full((HEAD_DIM, HEAD_DIM))],
        out_specs=[out_spec, out_spec],
        out_shape=[out_shape, out_shape],
        scratch_shapes=[pltpu.VMEM((seq, HEAD_DIM), F32)],
        compiler_params=_cparams("arbitrary", "arbitrary"),
        name="nsa_compress",
    )(h, h, pe_k, w1_k.astype(BF16), w2_k.astype(BF16), pe_v, w1_v.astype(BF16), w2_v.astype(BF16))


def _cmp_table(rel_bias, nc):
    u = np.arange(2 * nc)[:, None] - nc
    dist = np.arange(LANES)[None, :] - (u * CMP_STRIDE + CMP_LEN - 1)
    return _rel_table(rel_bias, dist, dist >= 0)


def _inter_t(seq, nc):
    nb = seq // SEL_LEN
    cstart = np.arange(nc) * CMP_STRIDE
    sstart = np.arange(nb) * SEL_LEN
    inter = np.clip(np.minimum(cstart[None, :] + CMP_LEN, sstart[:, None] + SEL_LEN)
                    - np.maximum(cstart[None, :], sstart[:, None]), 0, None) / CMP_LEN
    return inter.astype(np.float32)


def _cmp_select_kernel(qt_ref, kc_ref, vc_ref, tbl_ref, inter_ref, o_ref, sel_ref, *, nb, nc):
    first = pl.program_id(1) * TILES_PER_STEP
    width = TILES_PER_STEP * LANES
    kc = kc_ref[...]
    vc = vc_ref[...]
    per_tile = LANES // CMP_STRIDE
    row0 = [pl.multiple_of(nc - (first + u) * per_tile, per_tile) for u in range(TILES_PER_STEP)]
    outs = []
    p_sum_t = None
    for r in range(HPG):
        tbl = jnp.concatenate([tbl_ref[r, pl.ds(row, nc), :] for row in row0], axis=1)
        st = _dot(kc, qt_ref[r * HEAD_DIM:(r + 1) * HEAD_DIM, :]) + tbl
        pt = jnp.where(tbl > 0.5 * NEG_INF, jnp.exp2(st - jnp.max(st, axis=0, keepdims=True)), 0.0)
        pt = pt / jnp.maximum(jnp.sum(pt, axis=0, keepdims=True), 1e-30)
        p_sum_t = pt if p_sum_t is None else p_sum_t + pt
        outs.append(_dot(pt.T.astype(BF16), vc))
    o_ref[...] = jnp.concatenate(outs, axis=1).astype(o_ref.dtype)

    inter = inter_ref[...]
    imp = sum(_dot(inter, part) for part in _split3(p_sum_t))
    blk = lax.broadcasted_iota(jnp.int32, (nb, width), 0)
    pos = first * LANES + lax.broadcasted_iota(jnp.int32, (nb, width), 1)
    cur = lax.shift_right_logical(pos, int(math.log2(SEL_LEN)))
    allowed = blk * SEL_LEN <= pos
    forced = (blk == 0) | (blk == cur) | (blk == cur - 1)
    imp = jnp.where(allowed, jnp.where(forced, FORCED_SCORE, imp), NEG_INF)
    rank = jnp.zeros((nb, width), F32)
    for c in range(nb):
        row = imp[c:c + 1, :]
        earlier = jnp.where(blk > c, 1.0, 0.0)
        rank += jnp.where(row > imp, 1.0, jnp.where(row == imp, earlier, 0.0))
    sel_t = jnp.where((rank < SEL_TOPK) & (imp > 0.5 * NEG_INF), 1.0, 0.0)
    sel_t = jnp.concatenate([sel_t, jnp.zeros((LANES - nb, width), F32)], axis=0).astype(sel_ref.dtype)
    for u in range(TILES_PER_STEP):
        sel_ref[u] = sel_t[:, u * LANES:(u + 1) * LANES]


def _nsa_cmp_select(qvt, batch, seq, k_cmp, v_cmp, rel_bias):
    step = TILES_PER_STEP * LANES
    nq = seq // step
    nc = seq // CMP_STRIDE
    nb = seq // SEL_LEN
    assert nc == LANES and nb <= LANES
    tbl = _cmp_table(rel_bias, nc)
    inter = jnp.asarray(_inter_t(seq, nc), BF16)
    cmp_spec = pl.BlockSpec((None, None, nc, HEAD_DIM), lambda g, i, b: (b, g, 0, 0))
    return pl.pallas_call(
        functools.partial(_cmp_select_kernel, nb=nb, nc=nc),
        grid=(N_KV, nq, batch),
        in_specs=[pl.BlockSpec((GROUP_W, step), lambda g, i, b: (g, b * nq + i)),
                  cmp_spec, cmp_spec,
                  pl.BlockSpec((HPG, 2 * nc, LANES), lambda g, i, b: (g, 0, 0)),
                  pl.BlockSpec((nb, nc), lambda g, i, b: (0, 0))],
        out_specs=[pl.BlockSpec((step, GROUP_W), lambda g, i, b: (b * nq + i, g)),
                   pl.BlockSpec((None, None, TILES_PER_STEP, LANES, LANES), lambda g, i, b: (b, g, i, 0, 0))],
        out_shape=[jax.ShapeDtypeStruct((batch * seq, WIDTH), BF16),
                   jax.ShapeDtypeStruct((batch, N_KV, seq // LANES, LANES, LANES), BF16)],
        compiler_params=_cparams("arbitrary", "arbitrary", "arbitrary"),
        name="nsa_cmp_select",
    )(qvt, k_cmp, v_cmp, tbl, inter)


SEL_TILES = 2
SEL_KEYS = SEL_TILES * LANES


def _expand_np(seq):
    j = np.arange(LANES)[None, None, :]
    key = np.arange(seq // SEL_KEYS)[:, None, None] * SEL_KEYS + np.arange(SEL_KEYS)[None, :, None]
    return (j == key // SEL_LEN).astype(np.float32)


def _sel_attn_kernel(qt_ref, k_ref, vt_ref, sel_ref, tbl_ref, exp_ref, o_ref):
    last = pl.program_id(2)
    lanes = HPG * LANES
    beyond = tbl_ref.shape[1] - 1
    far = beyond - 1
    tiles = range(SEL_TILES)
    qt_aug = []
    for u in tiles:
        blocked = ((1.0 - sel_ref[u].astype(F32)) * NEG_INF).astype(BF16)
        qt_aug.append(jnp.concatenate([_group_queries(qt_ref[:, u * LANES:(u + 1) * LANES]),
                                       jnp.concatenate([blocked] * HPG, axis=1)], axis=0))

    def scores(u, kt):
        start = pl.multiple_of(kt * SEL_KEYS, SEL_KEYS)
        k_aug = jnp.concatenate([k_ref[pl.ds(start, SEL_KEYS), :], exp_ref[kt]], axis=1)
        return _dot(k_aug, qt_aug[u])

    def body(kt, carry):
        start = pl.multiple_of(kt * SEL_KEYS, SEL_KEYS)
        out = []
        for u in tiles:
            s, m, acc = carry[u]
            s_next = scores(u, jnp.minimum(kt + 1, last))
            behind = [(last - kt) * SEL_TILES + u - c for c in tiles]
            x = s + jnp.concatenate(
                [_group_table(tbl_ref, jnp.where(d >= 0, jnp.minimum(d, far), beyond)) for d in behind], axis=0)
            out.append((s_next,) + _online_update(x, vt_ref[:, pl.ds(start, SEL_KEYS)], m, acc))
        return tuple(out)

    init = tuple((scores(u, 0), jnp.full((1, lanes), M_INIT, F32),
                  jnp.zeros((HEAD_DIM + BF16_ROWS, lanes), F32)) for u in tiles)
    final = lax.fori_loop(0, last + 1, body, init)
    for u in tiles:
        o_ref[u * LANES:(u + 1) * LANES, :] = _finish_group(final[u][2]).astype(o_ref.dtype)


def _nsa_sel_attn(h, qvt, batch, seq, kblk, vtblk, sel, rel_bias):
    nq = seq // SEL_KEYS
    tbl = _band_tables(rel_bias, 3, None, True)
    expand = jnp.asarray(_expand_np(seq), BF16)
    return pl.pallas_call(
        _sel_attn_kernel,
        grid=(batch, N_KV, nq),
        in_specs=[pl.BlockSpec((GROUP_W, SEL_KEYS), lambda b, g, i: (g, b * nq + i)),
                  pl.BlockSpec((seq, HEAD_DIM), lambda b, g, i: (b, kblk + g)),
                  pl.BlockSpec((HEAD_DIM, seq), lambda b, g, i: (vtblk + g, b)),
                  pl.BlockSpec((None, None, SEL_TILES, LANES, LANES), lambda b, g, i: (b, g, i, 0, 0)),
                  pl.BlockSpec((HPG, 4, LANES, LANES), lambda b, g, i: (g, 0, 0, 0)),
                  pl.BlockSpec((seq // SEL_KEYS, SEL_KEYS, LANES), lambda b, g, i: (0, 0, 0))],
        out_specs=pl.BlockSpec((SEL_KEYS, GROUP_W), lambda b, g, i: (b * nq + i, g)),
        out_shape=jax.ShapeDtypeStruct((batch * seq, WIDTH), BF16),
        compiler_params=_cparams("arbitrary", "arbitrary", "arbitrary"),
        name="nsa_sel_attn",
    )(qvt, h, qvt, sel, tbl, expand)


HEADS_PER_STEP = 8
STEP_W = HEADS_PER_STEP * HEAD_DIM


def _pipelined_heads(i, scores, consume, init):
    heads = range(HEADS_PER_STEP)
    ahead = [scores(j, jnp.maximum(i - 1, 0)) for j in heads]
    state = [consume(j, scores(j, i), i, *init, True) for j in heads]

    def body(step, carry):
        kb = i - step
        return tuple((scores(j, jnp.maximum(kb - 1, 0)),) + consume(j, carry[j][0], kb, *carry[j][1:], False)
                     for j in heads)

    carry = lax.fori_loop(1, i + 1, body, tuple((ahead[j],) + state[j] for j in heads))
    return [c[1:] for c in carry]


SB_HEADS = 8
SB_SPAN = 2
SB_W = SB_HEADS * HEAD_DIM


def _sb_kernel(q_ref, k_ref, v_ref, z_ref, o_ref, *, t):
    i = pl.program_id(2)
    tq = SB_SPAN * t
    qry = lax.broadcasted_iota(jnp.int32, (tq, t), 0)
    key = lax.broadcasted_iota(jnp.int32, (tq, t), 1)
    row = lax.broadcasted_iota(jnp.int32, (t, t), 0)
    col = lax.broadcasted_iota(jnp.int32, (t, t), 1)
    later = jnp.where(row >= col, 1.0, 0.0).astype(BF16)
    heads = range(SB_HEADS)

    def scores(j, kb):
        start = pl.multiple_of(kb * t, t)
        return _dot_nt(q_ref[:, _head(j)], k_ref[pl.ds(start, t), _head(j)])

    def consume(j, y, kb, run, acc, strict=None):
        start = pl.multiple_of(kb * t, t)
        neg = -y
        soft = jnp.log(1.0 + jnp.exp2(jnp.minimum(y, neg))) * LOG2E
        log_keep = jnp.minimum(neg, 0.0) - soft
        if strict is not None:
            log_keep = jnp.where(strict, log_keep, 0.0)
        keep = _dot(log_keep.astype(BF16), later)
        a = jnp.exp2(y + keep + run)
        if strict is not None:
            a = jnp.where(strict, a, 0.0)
        acc = acc + _dot(a.astype(BF16), v_ref[pl.ds(start, t), _head(j)])
        return run + keep[:, 0:1], acc

    top = SB_SPAN * i + SB_SPAN - 1
    state = [(scores(j, top), jnp.zeros((tq, 1), F32), jnp.zeros((tq, HEAD_DIM), F32)) for j in heads]
    for c in reversed(range(SB_SPAN)):
        kb = SB_SPAN * i + c
        strict = key + c * t < qry
        state = [(scores(j, jnp.maximum(kb - 1, 0)),) + consume(j, state[j][0], kb, *state[j][1:], strict)
                 for j in heads]

    def body(step, carry):
        kb = SB_SPAN * i - step
        return tuple((scores(j, jnp.maximum(kb - 1, 0)),) + consume(j, carry[j][0], kb, *carry[j][1:])
                     for j in heads)

    final = lax.fori_loop(1, SB_SPAN * i + 1, body, tuple(state))
    o = jnp.concatenate([acc for _, _, acc in final], axis=1)
    o_ref[...] = (o * _silu(z_ref[...].astype(F32))).astype(o_ref.dtype)


def _sb_call(h, batch, seq, t):
    tq = SB_SPAN * t
    nq = seq // tq
    per_w = WIDTH // SB_W
    rows = lambda part: pl.BlockSpec((tq, SB_W), lambda b, hp, i: (b * nq + i, part * per_w + hp))
    whole = lambda part: pl.BlockSpec((seq, SB_W), lambda b, hp, i: (b, part * per_w + hp))
    return pl.pallas_call(
        functools.partial(_sb_kernel, t=t),
        grid=(batch, per_w, nq),
        in_specs=[rows(0), whole(1), whole(2), rows(3)],
        out_specs=rows(0),
        out_shape=jax.ShapeDtypeStruct((batch * seq, WIDTH), BF16),
        compiler_params=_cparams("arbitrary", "arbitrary", "arbitrary"),
        name="stick_breaking",
    )(h, h, h, h)


def _full_attn_call(kernel, h, qvt, batch, seq, t, extra_specs, extra_args, name):
    nq = seq // t
    per_w = WIDTH // STEP_W
    in_specs = [pl.BlockSpec((STEP_W, t), lambda b, hp, i: (hp, b * nq + i)),
                pl.BlockSpec((seq, STEP_W), lambda b, hp, i: (b, hp)),
                pl.BlockSpec((STEP_W, seq), lambda b, hp, i: (per_w + hp, b)),
                pl.BlockSpec((t, STEP_W), lambda b, hp, i: (b * nq + i, per_w + hp))]
    return pl.pallas_call(
        functools.partial(kernel, t=t),
        grid=(batch, per_w, nq),
        in_specs=in_specs + extra_specs,
        out_specs=pl.BlockSpec((t, STEP_W), lambda b, hp, i: (b * nq + i, hp)),
        out_shape=jax.ShapeDtypeStruct((batch * seq, WIDTH), BF16),
        compiler_params=_cparams("arbitrary", "arbitrary", "arbitrary"),
        name=name,
    )(qvt, h, qvt, h, *extra_args)


KEY_PARTS = 3


def _fox_prep_kernel(fl_ref, bias_ref, ct_ref, ck_ref, *, seq):
    row = lax.broadcasted_iota(jnp.int32, (LANES, LANES), 0)
    col = lax.broadcasted_iota(jnp.int32, (LANES, LANES), 1)
    upto = jnp.where(row <= col, 1.0, 0.0).astype(BF16)
    downto = jnp.where(col <= row, 1.0, 0.0).astype(BF16)
    place = [jnp.where((col == row + j * N_HEADS) & (row < N_HEADS), 1.0, 0.0).astype(BF16)
             for j in range(KEY_PARTS)]
    carry_t = jnp.zeros((LANES, 1), F32)
    carry = jnp.zeros((1, LANES), F32)
    for blk in range(seq // LANES):
        x = fl_ref[blk * LANES:(blk + 1) * LANES, :] + bias_ref[...]
        log_f = jnp.minimum(x, 0.0) - jnp.log1p(jnp.exp(-jnp.abs(x)))
        cs_t = sum(_dot(part, upto) for part in _split3(log_f.T)) + carry_t
        ct_ref[:, blk * LANES:(blk + 1) * LANES] = cs_t * LOG2E
        carry_t = cs_t[:, LANES - 1:LANES]
        cs = sum(_dot(downto, part) for part in _split3(log_f)) + carry
        carry = cs[LANES - 1:LANES, :]
        parts = _split3(-(cs * LOG2E))
        ck_ref[blk * LANES:(blk + 1) * LANES, :] = sum(
            _dot(part, sel) for part, sel in zip(parts, place)).astype(ck_ref.dtype)


def _fox_prep(fl, fgate_bias, batch, seq):
    assert KEY_PARTS * N_HEADS <= LANES
    bias = jnp.zeros((1, LANES), F32).at[0, :N_HEADS].set(fgate_bias.astype(F32))
    return pl.pallas_call(
        functools.partial(_fox_prep_kernel, seq=seq),
        grid=(batch,),
        in_specs=[pl.BlockSpec((seq, LANES), lambda b: (b, 0)),
                  pl.BlockSpec((1, LANES), lambda b: (0, 0))],
        out_specs=[pl.BlockSpec((None, LANES, seq), lambda b: (b, 0, 0)),
                   pl.BlockSpec((seq, LANES), lambda b: (b, 0))],
        out_shape=[jax.ShapeDtypeStruct((batch, LANES, seq), F32),
                   jax.ShapeDtypeStruct((batch * seq, LANES), BF16)],
        compiler_params=_cparams("arbitrary"),
        name="fox_prep",
    )(fl, bias)


def _fox_kernel(qt_ref, k_ref, vt_ref, z_ref, crow_ref, ck_ref, o_ref, *, t):
    first_head = pl.program_id(1) * HEADS_PER_STEP
    i = pl.program_id(2)
    key = lax.broadcasted_iota(jnp.int32, (t, t), 0)
    qry = lax.broadcasted_iota(jnp.int32, (t, t), 1)
    causal = key <= qry
    row = lax.broadcasted_iota(jnp.int32, (LANES, t), 0)
    qt_aug = [jnp.concatenate(
        [qt_ref[_head(j), :],
         jnp.where((row % N_HEADS == first_head + j) & (row < KEY_PARTS * N_HEADS), 1.0, 0.0).astype(BF16)], axis=0)
        for j in range(HEADS_PER_STEP)]

    def scores(j, kb):
        start = pl.multiple_of(kb * t, t)
        k_aug = jnp.concatenate([k_ref[pl.ds(start, t), _head(j)], ck_ref[pl.ds(start, t), :]], axis=1)
        return _dot(k_aug, qt_aug[j])

    def consume(j, x, kb, m, acc, diag):
        start = pl.multiple_of(kb * t, t)
        c_q = crow_ref[pl.ds((first_head + j) % F32_ROWS, 1), pl.ds(pl.multiple_of(i * t, t), t)]
        if diag:
            x = jnp.where(causal, x, NEG_INF)
        m_new = jnp.maximum(m, jnp.max(x, axis=0, keepdims=True) + c_q)
        p = jnp.exp2(x + (c_q - m_new))
        vt = _with_ones_rows(vt_ref[_head(j), pl.ds(start, t)])
        return m_new, jnp.exp2(m - m_new) * acc + _dot(vt, p.astype(BF16))

    final = _pipelined_heads(i, scores, consume,
                             (jnp.full((1, t), M_INIT, F32), jnp.zeros((HEAD_DIM + BF16_ROWS, t), F32)))
    o = jnp.concatenate([(acc[:HEAD_DIM] / jnp.maximum(acc[HEAD_DIM:HEAD_DIM + 1], 1e-30)).T for _, acc in final],
                        axis=1)
    o_ref[...] = (o * _silu(z_ref[...].astype(F32))).astype(o_ref.dtype)


class _Weight:
    def __init__(self, w):
        self.features_first = w.shape[1] % LANES != 0
        self.src = w.T if self.features_first else w

    def pick(self, *ranges, scale_first=None, group_pad=None):
        axis = 0 if self.features_first else 1
        parts = [lax.slice_in_dim(self.src, a, b, axis=axis) for a, b in ranges]
        if scale_first is not None:
            parts[0] = parts[0] * scale_first
        out = jnp.concatenate(parts, axis=axis)
        if group_pad is not None:
            groups, width = group_pad
            out = jnp.moveaxis(out, axis, 0)
            out = out.reshape(groups, -1, out.shape[1])
            out = jnp.pad(out, ((0, 0), (0, width - out.shape[1]), (0, 0))).reshape(groups * width, -1)
            out = jnp.moveaxis(out, 0, axis)
        return out.astype(BF16)

    def project(self, xb, operand, name, out_dtype=BF16, transposed_out=False):
        return _projection(xb, operand, out_dtype, name, features_first=self.features_first,
                           transposed_out=transposed_out)


def _layer_nsa(xb, batch, seq, rel_bias, w_in, cmp_pe_k, cmp_w1_k, cmp_w2_k, cmp_pe_v, cmp_w1_v, cmp_w2_v):
    kc0, ks0, vs0, kw0, vw0, gate0 = (WIDTH + j * KV_WIDTH for j in (0, 2, 3, 4, 5, 6))
    gate1 = gate0 + 3 * N_HEADS
    w = _Weight(w_in)
    h = w.project(xb, w.pick((kc0, vs0), (kw0, vw0), (gate1, w_in.shape[1])), "proj_nsa")
    qvt = w.project(xb, w.pick((0, WIDTH), (vs0, kw0), (vw0, gate0), scale_first=QSCALE), "proj_nsa_t",
                    transposed_out=True)
    gl = w.project(xb, w.pick((gate0, gate1), group_pad=(N_KV, LANES)), "proj_nsa_gates", out_dtype=F32)
    blk = lambda j: j * N_KV
    vrow = WIDTH // HEAD_DIM
    k_cmp, v_cmp = _nsa_compress(h, batch, seq, blk(0), blk(1), cmp_pe_k, cmp_w1_k, cmp_w2_k,
                                 cmp_pe_v, cmp_w1_v, cmp_w2_v)
    o_cmp, sel = _nsa_cmp_select(qvt, batch, seq, k_cmp, v_cmp, rel_bias)
    o_sel = _nsa_sel_attn(h, qvt, batch, seq, blk(2), vrow, sel, rel_bias)
    nd = NSA_WINDOW // LANES + 1
    return _banded_gqa(h, qvt, batch, seq, blk(3), vrow + N_KV, 4 * KV_WIDTH // GROUP_W,
                       _band_tables(rel_bias, nd, NSA_WINDOW, True), mix=(o_cmp, o_sel, gl), name="nsa_window_mix")


def _layer_swa(xb, batch, seq, rel_bias, w_in, sinks):
    v0, z0 = WIDTH + KV_WIDTH, WIDTH + 2 * KV_WIDTH
    w = _Weight(w_in)
    h = w.project(xb, w.pick((WIDTH, v0), (z0, w_in.shape[1])), "proj_swa")
    qvt = w.project(xb, w.pick((0, WIDTH), (v0, z0), scale_first=QSCALE), "proj_swa_t",
                    transposed_out=True)
    nd = SWA_WINDOW // LANES + 1
    return _banded_gqa(h, qvt, batch, seq, 0, WIDTH // HEAD_DIM, KV_WIDTH // GROUP_W,
                       _band_tables(rel_bias, nd, SWA_WINDOW, True), sinks=sinks, name="swa")


def _layer_sb(xb, batch, seq, w_in, t=256):
    w = _Weight(w_in)
    h = w.project(xb, w.pick((0, WIDTH), (WIDTH, w_in.shape[1]), scale_first=QSCALE), "proj_sb")
    return _sb_call(h, batch, seq, t)


def _layer_fox(xb, batch, seq, w_in, fgate_bias, t=256):
    f0 = 3 * WIDTH
    f1 = f0 + N_HEADS
    w = _Weight(w_in)
    h = w.project(xb, w.pick((WIDTH, 2 * WIDTH), (f1, w_in.shape[1])), "proj_fox")
    qvt = w.project(xb, w.pick((0, WIDTH), (2 * WIDTH, f0), scale_first=QSCALE), "proj_fox_t",
                    transposed_out=True)
    fl = w.project(xb, w.pick((f0, f1), group_pad=(1, LANES)), "proj_fox_gates", out_dtype=F32)
    ct, ck = _fox_prep(fl, fgate_bias, batch, seq)
    assert F32_ROWS % HEADS_PER_STEP == 0
    extra_specs = [pl.BlockSpec((None, F32_ROWS, seq), lambda b, hp, i: (b, hp * HEADS_PER_STEP // F32_ROWS, 0)),
                   pl.BlockSpec((seq, LANES), lambda b, hp, i: (b, 0))]
    return _full_attn_call(_fox_kernel, h, qvt, batch, seq, t, extra_specs, [ct, ck], "forgetting")


def kernel(x, rel_bias, w_in_a, w_out_a, ln_g_a, ln_b_a, cmp_pe_k, cmp_w1_k, cmp_w2_k, cmp_pe_v, cmp_w1_v, cmp_w2_v, w_in_b, w_out_b, ln_g_b, ln_b_b, sinks_b, w_in_c, w_out_c, ln_g_c, ln_b_c, w_in_d, w_out_d, ln_g_d, ln_b_d, fgate_bias_d):
    batch, seq, d_model = x.shape
    xf = x.reshape(batch * seq, d_model)
    xb = xf.astype(BF16)
    og = _layer_nsa(xb, batch, seq, rel_bias, w_in_a, cmp_pe_k, cmp_w1_k, cmp_w2_k, cmp_pe_v, cmp_w1_v, cmp_w2_v)
    xf, xb = _out_proj_ln(og, w_out_a, xf, ln_g_a, ln_b_a, True)
    og = _layer_swa(xb, batch, seq, rel_bias, w_in_b, sinks_b)
    xf, xb = _out_proj_ln(og, w_out_b, xf, ln_g_b, ln_b_b, True)
    og = _layer_sb(xb, batch, seq, w_in_c)
    xf, xb = _out_proj_ln(og, w_out_c, xf, ln_g_c, ln_b_c, True)
    og = _layer_fox(xb, batch, seq, w_in_d, fgate_bias_d)
    (xf,) = _out_proj_ln(og, w_out_d, xf, ln_g_d, ln_b_d, False)
    return xf.reshape(batch, seq, d_model)
```

```python
import functools
import math

import numpy as np
import jax
import jax.numpy as jnp
from jax import lax
from jax.experimental import pallas as pl
from jax.experimental.pallas import tpu as pltpu

F32 = jnp.float32
BF16 = jnp.bfloat16

N_HEADS = 16
HEAD_DIM = 128
N_KV = 4
HPG = N_HEADS // N_KV
WIDTH = N_HEADS * HEAD_DIM
KV_WIDTH = N_KV * HEAD_DIM
GROUP_W = HPG * HEAD_DIM
REL_BUCKETS = 32
REL_MAX_DIST = 128
CMP_LEN = 32
CMP_STRIDE = 16
SEL_LEN = 64
SEL_TOPK = 8
NSA_WINDOW = 512
SWA_WINDOW = 128
DEPTH = 4
DN_ALPHA = (2 * DEPTH) ** 0.25
LN_EPS = 1e-5
NEG_INF = -1e30
FORCED_SCORE = 1e4
SCALE = HEAD_DIM ** -0.5
LOG2E = math.log2(math.e)
QSCALE = SCALE * LOG2E

LANES = 128
BF16_ROWS = 16
F32_ROWS = 8
LN_ROWS = 128
TILES_PER_STEP = 16
M_INIT = -1e29
VMEM_LIMIT = 56 * 1024 * 1024


def _cparams(*sem):
    return pltpu.CompilerParams(dimension_semantics=sem, vmem_limit_bytes=VMEM_LIMIT)


def _silu(z):
    return z / (1.0 + jnp.exp(-z))


def _dot_nt(a, b):
    return lax.dot_general(a, b, (((1,), (1,)), ((), ())), preferred_element_type=F32)


def _dot(a, b):
    return jnp.dot(a, b, preferred_element_type=F32)


def _head(j):
    return slice(j * HEAD_DIM, (j + 1) * HEAD_DIM)


def _group_queries(qt):
    return jnp.concatenate([qt[r * HEAD_DIM:(r + 1) * HEAD_DIM] for r in range(HPG)], axis=1)


def _split3(x):
    a1 = x.astype(BF16)
    r1 = x - a1.astype(F32)
    a2 = r1.astype(BF16)
    a3 = (r1 - a2.astype(F32)).astype(BF16)
    return a1, a2, a3


def _proj_kernel(x_ref, w_ref, o_ref, *, w_dim, transposed_out):
    x, w = x_ref[...], w_ref[...]
    if transposed_out:
        y = lax.dot_general(w, x, (((w_dim,), (1,)), ((), ())), preferred_element_type=F32)
    else:
        y = lax.dot_general(x, w, (((1,), (w_dim,)), ((), ())), preferred_element_type=F32)
    o_ref[...] = y.astype(o_ref.dtype)


def _feature_tile(n):
    return next(t for t in (1024, 1280, 1536, 768, 512, 256, 128) if n % t == 0)


def _projection(x, w, out_dtype, name, *, features_first=False, transposed_out=False, tm=1024):
    m, k = x.shape
    n = w.shape[0] if features_first else w.shape[1]
    tn = _feature_tile(n)
    assert m % tm == 0
    w_spec = (pl.BlockSpec((tn, k), lambda j, i: (j, 0)) if features_first
              else pl.BlockSpec((k, tn), lambda j, i: (0, j)))
    if transposed_out:
        out_spec, out_shape = pl.BlockSpec((tn, tm), lambda j, i: (j, i)), (n, m)
    else:
        out_spec, out_shape = pl.BlockSpec((tm, tn), lambda j, i: (i, j)), (m, n)
    return pl.pallas_call(
        functools.partial(_proj_kernel, w_dim=1 if features_first else 0, transposed_out=transposed_out),
        grid=(n // tn, m // tm),
        in_specs=[pl.BlockSpec((tm, k), lambda j, i: (i, 0)), w_spec],
        out_specs=out_spec,
        out_shape=jax.ShapeDtypeStruct(out_shape, out_dtype),
        compiler_params=_cparams("arbitrary", "arbitrary"),
        name=name,
    )(x, w)


def _outln_kernel(og_ref, w_ref, x_ref, g_ref, b_ref, xo_ref, *maybe_xb_ref):
    for c in range(og_ref.shape[0] // LN_ROWS):
        rows = slice(c * LN_ROWS, (c + 1) * LN_ROWS)
        t = DN_ALPHA * x_ref[rows, :] + _dot(og_ref[rows, :], w_ref[...])
        mu = jnp.mean(t, axis=-1, keepdims=True)
        d = t - mu
        var = jnp.mean(d * d, axis=-1, keepdims=True)
        out = d * lax.rsqrt(var + LN_EPS) * g_ref[...] + b_ref[...]
        xo_ref[rows, :] = out
        for xb_ref in maybe_xb_ref:
            xb_ref[rows, :] = out.astype(BF16)


def _out_proj_ln(og, w_out, x, ln_g, ln_b, with_bf16, tm=512):
    m, k = og.shape
    n = w_out.shape[1]
    rows = pl.BlockSpec((tm, n), lambda i: (i, 0))
    out_dtypes = (F32, BF16) if with_bf16 else (F32,)
    return pl.pallas_call(
        _outln_kernel,
        grid=(m // tm,),
        in_specs=[pl.BlockSpec((tm, k), lambda i: (i, 0)),
                  pl.BlockSpec((k, n), lambda i: (0, 0)),
                  rows,
                  pl.BlockSpec((1, n), lambda i: (0, 0)),
                  pl.BlockSpec((1, n), lambda i: (0, 0))],
        out_specs=[rows] * len(out_dtypes),
        out_shape=[jax.ShapeDtypeStruct((m, n), dt) for dt in out_dtypes],
        compiler_params=_cparams("arbitrary"),
        name="out_proj_ln",
    )(og, w_out.astype(BF16), x, ln_g.reshape(1, n), ln_b.reshape(1, n))


def _bucket_np(dist):
    max_exact = REL_BUCKETS // 2
    ratio = np.maximum(dist, max_exact).astype(np.float32) / max_exact
    large = max_exact + (np.log(ratio) / math.log(REL_MAX_DIST / max_exact)
                         * (REL_BUCKETS - max_exact)).astype(np.int32)
    return np.where(dist < max_exact, dist, np.minimum(large, REL_BUCKETS - 1))


def _rel_table(rel_bias, dist, ok):
    bucket = _bucket_np(np.maximum(dist, 0)).reshape(-1)
    onehot_t = np.zeros((REL_BUCKETS, bucket.size), np.float32)
    onehot_t[bucket, np.arange(bucket.size)] = 1.0
    t = jnp.dot(rel_bias.T.astype(F32) * LOG2E, jnp.asarray(onehot_t), precision=lax.Precision.HIGHEST)
    blocked = np.where(ok.reshape(-1), 0.0, NEG_INF).astype(np.float32)
    return (t + blocked[None, :]).reshape((N_HEADS,) + dist.shape)


def _band_tables(rel_bias, nd, window, with_blocked=False):
    key = np.arange(LANES)[:, None]
    qry = np.arange(LANES)[None, :]
    dist = np.stack([LANES * d + qry - key for d in range(nd)])
    ok = dist >= 0
    if window is not None:
        ok &= dist < window
    if with_blocked:
        dist = np.concatenate([dist, np.zeros((1, LANES, LANES), dist.dtype)])
        ok = np.concatenate([ok, np.zeros((1, LANES, LANES), bool)])
    return _rel_table(rel_bias, dist, ok)


def _group_table(tbl_ref, idx):
    return jnp.concatenate([tbl_ref[r, idx] for r in range(HPG)], axis=1)


def _with_ones_rows(vt):
    return jnp.concatenate([vt, jnp.ones((BF16_ROWS, vt.shape[1]), BF16)], axis=0)


def _finish_group(acc):
    o_t = acc[:HEAD_DIM] / jnp.maximum(acc[HEAD_DIM:HEAD_DIM + 1], 1e-30)
    return jnp.concatenate([o_t[:, r * LANES:(r + 1) * LANES].T for r in range(HPG)], axis=1)


def _online_update(x, vt, m, acc):
    m_new = jnp.maximum(m, jnp.max(x, axis=0, keepdims=True))
    p = jnp.exp2(x - m_new)
    acc = jnp.exp2(m - m_new) * acc + _dot(_with_ones_rows(vt), p.astype(BF16))
    return m_new, acc


def _banded_kernel(*refs, nd, has_sink, mixed):
    refs = list(refs)
    qt_ref, k_ref, vt_ref, tbl_ref, z_ref = refs[:5]
    rest = refs[5:]
    sink_ref = rest.pop(0) if has_sink else None
    oc_ref, os_ref, gl_ref = (rest.pop(0), rest.pop(0), rest.pop(0)) if mixed else (None, None, None)
    (o_ref,) = rest
    for u in range(TILES_PER_STEP):
        i = pl.program_id(2) * TILES_PER_STEP + u
        tile = slice(u * LANES, (u + 1) * LANES)
        ks, vts, tbls = [], [], []
        for d in range(nd):
            kb = i - d
            start = pl.multiple_of(jnp.maximum(kb, 0) * LANES, LANES)
            ks.append(k_ref[pl.ds(start, LANES), :])
            vts.append(vt_ref[:, pl.ds(start, LANES)])
            tbls.append(_group_table(tbl_ref, jnp.where(kb >= 0, d, nd)))
        x = _dot(jnp.concatenate(ks, axis=0), _group_queries(qt_ref[:, tile])) + jnp.concatenate(tbls, axis=0)
        m = jnp.max(x, axis=0, keepdims=True)
        if has_sink:
            m = jnp.maximum(m, sink_ref[...])
        p = jnp.exp2(x - m)
        acc = _dot(_with_ones_rows(jnp.concatenate(vts, axis=1)), p.astype(BF16))
        if has_sink:
            rows = lax.broadcasted_iota(jnp.int32, acc.shape, 0)
            acc = acc + jnp.where(rows >= HEAD_DIM, jnp.exp2(sink_ref[...] - m), 0.0)
        o = _finish_group(acc)
        if mixed:
            gates = 1.0 / (1.0 + jnp.exp(-gl_ref[tile, :]))
            oc = oc_ref[tile, :].astype(F32)
            osel = os_ref[tile, :].astype(F32)
            o = jnp.concatenate(
                [gates[:, 3 * r:3 * r + 1] * oc[:, _head(r)] + gates[:, 3 * r + 1:3 * r + 2] * osel[:, _head(r)]
                 + gates[:, 3 * r + 2:3 * r + 3] * o[:, _head(r)] for r in range(HPG)], axis=1)
        o_ref[tile, :] = (o * _silu(z_ref[tile, :].astype(F32))).astype(o_ref.dtype)


def _banded_gqa(h, qvt, batch, seq, kblk, vtblk, zblk, tbl, sinks=None, mix=None, name="banded"):
    step = TILES_PER_STEP * LANES
    nq = seq // step
    nd = tbl.shape[1] - 1
    rows_of_group = lambda blk: pl.BlockSpec((step, GROUP_W), lambda b, g, i: (b * nq + i, blk + g))
    in_specs = [pl.BlockSpec((GROUP_W, step), lambda b, g, i: (g, b * nq + i)),
                pl.BlockSpec((seq, HEAD_DIM), lambda b, g, i: (b, kblk + g)),
                pl.BlockSpec((HEAD_DIM, seq), lambda b, g, i: (vtblk + g, b)),
                pl.BlockSpec((HPG, nd + 1, LANES, LANES), lambda b, g, i: (g, 0, 0, 0)),
                rows_of_group(zblk)]
    args = [qvt, h, qvt, tbl, h]
    if sinks is not None:
        sink_row = jnp.repeat(sinks.astype(F32).reshape(N_KV, HPG) * LOG2E, LANES, axis=1)
        in_specs.append(pl.BlockSpec((None, 1, HPG * LANES), lambda b, g, i: (g, 0, 0)))
        args.append(sink_row.reshape(N_KV, 1, HPG * LANES))
    if mix is not None:
        in_specs += [rows_of_group(0), rows_of_group(0),
                     pl.BlockSpec((step, LANES), lambda b, g, i: (b * nq + i, g))]
        args += list(mix)
    return pl.pallas_call(
        functools.partial(_banded_kernel, nd=nd, has_sink=sinks is not None, mixed=mix is not None),
        grid=(batch, N_KV, nq),
        in_specs=in_specs,
        out_specs=pl.BlockSpec((step, GROUP_W), lambda b, g, i: (b * nq + i, g)),
        out_shape=jax.ShapeDtypeStruct((batch * seq, WIDTH), BF16),
        compiler_params=_cparams("arbitrary", "arbitrary", "arbitrary"),
        name=name,
    )(*args)


def _compress_kernel(kc_ref, vc_ref, pek_ref, w1k_ref, w2k_ref, pev_ref, w1v_ref, w2v_ref,
                     ko_ref, vo_ref, xs_ref, *, nc):
    half = CMP_LEN // 2
    for x_ref, pe_ref, w1_ref, w2_ref, o_ref in ((kc_ref, pek_ref, w1k_ref, w2k_ref, ko_ref),
                                                 (vc_ref, pev_ref, w1v_ref, w2v_ref, vo_ref)):
        xs_ref[...] = x_ref[...].astype(F32)
        first = jnp.zeros((nc, HEAD_DIM), F32)
        second = jnp.zeros((nc, HEAD_DIM), F32)
        for l in range(half):
            xl = xs_ref[pl.ds(l, nc, stride=CMP_STRIDE), :]
            first += _dot((xl + pe_ref[l:l + 1, :]).astype(BF16), w1_ref[l])
            second += _dot((xl + pe_ref[half + l:half + l + 1, :]).astype(BF16), w1_ref[half + l])
        hid = _silu(first + pltpu.roll(second, nc - 1, 0))
        o_ref[...] = _dot(hid.astype(BF16), w2_ref[...]).astype(o_ref.dtype)


def _nsa_compress(h, batch, seq, kcblk, vcblk, pe_k, w1_k, w2_k, pe_v, w1_v, w2_v):
    assert CMP_LEN == 2 * CMP_STRIDE
    nc = seq // CMP_STRIDE
    kv_spec = lambda blk: pl.BlockSpec((seq, HEAD_DIM), lambda b, g: (b, blk + g))
    full = lambda shape: pl.BlockSpec(shape, lambda b, g: (0,) * len(shape))
    out_spec = pl.BlockSpec((None, None, nc, HEAD_DIM), lambda b, g: (b, g, 0, 0))
    out_shape = jax.ShapeDtypeStruct((batch, N_KV, nc, HEAD_DIM), BF16)
    return pl.pallas_call(
        functools.partial(_compress_kernel, nc=nc),
        grid=(batch, N_KV),
        in_specs=[kv_spec(kcblk), kv_spec(vcblk),
                  full((CMP_LEN, HEAD_DIM)), full((CMP_LEN, HEAD_DIM, HEAD_DIM)), full((HEAD_DIM, HEAD_DIM)),
                  full((CMP_LEN, HEAD_DIM)), full((CMP_LEN, HEAD_DIM, HEAD_DIM)), full((HEAD_DIM, HEAD_DIM))],
        out_specs=[out_spec, out_spec],
        out_shape=[out_shape, out_shape],
        scratch_shapes=[pltpu.VMEM((seq, HEAD_DIM), F32)],
        compiler_params=_cparams("arbitrary", "arbitrary"),
        name="nsa_compress",
    )(h, h, pe_k, w1_k.astype(BF16), w2_k.astype(BF16), pe_v, w1_v.astype(BF16), w2_v.astype(BF16))


def _cmp_table(rel_bias, nc):
    u = np.arange(2 * nc)[:, None] - nc
    dist = np.arange(LANES)[None, :] - (u * CMP_STRIDE + CMP_LEN - 1)
    return _rel_table(rel_bias, dist, dist >= 0)


def _inter_t(seq, nc):
    nb = seq // SEL_LEN
    cstart = np.arange(nc) * CMP_STRIDE
    sstart = np.arange(nb) * SEL_LEN
    inter = np.clip(np.minimum(cstart[None, :] + CMP_LEN, sstart[:, None] + SEL_LEN)
                    - np.maximum(cstart[None, :], sstart[:, None]), 0, None) / CMP_LEN
    return inter.astype(np.float32)


def _cmp_select_kernel(qt_ref, kc_ref, vc_ref, tbl_ref, inter_ref, o_ref, sel_ref, *, nb, nc):
    first = pl.program_id(1) * TILES_PER_STEP
    width = TILES_PER_STEP * LANES
    kc = kc_ref[...]
    vc = vc_ref[...]
    per_tile = LANES // CMP_STRIDE
    row0 = [pl.multiple_of(nc - (first + u) * per_tile, per_tile) for u in range(TILES_PER_STEP)]
    outs = []
    p_sum_t = None
    for r in range(HPG):
        tbl = jnp.concatenate([tbl_ref[r, pl.ds(row, nc), :] for row in row0], axis=1)
        st = _dot(kc, qt_ref[r * HEAD_DIM:(r + 1) * HEAD_DIM, :]) + tbl
        pt = jnp.where(tbl > 0.5 * NEG_INF, jnp.exp2(st - jnp.max(st, axis=0, keepdims=True)), 0.0)
        pt = pt / jnp.maximum(jnp.sum(pt, axis=0, keepdims=True), 1e-30)
        p_sum_t = pt if p_sum_t is None else p_sum_t + pt
        outs.append(_dot(pt.T.astype(BF16), vc))
    o_ref[...] = jnp.concatenate(outs, axis=1).astype(o_ref.dtype)

    inter = inter_ref[...]
    imp = sum(_dot(inter, part) for part in _split3(p_sum_t))
    blk = lax.broadcasted_iota(jnp.int32, (nb, width), 0)
    pos = first * LANES + lax.broadcasted_iota(jnp.int32, (nb, width), 1)
    cur = lax.shift_right_logical(pos, int(math.log2(SEL_LEN)))
    allowed = blk * SEL_LEN <= pos
    forced = (blk == 0) | (blk == cur) | (blk == cur - 1)
    imp = jnp.where(allowed, jnp.where(forced, FORCED_SCORE, imp), NEG_INF)
    rank = jnp.zeros((nb, width), F32)
    for c in range(nb):
        row = imp[c:c + 1, :]
        earlier = jnp.where(blk > c, 1.0, 0.0)
        rank += jnp.where(row > imp, 1.0, jnp.where(row == imp, earlier, 0.0))
    sel_t = jnp.where((rank < SEL_TOPK) & (imp > 0.5 * NEG_INF), 1.0, 0.0)
    sel_t = jnp.concatenate([sel_t, jnp.zeros((LANES - nb, width), F32)], axis=0).astype(sel_ref.dtype)
    for u in range(TILES_PER_STEP):
        sel_ref[u] = sel_t[:, u * LANES:(u + 1) * LANES]


def _nsa_cmp_select(qvt, batch, seq, k_cmp, v_cmp, rel_bias):
    step = TILES_PER_STEP * LANES
    nq = seq // step
    nc = seq // CMP_STRIDE
    nb = seq // SEL_LEN
    assert nc == LANES and nb <= LANES
    tbl = _cmp_table(rel_bias, nc)
    inter = jnp.asarray(_inter_t(seq, nc), BF16)
    cmp_spec = pl.BlockSpec((None, None, nc, HEAD_DIM), lambda g, i, b: (b, g, 0, 0))
    return pl.pallas_call(
        functools.partial(_cmp_select_kernel, nb=nb, nc=nc),
        grid=(N_KV, nq, batch),
        in_specs=[pl.BlockSpec((GROUP_W, step), lambda g, i, b: (g, b * nq + i)),
                  cmp_spec, cmp_spec,
                  pl.BlockSpec((HPG, 2 * nc, LANES), lambda g, i, b: (g, 0, 0)),
                  pl.BlockSpec((nb, nc), lambda g, i, b: (0, 0))],
        out_specs=[pl.BlockSpec((step, GROUP_W), lambda g, i, b: (b * nq + i, g)),
                   pl.BlockSpec((None, None, TILES_PER_STEP, LANES, LANES), lambda g, i, b: (b, g, i, 0, 0))],
        out_shape=[jax.ShapeDtypeStruct((batch * seq, WIDTH), BF16),
                   jax.ShapeDtypeStruct((batch, N_KV, seq // LANES, LANES, LANES), BF16)],
        compiler_params=_cparams("arbitrary", "arbitrary", "arbitrary"),
        name="nsa_cmp_select",
    )(qvt, k_cmp, v_cmp, tbl, inter)


SEL_TILES = 2
SEL_KEYS = SEL_TILES * LANES


def _expand_np(seq):
    j = np.arange(LANES)[None, None, :]
    key = np.arange(seq // SEL_KEYS)[:, None, None] * SEL_KEYS + np.arange(SEL_KEYS)[None, :, None]
    return (j == key // SEL_LEN).astype(np.float32)


def _sel_attn_kernel(qt_ref, k_ref, vt_ref, sel_ref, tbl_ref, exp_ref, o_ref):
    last = pl.program_id(2)
    lanes = HPG * LANES
    beyond = tbl_ref.shape[1] - 1
    far = beyond - 1
    tiles = range(SEL_TILES)
    qt_aug = []
    for u in tiles:
        blocked = ((1.0 - sel_ref[u].astype(F32)) * NEG_INF).astype(BF16)
        qt_aug.append(jnp.concatenate([_group_queries(qt_ref[:, u * LANES:(u + 1) * LANES]),
                                       jnp.concatenate([blocked] * HPG, axis=1)], axis=0))

    def scores(u, kt):
        start = pl.multiple_of(kt * SEL_KEYS, SEL_KEYS)
        k_aug = jnp.concatenate([k_ref[pl.ds(start, SEL_KEYS), :], exp_ref[kt]], axis=1)
        return _dot(k_aug, qt_aug[u])

    def body(kt, carry):
        start = pl.multiple_of(kt * SEL_KEYS, SEL_KEYS)
        out = []
        for u in tiles:
            s, m, acc = carry[u]
            s_next = scores(u, jnp.minimum(kt + 1, last))
            behind = [(last - kt) * SEL_TILES + u - c for c in tiles]
            x = s + jnp.concatenate(
                [_group_table(tbl_ref, jnp.where(d >= 0, jnp.minimum(d, far), beyond)) for d in behind], axis=0)
            out.append((s_next,) + _online_update(x, vt_ref[:, pl.ds(start, SEL_KEYS)], m, acc))
        return tuple(out)

    init = tuple((scores(u, 0), jnp.full((1, lanes), M_INIT, F32),
                  jnp.zeros((HEAD_DIM + BF16_ROWS, lanes), F32)) for u in tiles)
    final = lax.fori_loop(0, last + 1, body, init)
    for u in tiles:
        o_ref[u * LANES:(u + 1) * LANES, :] = _finish_group(final[u][2]).astype(o_ref.dtype)


def _nsa_sel_attn(h, qvt, batch, seq, kblk, vtblk, sel, rel_bias):
    nq = seq // SEL_KEYS
    tbl = _band_tables(rel_bias, 3, None, True)
    expand = jnp.asarray(_expand_np(seq), BF16)
    return pl.pallas_call(
        _sel_attn_kernel,
        grid=(batch, N_KV, nq),
        in_specs=[pl.BlockSpec((GROUP_W, SEL_KEYS), lambda b, g, i: (g, b * nq + i)),
                  pl.BlockSpec((seq, HEAD_DIM), lambda b, g, i: (b, kblk + g)),
                  pl.BlockSpec((HEAD_DIM, seq), lambda b, g, i: (vtblk + g, b)),
                  pl.BlockSpec((None, None, SEL_TILES, LANES, LANES), lambda b, g, i: (b, g, i, 0, 0)),
                  pl.BlockSpec((HPG, 4, LANES, LANES), lambda b, g, i: (g, 0, 0, 0)),
                  pl.BlockSpec((seq // SEL_KEYS, SEL_KEYS, LANES), lambda b, g, i: (0, 0, 0))],
        out_specs=pl.BlockSpec((SEL_KEYS, GROUP_W), lambda b, g, i: (b * nq + i, g)),
        out_shape=jax.ShapeDtypeStruct((batch * seq, WIDTH), BF16),
        compiler_params=_cparams("arbitrary", "arbitrary", "arbitrary"),
        name="nsa_sel_attn",
    )(qvt, h, qvt, sel, tbl, expand)


HEADS_PER_STEP = 8
STEP_W = HEADS_PER_STEP * HEAD_DIM


def _pipelined_heads(i, scores, consume, init):
    heads = range(HEADS_PER_STEP)
    ahead = [scores(j, jnp.maximum(i - 1, 0)) for j in heads]
    state = [consume(j, scores(j, i), i, *init, True) for j in heads]

    def body(step, carry):
        kb = i - step
        return tuple((scores(j, jnp.maximum(kb - 1, 0)),) + consume(j, carry[j][0], kb, *carry[j][1:], False)
                     for j in heads)

    carry = lax.fori_loop(1, i + 1, body, tuple((ahead[j],) + state[j] for j in heads))
    return [c[1:] for c in carry]


SB_HEADS = 8
SB_SPAN = 2
SB_W = SB_HEADS * HEAD_DIM


def _sb_kernel(q_ref, k_ref, v_ref, z_ref, o_ref, *, t):
    i = pl.program_id(2)
    tq = SB_SPAN * t
    qry = lax.broadcasted_iota(jnp.int32, (tq, t), 0)
    key = lax.broadcasted_iota(jnp.int32, (tq, t), 1)
    row = lax.broadcasted_iota(jnp.int32, (t, t), 0)
    col = lax.broadcasted_iota(jnp.int32, (t, t), 1)
    later = jnp.where(row >= col, 1.0, 0.0).astype(BF16)
    heads = range(SB_HEADS)

    def scores(j, kb):
        start = pl.multiple_of(kb * t, t)
        return _dot_nt(q_ref[:, _head(j)], k_ref[pl.ds(start, t), _head(j)])

    def consume(j, y, kb, run, acc, strict=None):
        start = pl.multiple_of(kb * t, t)
        neg = -y
        soft = jnp.log(1.0 + jnp.exp2(jnp.minimum(y, neg))) * LOG2E
        log_keep = jnp.minimum(neg, 0.0) - soft
        if strict is not None:
            log_keep = jnp.where(strict, log_keep, 0.0)
        keep = _dot(log_keep.astype(BF16), later)
        a = jnp.exp2(y + keep + run)
        if strict is not None:
            a = jnp.where(strict, a, 0.0)
        acc = acc + _dot(a.astype(BF16), v_ref[pl.ds(start, t), _head(j)])
        return run + keep[:, 0:1], acc

    top = SB_SPAN * i + SB_SPAN - 1
    state = [(scores(j, top), jnp.zeros((tq, 1), F32), jnp.zeros((tq, HEAD_DIM), F32)) for j in heads]
    for c in reversed(range(SB_SPAN)):
        kb = SB_SPAN * i + c
        strict = key + c * t < qry
        state = [(scores(j, jnp.maximum(kb - 1, 0)),) + consume(j, state[j][0], kb, *state[j][1:], strict)
                 for j in heads]

    def body(step, carry):
        kb = SB_SPAN * i - step
        return tuple((scores(j, jnp.maximum(kb - 1, 0)),) + consume(j, carry[j][0], kb, *carry[j][1:])
                     for j in heads)

    final = lax.fori_loop(1, SB_SPAN * i + 1, body, tuple(state))
    o = jnp.concatenate([acc for _, _, acc in final], axis=1)
    o_ref[...] = (o * _silu(z_ref[...].astype(F32))).astype(o_ref.dtype)


def _sb_call(h, batch, seq, t):
    tq = SB_SPAN * t
    nq = seq // tq
    per_w = WIDTH // SB_W
    rows = lambda part: pl.BlockSpec((tq, SB_W), lambda b, hp, i: (b * nq + i, part * per_w + hp))
    whole = lambda part: pl.BlockSpec((seq, SB_W), lambda b, hp, i: (b, part * per_w + hp))
    return pl.pallas_call(
        functools.partial(_sb_kernel, t=t),
        grid=(batch, per_w, nq),
        in_specs=[rows(0), whole(1), whole(2), rows(3)],
        out_specs=rows(0),
        out_shape=jax.ShapeDtypeStruct((batch * seq, WIDTH), BF16),
        compiler_params=_cparams("arbitrary", "arbitrary", "arbitrary"),
        name="stick_breaking",
    )(h, h, h, h)


def _full_attn_call(kernel, h, qvt, batch, seq, t, extra_specs, extra_args, name):
    nq = seq // t
    per_w = WIDTH // STEP_W
    in_specs = [pl.BlockSpec((STEP_W, t), lambda b, hp, i: (hp, b * nq + i)),
                pl.BlockSpec((seq, STEP_W), lambda b, hp, i: (b, hp)),
                pl.BlockSpec((STEP_W, seq), lambda b, hp, i: (per_w + hp, b)),
                pl.BlockSpec((t, STEP_W), lambda b, hp, i: (b * nq + i, per_w + hp))]
    return pl.pallas_call(
        functools.partial(kernel, t=t),
        grid=(batch, per_w, nq),
        in_specs=in_specs + extra_specs,
        out_specs=pl.BlockSpec((t, STEP_W), lambda b, hp, i: (b * nq + i, hp)),
        out_shape=jax.ShapeDtypeStruct((batch * seq, WIDTH), BF16),
        compiler_params=_cparams("arbitrary", "arbitrary", "arbitrary"),
        name=name,
    )(qvt, h, qvt, h, *extra_args)


KEY_PARTS = 3


def _fox_prep_kernel(fl_ref, bias_ref, ct_ref, ck_ref, *, seq):
    row = lax.broadcasted_iota(jnp.int32, (LANES, LANES), 0)
    col = lax.broadcasted_iota(jnp.int32, (LANES, LANES), 1)
    upto = jnp.where(row <= col, 1.0, 0.0).astype(BF16)
    downto = jnp.where(col <= row, 1.0, 0.0).astype(BF16)
    place = [jnp.where((col == row + j * N_HEADS) & (row < N_HEADS), 1.0, 0.0).astype(BF16)
             for j in range(KEY_PARTS)]
    carry_t = jnp.zeros((LANES, 1), F32)
    carry = jnp.zeros((1, LANES), F32)
    for blk in range(seq // LANES):
        x = fl_ref[blk * LANES:(blk + 1) * LANES, :] + bias_ref[...]
        log_f = jnp.minimum(x, 0.0) - jnp.log1p(jnp.exp(-jnp.abs(x)))
        cs_t = sum(_dot(part, upto) for part in _split3(log_f.T)) + carry_t
        ct_ref[:, blk * LANES:(blk + 1) * LANES] = cs_t * LOG2E
        carry_t = cs_t[:, LANES - 1:LANES]
        cs = sum(_dot(downto, part) for part in _split3(log_f)) + carry
        carry = cs[LANES - 1:LANES, :]
        parts = _split3(-(cs * LOG2E))
        ck_ref[blk * LANES:(blk + 1) * LANES, :] = sum(
            _dot(part, sel) for part, sel in zip(parts, place)).astype(ck_ref.dtype)


def _fox_prep(fl, fgate_bias, batch, seq):
    assert KEY_PARTS * N_HEADS <= LANES
    bias = jnp.zeros((1, LANES), F32).at[0, :N_HEADS].set(fgate_bias.astype(F32))
    return pl.pallas_call(
        functools.partial(_fox_prep_kernel, seq=seq),
        grid=(batch,),
        in_specs=[pl.BlockSpec((seq, LANES), lambda b: (b, 0)),
                  pl.BlockSpec((1, LANES), lambda b: (0, 0))],
        out_specs=[pl.BlockSpec((None, LANES, seq), lambda b: (b, 0, 0)),
                   pl.BlockSpec((seq, LANES), lambda b: (b, 0))],
        out_shape=[jax.ShapeDtypeStruct((batch, LANES, seq), F32),
                   jax.ShapeDtypeStruct((batch * seq, LANES), BF16)],
        compiler_params=_cparams("arbitrary"),
        name="fox_prep",
    )(fl, bias)


def _fox_kernel(qt_ref, k_ref, vt_ref, z_ref, crow_ref, ck_ref, o_ref, *, t):
    first_head = pl.program_id(1) * HEADS_PER_STEP
    i = pl.program_id(2)
    key = lax.broadcasted_iota(jnp.int32, (t, t), 0)
    qry = lax.broadcasted_iota(jnp.int32, (t, t), 1)
    causal = key <= qry
    row = lax.broadcasted_iota(jnp.int32, (LANES, t), 0)
    qt_aug = [jnp.concatenate(
        [qt_ref[_head(j), :],
         jnp.where((row % N_HEADS == first_head + j) & (row < KEY_PARTS * N_HEADS), 1.0, 0.0).astype(BF16)], axis=0)
        for j in range(HEADS_PER_STEP)]

    def scores(j, kb):
        start = pl.multiple_of(kb * t, t)
        k_aug = jnp.concatenate([k_ref[pl.ds(start, t), _head(j)], ck_ref[pl.ds(start, t), :]], axis=1)
        return _dot(k_aug, qt_aug[j])

    def consume(j, x, kb, m, acc, diag):
        start = pl.multiple_of(kb * t, t)
        c_q = crow_ref[pl.ds((first_head + j) % F32_ROWS, 1), pl.ds(pl.multiple_of(i * t, t), t)]
        if diag:
            x = jnp.where(causal, x, NEG_INF)
        m_new = jnp.maximum(m, jnp.max(x, axis=0, keepdims=True) + c_q)
        p = jnp.exp2(x + (c_q - m_new))
        vt = _with_ones_rows(vt_ref[_head(j), pl.ds(start, t)])
        return m_new, jnp.exp2(m - m_new) * acc + _dot(vt, p.astype(BF16))

    final = _pipelined_heads(i, scores, consume,
                             (jnp.full((1, t), M_INIT, F32), jnp.zeros((HEAD_DIM + BF16_ROWS, t), F32)))
    o = jnp.concatenate([(acc[:HEAD_DIM] / jnp.maximum(acc[HEAD_DIM:HEAD_DIM + 1], 1e-30)).T for _, acc in final],
                        axis=1)
    o_ref[...] = (o * _silu(z_ref[...].astype(F32))).astype(o_ref.dtype)


class _Weight:
    def __init__(self, w):
        self.features_first = w.shape[1] % LANES != 0
        self.src = w.T if self.features_first else w

    def pick(self, *ranges, scale_first=None, group_pad=None):
        axis = 0 if self.features_first else 1
        parts = [lax.slice_in_dim(self.src, a, b, axis=axis) for a, b in ranges]
        if scale_first is not None:
            parts[0] = parts[0] * scale_first
        out = jnp.concatenate(parts, axis=axis)
        if group_pad is not None:
            groups, width = group_pad
            out = jnp.moveaxis(out, axis, 0)
            out = out.reshape(groups, -1, out.shape[1])
            out = jnp.pad(out, ((0, 0), (0, width - out.shape[1]), (0, 0))).reshape(groups * width, -1)
            out = jnp.moveaxis(out, 0, axis)
        return out.astype(BF16)

    def project(self, xb, operand, name, out_dtype=BF16, transposed_out=False):
        return _projection(xb, operand, out_dtype, name, features_first=self.features_first,
                           transposed_out=transposed_out)


def _layer_nsa(xb, batch, seq, rel_bias, w_in, cmp_pe_k, cmp_w1_k, cmp_w2_k, cmp_pe_v, cmp_w1_v, cmp_w2_v):
    kc0, ks0, vs0, kw0, vw0, gate0 = (WIDTH + j * KV_WIDTH for j in (0, 2, 3, 4, 5, 6))
    gate1 = gate0 + 3 * N_HEADS
    w = _Weight(w_in)
    h = w.project(xb, w.pick((kc0, vs0), (kw0, vw0), (gate1, w_in.shape[1])), "proj_nsa")
    qvt = w.project(xb, w.pick((0, WIDTH), (vs0, kw0), (vw0, gate0), scale_first=QSCALE), "proj_nsa_t",
                    transposed_out=True)
    gl = w.project(xb, w.pick((gate0, gate1), group_pad=(N_KV, LANES)), "proj_nsa_gates", out_dtype=F32)
    blk = lambda j: j * N_KV
    vrow = WIDTH // HEAD_DIM
    k_cmp, v_cmp = _nsa_compress(h, batch, seq, blk(0), blk(1), cmp_pe_k, cmp_w1_k, cmp_w2_k,
                                 cmp_pe_v, cmp_w1_v, cmp_w2_v)
    o_cmp, sel = _nsa_cmp_select(qvt, batch, seq, k_cmp, v_cmp, rel_bias)
    o_sel = _nsa_sel_attn(h, qvt, batch, seq, blk(2), vrow, sel, rel_bias)
    nd = NSA_WINDOW // LANES + 1
    return _banded_gqa(h, qvt, batch, seq, blk(3), vrow + N_KV, 4 * KV_WIDTH // GROUP_W,
                       _band_tables(rel_bias, nd, NSA_WINDOW, True), mix=(o_cmp, o_sel, gl), name="nsa_window_mix")


def _layer_swa(xb, batch, seq, rel_bias, w_in, sinks):
    v0, z0 = WIDTH + KV_WIDTH, WIDTH + 2 * KV_WIDTH
    w = _Weight(w_in)
    h = w.project(xb, w.pick((WIDTH, v0), (z0, w_in.shape[1])), "proj_swa")
    qvt = w.project(xb, w.pick((0, WIDTH), (v0, z0), scale_first=QSCALE), "proj_swa_t",
                    transposed_out=True)
    nd = SWA_WINDOW // LANES + 1
    return _banded_gqa(h, qvt, batch, seq, 0, WIDTH // HEAD_DIM, KV_WIDTH // GROUP_W,
                       _band_tables(rel_bias, nd, SWA_WINDOW, True), sinks=sinks, name="swa")


def _layer_sb(xb, batch, seq, w_in, t=256):
    w = _Weight(w_in)
    h = w.project(xb, w.pick((0, WIDTH), (WIDTH, w_in.shape[1]), scale_first=QSCALE), "proj_sb")
    return _sb_call(h, batch, seq, t)


def _layer_fox(xb, batch, seq, w_in, fgate_bias, t=256):
    f0 = 3 * WIDTH
    f1 = f0 + N_HEADS
    w = _Weight(w_in)
    h = w.project(xb, w.pick((WIDTH, 2 * WIDTH), (f1, w_in.shape[1])), "proj_fox")
    qvt = w.project(xb, w.pick((0, WIDTH), (2 * WIDTH, f0), scale_first=QSCALE), "proj_fox_t",
                    transposed_out=True)
    fl = w.project(xb, w.pick((f0, f1), group_pad=(1, LANES)), "proj_fox_gates", out_dtype=F32)
    ct, ck = _fox_prep(fl, fgate_bias, batch, seq)
    assert F32_ROWS % HEADS_PER_STEP == 0
    extra_specs = [pl.BlockSpec((None, F32_ROWS, seq), lambda b, hp, i: (b, hp * HEADS_PER_STEP // F32_ROWS, 0)),
                   pl.BlockSpec((seq, LANES), lambda b, hp, i: (b, 0))]
    return _full_attn_call(_fox_kernel, h, qvt, batch, seq, t, extra_specs, [ct, ck], "forgetting")


def kernel(x, rel_bias, w_in_a, w_out_a, ln_g_a, ln_b_a, cmp_pe_k, cmp_w1_k, cmp_w2_k, cmp_pe_v, cmp_w1_v, cmp_w2_v, w_in_b, w_out_b, ln_g_b, ln_b_b, sinks_b, w_in_c, w_out_c, ln_g_c, ln_b_c, w_in_d, w_out_d, ln_g_d, ln_b_d, fgate_bias_d):
    batch, seq, d_model = x.shape
    xf = x.reshape(batch * seq, d_model)
    xb = xf.astype(BF16)
    og = _layer_nsa(xb, batch, seq, rel_bias, w_in_a, cmp_pe_k, cmp_w1_k, cmp_w2_k, cmp_pe_v, cmp_w1_v, cmp_w2_v)
    xf, xb = _out_proj_ln(og, w_out_a, xf, ln_g_a, ln_b_a, True)
    og = _layer_swa(xb, batch, seq, rel_bias, w_in_b, sinks_b)
    xf, xb = _out_proj_ln(og, w_out_b, xf, ln_g_b, ln_b_b, True)
    og = _layer_sb(xb, batch, seq, w_in_c)
    xf, xb = _out_proj_ln(og, w_out_c, xf, ln_g_c, ln_b_c, True)
    og = _layer_fox(xb, batch, seq, w_in_d, fgate_bias_d)
    (xf,) = _out_proj_ln(og, w_out_d, xf, ln_g_d, ln_b_d, False)
    return xf.reshape(batch, seq, d_model)
```

```python
import functools
import math

import numpy as np
import jax
import jax.numpy as jnp
from jax import lax
from jax.experimental import pallas as pl
from jax.experimental.pallas import tpu as pltpu

F32 = jnp.float32
BF16 = jnp.bfloat16

N_HEADS = 16
HEAD_DIM = 128
N_KV = 4
HPG = N_HEADS // N_KV
WIDTH = N_HEADS * HEAD_DIM
KV_WIDTH = N_KV * HEAD_DIM
GROUP_W = HPG * HEAD_DIM
REL_BUCKETS = 32
REL_MAX_DIST = 128
CMP_LEN = 32
CMP_STRIDE = 16
SEL_LEN = 64
SEL_TOPK = 8
NSA_WINDOW = 512
SWA_WINDOW = 128
DEPTH = 4
DN_ALPHA = (2 * DEPTH) ** 0.25
LN_EPS = 1e-5
NEG_INF = -1e30
FORCED_SCORE = 1e4
SCALE = HEAD_DIM ** -0.5
LOG2E = math.log2(math.e)
QSCALE = SCALE * LOG2E

LANES = 128
BF16_ROWS = 16
F32_ROWS = 8
LN_ROWS = 128
TILES_PER_STEP = 16
M_INIT = -1e29
VMEM_LIMIT = 56 * 1024 * 1024


def _cparams(*sem):
    return pltpu.CompilerParams(dimension_semantics=sem, vmem_limit_bytes=VMEM_LIMIT)


def _silu(z):
    return z / (1.0 + jnp.exp(-z))


def _dot_nt(a, b):
    return lax.dot_general(a, b, (((1,), (1,)), ((), ())), preferred_element_type=F32)


def _dot(a, b):
    return jnp.dot(a, b, preferred_element_type=F32)


def _head(j):
    return slice(j * HEAD_DIM, (j + 1) * HEAD_DIM)


def _group_queries(qt):
    return jnp.concatenate([qt[r * HEAD_DIM:(r + 1) * HEAD_DIM] for r in range(HPG)], axis=1)


def _split3(x):
    a1 = x.astype(BF16)
    r1 = x - a1.astype(F32)
    a2 = r1.astype(BF16)
    a3 = (r1 - a2.astype(F32)).astype(BF16)
    return a1, a2, a3


def _proj_kernel(x_ref, w_ref, o_ref, *, w_dim, transposed_out):
    x, w = x_ref[...], w_ref[...]
    if transposed_out:
        y = lax.dot_general(w, x, (((w_dim,), (1,)), ((), ())), preferred_element_type=F32)
    else:
        y = lax.dot_general(x, w, (((1,), (w_dim,)), ((), ())), preferred_element_type=F32)
    o_ref[...] = y.astype(o_ref.dtype)


def _feature_tile(n):
    return next(t for t in (1024, 1280, 1536, 768, 512, 256, 128) if n % t == 0)


def _projection(x, w, out_dtype, name, *, features_first=False, transposed_out=False, tm=2048):
    m, k = x.shape
    n = w.shape[0] if features_first else w.shape[1]
    tn = _feature_tile(n)
    assert m % tm == 0
    w_spec = (pl.BlockSpec((tn, k), lambda j, i: (j, 0)) if features_first
              else pl.BlockSpec((k, tn), lambda j, i: (0, j)))
    if transposed_out:
        out_spec, out_shape = pl.BlockSpec((tn, tm), lambda j, i: (j, i)), (n, m)
    else:
        out_spec, out_shape = pl.BlockSpec((tm, tn), lambda j, i: (i, j)), (m, n)
    return pl.pallas_call(
        functools.partial(_proj_kernel, w_dim=1 if features_first else 0, transposed_out=transposed_out),
        grid=(n // tn, m // tm),
        in_specs=[pl.BlockSpec((tm, k), lambda j, i: (i, 0)), w_spec],
        out_specs=out_spec,
        out_shape=jax.ShapeDtypeStruct(out_shape, out_dtype),
        compiler_params=_cparams("arbitrary", "arbitrary"),
        name=name,
    )(x, w)


def _outln_kernel(og_ref, w_ref, x_ref, g_ref, b_ref, xo_ref, *maybe_xb_ref):
    for c in range(og_ref.shape[0] // LN_ROWS):
        rows = slice(c * LN_ROWS, (c + 1) * LN_ROWS)
        t = DN_ALPHA * x_ref[rows, :] + _dot(og_ref[rows, :], w_ref[...])
        mu = jnp.mean(t, axis=-1, keepdims=True)
        d = t - mu
        var = jnp.mean(d * d, axis=-1, keepdims=True)
        out = d * lax.rsqrt(var + LN_EPS) * g_ref[...] + b_ref[...]
        xo_ref[rows, :] = out
        for xb_ref in maybe_xb_ref:
            xb_ref[rows, :] = out.astype(BF16)


def _out_proj_ln(og, w_out, x, ln_g, ln_b, with_bf16, tm=512):
    m, k = og.shape
    n = w_out.shape[1]
    rows = pl.BlockSpec((tm, n), lambda i: (i, 0))
    out_dtypes = (F32, BF16) if with_bf16 else (F32,)
    return pl.pallas_call(
        _outln_kernel,
        grid=(m // tm,),
        in_specs=[pl.BlockSpec((tm, k), lambda i: (i, 0)),
                  pl.BlockSpec((k, n), lambda i: (0, 0)),
                  rows,
                  pl.BlockSpec((1, n), lambda i: (0, 0)),
                  pl.BlockSpec((1, n), lambda i: (0, 0))],
        out_specs=[rows] * len(out_dtypes),
        out_shape=[jax.ShapeDtypeStruct((m, n), dt) for dt in out_dtypes],
        compiler_params=_cparams("arbitrary"),
        name="out_proj_ln",
    )(og, w_out.astype(BF16), x, ln_g.reshape(1, n), ln_b.reshape(1, n))


def _bucket_np(dist):
    max_exact = REL_BUCKETS // 2
    ratio = np.maximum(dist, max_exact).astype(np.float32) / max_exact
    large = max_exact + (np.log(ratio) / math.log(REL_MAX_DIST / max_exact)
                         * (REL_BUCKETS - max_exact)).astype(np.int32)
    return np.where(dist < max_exact, dist, np.minimum(large, REL_BUCKETS - 1))


def _rel_table(rel_bias, dist, ok):
    bucket = _bucket_np(np.maximum(dist, 0)).reshape(-1)
    onehot_t = np.zeros((REL_BUCKETS, bucket.size), np.float32)
    onehot_t[bucket, np.arange(bucket.size)] = 1.0
    t = jnp.dot(rel_bias.T.astype(F32) * LOG2E, jnp.asarray(onehot_t), precision=lax.Precision.HIGHEST)
    blocked = np.where(ok.reshape(-1), 0.0, NEG_INF).astype(np.float32)
    return (t + blocked[None, :]).reshape((N_HEADS,) + dist.shape)


def _band_tables(rel_bias, nd, window, with_blocked=False):
    key = np.arange(LANES)[:, None]
    qry = np.arange(LANES)[None, :]
    dist = np.stack([LANES * d + qry - key for d in range(nd)])
    ok = dist >= 0
    if window is not None:
        ok &= dist < window
    if with_blocked:
        dist = np.concatenate([dist, np.zeros((1, LANES, LANES), dist.dtype)])
        ok = np.concatenate([ok, np.zeros((1, LANES, LANES), bool)])
    return _rel_table(rel_bias, dist, ok)


def _group_table(tbl_ref, idx):
    return jnp.concatenate([tbl_ref[r, idx] for r in range(HPG)], axis=1)


def _with_ones_rows(vt):
    return jnp.concatenate([vt, jnp.ones((BF16_ROWS, vt.shape[1]), BF16)], axis=0)


def _finish_group(acc):
    o_t = acc[:HEAD_DIM] / jnp.maximum(acc[HEAD_DIM:HEAD_DIM + 1], 1e-30)
    return jnp.concatenate([o_t[:, r * LANES:(r + 1) * LANES].T for r in range(HPG)], axis=1)


def _online_update(x, vt, m, acc):
    m_new = jnp.maximum(m, jnp.max(x, axis=0, keepdims=True))
    p = jnp.exp2(x - m_new)
    acc = jnp.exp2(m - m_new) * acc + _dot(_with_ones_rows(vt), p.astype(BF16))
    return m_new, acc


def _banded_kernel(*refs, nd, has_sink, mixed):
    refs = list(refs)
    qt_ref, k_ref, vt_ref, tbl_ref, z_ref = refs[:5]
    rest = refs[5:]
    sink_ref = rest.pop(0) if has_sink else None
    oc_ref, os_ref, gl_ref = (rest.pop(0), rest.pop(0), rest.pop(0)) if mixed else (None, None, None)
    (o_ref,) = rest
    for u in range(TILES_PER_STEP):
        i = pl.program_id(2) * TILES_PER_STEP + u
        tile = slice(u * LANES, (u + 1) * LANES)
        ks, vts, tbls = [], [], []
        for d in range(nd):
            kb = i - d
            start = pl.multiple_of(jnp.maximum(kb, 0) * LANES, LANES)
            ks.append(k_ref[pl.ds(start, LANES), :])
            vts.append(vt_ref[:, pl.ds(start, LANES)])
            tbls.append(_group_table(tbl_ref, jnp.where(kb >= 0, d, nd)))
        x = _dot(jnp.concatenate(ks, axis=0), _group_queries(qt_ref[:, tile])) + jnp.concatenate(tbls, axis=0)
        m = jnp.max(x, axis=0, keepdims=True)
        if has_sink:
            m = jnp.maximum(m, sink_ref[...])
        p = jnp.exp2(x - m)
        acc = _dot(_with_ones_rows(jnp.concatenate(vts, axis=1)), p.astype(BF16))
        if has_sink:
            rows = lax.broadcasted_iota(jnp.int32, acc.shape, 0)
            acc = acc + jnp.where(rows >= HEAD_DIM, jnp.exp2(sink_ref[...] - m), 0.0)
        o = _finish_group(acc)
        if mixed:
            gates = 1.0 / (1.0 + jnp.exp(-gl_ref[tile, :]))
            oc = oc_ref[tile, :].astype(F32)
            osel = os_ref[tile, :].astype(F32)
            o = jnp.concatenate(
                [gates[:, 3 * r:3 * r + 1] * oc[:, _head(r)] + gates[:, 3 * r + 1:3 * r + 2] * osel[:, _head(r)]
                 + gates[:, 3 * r + 2:3 * r + 3] * o[:, _head(r)] for r in range(HPG)], axis=1)
        o_ref[tile, :] = (o * _silu(z_ref[tile, :].astype(F32))).astype(o_ref.dtype)


def _banded_gqa(h, qvt, batch, seq, kblk, vtblk, zblk, tbl, sinks=None, mix=None, name="banded"):
    step = TILES_PER_STEP * LANES
    nq = seq // step
    nd = tbl.shape[1] - 1
    rows_of_group = lambda blk: pl.BlockSpec((step, GROUP_W), lambda b, g, i: (b * nq + i, blk + g))
    in_specs = [pl.BlockSpec((GROUP_W, step), lambda b, g, i: (g, b * nq + i)),
                pl.BlockSpec((seq, HEAD_DIM), lambda b, g, i: (b, kblk + g)),
                pl.BlockSpec((HEAD_DIM, seq), lambda b, g, i: (vtblk + g, b)),
                pl.BlockSpec((HPG, nd + 1, LANES, LANES), lambda b, g, i: (g, 0, 0, 0)),
                rows_of_group(zblk)]
    args = [qvt, h, qvt, tbl, h]
    if sinks is not None:
        sink_row = jnp.repeat(sinks.astype(F32).reshape(N_KV, HPG) * LOG2E, LANES, axis=1)
        in_specs.append(pl.BlockSpec((None, 1, HPG * LANES), lambda b, g, i: (g, 0, 0)))
        args.append(sink_row.reshape(N_KV, 1, HPG * LANES))
    if mix is not None:
        in_specs += [rows_of_group(0), rows_of_group(0),
                     pl.BlockSpec((step, LANES), lambda b, g, i: (b * nq + i, g))]
        args += list(mix)
    return pl.pallas_call(
        functools.partial(_banded_kernel, nd=nd, has_sink=sinks is not None, mixed=mix is not None),
        grid=(batch, N_KV, nq),
        in_specs=in_specs,
        out_specs=pl.BlockSpec((step, GROUP_W), lambda b, g, i: (b * nq + i, g)),
        out_shape=jax.ShapeDtypeStruct((batch * seq, WIDTH), BF16),
        compiler_params=_cparams("arbitrary", "arbitrary", "arbitrary"),
        name=name,
    )(*args)


def _compress_kernel(kc_ref, vc_ref, pek_ref, w1k_ref, w2k_ref, pev_ref, w1v_ref, w2v_ref,
                     ko_ref, vo_ref, xs_ref, *, nc):
    half = CMP_LEN // 2
    for x_ref, pe_ref, w1_ref, w2_ref, o_ref in ((kc_ref, pek_ref, w1k_ref, w2k_ref, ko_ref),
                                                 (vc_ref, pev_ref, w1v_ref, w2v_ref, vo_ref)):
        xs_ref[...] = x_ref[...].astype(F32)
        first = jnp.zeros((nc, HEAD_DIM), F32)
        second = jnp.zeros((nc, HEAD_DIM), F32)
        for l in range(half):
            xl = xs_ref[pl.ds(l, nc, stride=CMP_STRIDE), :]
            first += _dot((xl + pe_ref[l:l + 1, :]).astype(BF16), w1_ref[l])
            second += _dot((xl + pe_ref[half + l:half + l + 1, :]).astype(BF16), w1_ref[half + l])
        hid = _silu(first + pltpu.roll(second, nc - 1, 0))
        o_ref[...] = _dot(hid.astype(BF16), w2_ref[...]).astype(o_ref.dtype)


def _nsa_compress(h, batch, seq, kcblk, vcblk, pe_k, w1_k, w2_k, pe_v, w1_v, w2_v):
    assert CMP_LEN == 2 * CMP_STRIDE
    nc = seq // CMP_STRIDE
    kv_spec = lambda blk: pl.BlockSpec((seq, HEAD_DIM), lambda b, g: (b, blk + g))
    full = lambda shape: pl.BlockSpec(shape, lambda b, g: (0,) * len(shape))
    out_spec = pl.BlockSpec((None, None, nc, HEAD_DIM), lambda b, g: (b, g, 0, 0))
    out_shape = jax.ShapeDtypeStruct((batch, N_KV, nc, HEAD_DIM), BF16)
    return pl.pallas_call(
        functools.partial(_compress_kernel, nc=nc),
        grid=(batch, N_KV),
        in_specs=[kv_spec(kcblk), kv_spec(vcblk),
                  full((CMP_LEN, HEAD_DIM)), full((CMP_LEN, HEAD_DIM, HEAD_DIM)), full((HEAD_DIM, HEAD_DIM)),
                  full((CMP_LEN, HEAD_DIM)), full((CMP_LEN, HEAD_DIM, HEAD_DIM)), full((HEAD_DIM, HEAD_DIM))],
        out_specs=[out_spec, out_spec],
        out_shape=[out_shape, out_shape],
        scratch_shapes=[pltpu.VMEM((seq, HEAD_DIM), F32)],
        compiler_params=_cparams("arbitrary", "arbitrary"),
        name="nsa_compress",
    )(h, h, pe_k, w1_k.astype(BF16), w2_k.astype(BF16), pe_v, w1_v.astype(BF16), w2_v.astype(BF16))


def _cmp_table(rel_bias, nc):
    u = np.arange(2 * nc)[:, None] - nc
    dist = np.arange(LANES)[None, :] - (u * CMP_STRIDE + CMP_LEN - 1)
    return _rel_table(rel_bias, dist, dist >= 0)


def _inter_t(seq, nc):
    nb = seq // SEL_LEN
    cstart = np.arange(nc) * CMP_STRIDE
    sstart = np.arange(nb) * SEL_LEN
    inter = np.clip(np.minimum(cstart[None, :] + CMP_LEN, sstart[:, None] + SEL_LEN)
                    - np.maximum(cstart[None, :], sstart[:, None]), 0, None) / CMP_LEN
    return inter.astype(np.float32)


def _cmp_select_kernel(qt_ref, kc_ref, vc_ref, tbl_ref, inter_ref, o_ref, sel_ref, *, nb, nc):
    first = pl.program_id(1) * TILES_PER_STEP
    width = TILES_PER_STEP * LANES
    kc = kc_ref[...]
    vc = vc_ref[...]
    per_tile = LANES // CMP_STRIDE
    row0 = [pl.multiple_of(nc - (first + u) * per_tile, per_tile) for u in range(TILES_PER_STEP)]
    outs = []
    p_sum_t = None
    for r in range(HPG):
        tbl = jnp.concatenate([tbl_ref[r, pl.ds(row, nc), :] for row in row0], axis=1)
        st = _dot(kc, qt_ref[r * HEAD_DIM:(r + 1) * HEAD_DIM, :]) + tbl
        pt = jnp.where(tbl > 0.5 * NEG_INF, jnp.exp2(st - jnp.max(st, axis=0, keepdims=True)), 0.0)
        pt = pt / jnp.maximum(jnp.sum(pt, axis=0, keepdims=True), 1e-30)
        p_sum_t = pt if p_sum_t is None else p_sum_t + pt
        outs.append(_dot(pt.T.astype(BF16), vc))
    o_ref[...] = jnp.concatenate(outs, axis=1).astype(o_ref.dtype)

    inter = inter_ref[...]
    imp = sum(_dot(inter, part) for part in _split3(p_sum_t))
    blk = lax.broadcasted_iota(jnp.int32, (nb, width), 0)
    pos = first * LANES + lax.broadcasted_iota(jnp.int32, (nb, width), 1)
    cur = lax.shift_right_logical(pos, int(math.log2(SEL_LEN)))
    allowed = blk * SEL_LEN <= pos
    forced = (blk == 0) | (blk == cur) | (blk == cur - 1)
    imp = jnp.where(allowed, jnp.where(forced, FORCED_SCORE, imp), NEG_INF)
    rank = jnp.zeros((nb, width), F32)
    for c in range(nb):
        row = imp[c:c + 1, :]
        earlier = jnp.where(blk > c, 1.0, 0.0)
        rank += jnp.where(row > imp, 1.0, jnp.where(row == imp, earlier, 0.0))
    sel_t = jnp.where((rank < SEL_TOPK) & (imp > 0.5 * NEG_INF), 1.0, 0.0)
    sel_t = jnp.concatenate([sel_t, jnp.zeros((LANES - nb, width), F32)], axis=0).astype(sel_ref.dtype)
    for u in range(TILES_PER_STEP):
        sel_ref[u] = sel_t[:, u * LANES:(u + 1) * LANES]


def _nsa_cmp_select(qvt, batch, seq, k_cmp, v_cmp, rel_bias):
    step = TILES_PER_STEP * LANES
    nq = seq // step
    nc = seq // CMP_STRIDE
    nb = seq // SEL_LEN
    assert nc == LANES and nb <= LANES
    tbl = _cmp_table(rel_bias, nc)
    inter = jnp.asarray(_inter_t(seq, nc), BF16)
    cmp_spec = pl.BlockSpec((None, None, nc, HEAD_DIM), lambda g, i, b: (b, g, 0, 0))
    return pl.pallas_call(
        functools.partial(_cmp_select_kernel, nb=nb, nc=nc),
        grid=(N_KV, nq, batch),
        in_specs=[pl.BlockSpec((GROUP_W, step), lambda g, i, b: (g, b * nq + i)),
                  cmp_spec, cmp_spec,
                  pl.BlockSpec((HPG, 2 * nc, LANES), lambda g, i, b: (g, 0, 0)),
                  pl.BlockSpec((nb, nc), lambda g, i, b: (0, 0))],
        out_specs=[pl.BlockSpec((step, GROUP_W), lambda g, i, b: (b * nq + i, g)),
                   pl.BlockSpec((None, None, TILES_PER_STEP, LANES, LANES), lambda g, i, b: (b, g, i, 0, 0))],
        out_shape=[jax.ShapeDtypeStruct((batch * seq, WIDTH), BF16),
                   jax.ShapeDtypeStruct((batch, N_KV, seq // LANES, LANES, LANES), BF16)],
        compiler_params=_cparams("arbitrary", "arbitrary", "arbitrary"),
        name="nsa_cmp_select",
    )(qvt, k_cmp, v_cmp, tbl, inter)


SEL_TILES = 2
SEL_KEYS = SEL_TILES * LANES


def _expand_np(seq):
    j = np.arange(LANES)[None, None, :]
    key = np.arange(seq // SEL_KEYS)[:, None, None] * SEL_KEYS + np.arange(SEL_KEYS)[None, :, None]
    return (j == key // SEL_LEN).astype(np.float32)


def _sel_attn_kernel(qt_ref, k_ref, vt_ref, sel_ref, tbl_ref, exp_ref, o_ref):
    last = pl.program_id(2)
    lanes = HPG * LANES
    beyond = tbl_ref.shape[1] - 1
    far = beyond - 1
    tiles = range(SEL_TILES)
    qt_aug = []
    for u in tiles:
        blocked = ((1.0 - sel_ref[u].astype(F32)) * NEG_INF).astype(BF16)
        qt_aug.append(jnp.concatenate([_group_queries(qt_ref[:, u * LANES:(u + 1) * LANES]),
                                       jnp.concatenate([blocked] * HPG, axis=1)], axis=0))

    def scores(u, kt):
        start = pl.multiple_of(kt * SEL_KEYS, SEL_KEYS)
        k_aug = jnp.concatenate([k_ref[pl.ds(start, SEL_KEYS), :], exp_ref[kt]], axis=1)
        return _dot(k_aug, qt_aug[u])

    def body(kt, carry):
        start = pl.multiple_of(kt * SEL_KEYS, SEL_KEYS)
        out = []
        for u in tiles:
            s, m, acc = carry[u]
            s_next = scores(u, jnp.minimum(kt + 1, last))
            behind = [(last - kt) * SEL_TILES + u - c for c in tiles]
            x = s + jnp.concatenate(
                [_group_table(tbl_ref, jnp.where(d >= 0, jnp.minimum(d, far), beyond)) for d in behind], axis=0)
            out.append((s_next,) + _online_update(x, vt_ref[:, pl.ds(start, SEL_KEYS)], m, acc))
        return tuple(out)

    init = tuple((scores(u, 0), jnp.full((1, lanes), M_INIT, F32),
                  jnp.zeros((HEAD_DIM + BF16_ROWS, lanes), F32)) for u in tiles)
    final = lax.fori_loop(0, last + 1, body, init)
    for u in tiles:
        o_ref[u * LANES:(u + 1) * LANES, :] = _finish_group(final[u][2]).astype(o_ref.dtype)


def _nsa_sel_attn(h, qvt, batch, seq, kblk, vtblk, sel, rel_bias):
    nq = seq // SEL_KEYS
    tbl = _band_tables(rel_bias, 3, None, True)
    expand = jnp.asarray(_expand_np(seq), BF16)
    return pl.pallas_call(
        _sel_attn_kernel,
        grid=(batch, N_KV, nq),
        in_specs=[pl.BlockSpec((GROUP_W, SEL_KEYS), lambda b, g, i: (g, b * nq + i)),
                  pl.BlockSpec((seq, HEAD_DIM), lambda b, g, i: (b, kblk + g)),
                  pl.BlockSpec((HEAD_DIM, seq), lambda b, g, i: (vtblk + g, b)),
                  pl.BlockSpec((None, None, SEL_TILES, LANES, LANES), lambda b, g, i: (b, g, i, 0, 0)),
                  pl.BlockSpec((HPG, 4, LANES, LANES), lambda b, g, i: (g, 0, 0, 0)),
                  pl.BlockSpec((seq // SEL_KEYS, SEL_KEYS, LANES), lambda b, g, i: (0, 0, 0))],
        out_specs=pl.BlockSpec((SEL_KEYS, GROUP_W), lambda b, g, i: (b * nq + i, g)),
        out_shape=jax.ShapeDtypeStruct((batch * seq, WIDTH), BF16),
        compiler_params=_cparams("arbitrary", "arbitrary", "arbitrary"),
        name="nsa_sel_attn",
    )(qvt, h, qvt, sel, tbl, expand)


HEADS_PER_STEP = 8
STEP_W = HEADS_PER_STEP * HEAD_DIM


def _pipelined_heads(i, scores, consume, init):
    heads = range(HEADS_PER_STEP)
    ahead = [scores(j, jnp.maximum(i - 1, 0)) for j in heads]
    state = [consume(j, scores(j, i), i, *init, True) for j in heads]

    def body(step, carry):
        kb = i - step
        return tuple((scores(j, jnp.maximum(kb - 1, 0)),) + consume(j, carry[j][0], kb, *carry[j][1:], False)
                     for j in heads)

    carry = lax.fori_loop(1, i + 1, body, tuple((ahead[j],) + state[j] for j in heads))
    return [c[1:] for c in carry]


SB_HEADS = 8
SB_SPAN = 2
SB_W = SB_HEADS * HEAD_DIM


def _sb_kernel(q_ref, k_ref, v_ref, z_ref, o_ref, *, t):
    i = pl.program_id(2)
    tq = SB_SPAN * t
    qry = lax.broadcasted_iota(jnp.int32, (tq, t), 0)
    key = lax.broadcasted_iota(jnp.int32, (tq, t), 1)
    row = lax.broadcasted_iota(jnp.int32, (t, t), 0)
    col = lax.broadcasted_iota(jnp.int32, (t, t), 1)
    later = jnp.where(row >= col, 1.0, 0.0).astype(BF16)
    heads = range(SB_HEADS)

    def scores(j, kb):
        start = pl.multiple_of(kb * t, t)
        return _dot_nt(q_ref[:, _head(j)], k_ref[pl.ds(start, t), _head(j)])

    def consume(j, y, kb, run, acc, strict=None):
        start = pl.multiple_of(kb * t, t)
        neg = -y
        soft = jnp.log(1.0 + jnp.exp2(jnp.minimum(y, neg))) * LOG2E
        log_keep = jnp.minimum(neg, 0.0) - soft
        if strict is not None:
            log_keep = jnp.where(strict, log_keep, 0.0)
        keep = _dot(log_keep.astype(BF16), later)
        a = jnp.exp2(y + keep + run)
        if strict is not None:
            a = jnp.where(strict, a, 0.0)
        acc = acc + _dot(a.astype(BF16), v_ref[pl.ds(start, t), _head(j)])
        return run + keep[:, 0:1], acc

    top = SB_SPAN * i + SB_SPAN - 1
    state = [(scores(j, top), jnp.zeros((tq, 1), F32), jnp.zeros((tq, HEAD_DIM), F32)) for j in heads]
    for c in reversed(range(SB_SPAN)):
        kb = SB_SPAN * i + c
        strict = key + c * t < qry
        state = [(scores(j, jnp.maximum(kb - 1, 0)),) + consume(j, state[j][0], kb, *state[j][1:], strict)
                 for j in heads]

    def body(step, carry):
        kb = SB_SPAN * i - step
        return tuple((scores(j, jnp.maximum(kb - 1, 0)),) + consume(j, carry[j][0], kb, *carry[j][1:])
                     for j in heads)

    final = lax.fori_loop(1, SB_SPAN * i + 1, body, tuple(state))
    o = jnp.concatenate([acc for _, _, acc in final], axis=1)
    o_ref[...] = (o * _silu(z_ref[...].astype(F32))).astype(o_ref.dtype)


def _sb_call(h, batch, seq, t):
    tq = SB_SPAN * t
    nq = seq // tq
    per_w = WIDTH // SB_W
    rows = lambda part: pl.BlockSpec((tq, SB_W), lambda b, hp, i: (b * nq + i, part * per_w + hp))
    whole = lambda part: pl.BlockSpec((seq, SB_W), lambda b, hp, i: (b, part * per_w + hp))
    return pl.pallas_call(
        functools.partial(_sb_kernel, t=t),
        grid=(batch, per_w, nq),
        in_specs=[rows(0), whole(1), whole(2), rows(3)],
        out_specs=rows(0),
        out_shape=jax.ShapeDtypeStruct((batch * seq, WIDTH), BF16),
        compiler_params=_cparams("arbitrary", "arbitrary", "arbitrary"),
        name="stick_breaking",
    )(h, h, h, h)


def _full_attn_call(kernel, h, qvt, batch, seq, t, extra_specs, extra_args, name):
    nq = seq // t
    per_w = WIDTH // STEP_W
    in_specs = [pl.BlockSpec((STEP_W, t), lambda b, hp, i: (hp, b * nq + i)),
                pl.BlockSpec((seq, STEP_W), lambda b, hp, i: (b, hp)),
                pl.BlockSpec((STEP_W, seq), lambda b, hp, i: (per_w + hp, b)),
                pl.BlockSpec((t, STEP_W), lambda b, hp, i: (b * nq + i, per_w + hp))]
    return pl.pallas_call(
        functools.partial(kernel, t=t),
        grid=(batch, per_w, nq),
        in_specs=in_specs + extra_specs,
        out_specs=pl.BlockSpec((t, STEP_W), lambda b, hp, i: (b * nq + i, hp)),
        out_shape=jax.ShapeDtypeStruct((batch * seq, WIDTH), BF16),
        compiler_params=_cparams("arbitrary", "arbitrary", "arbitrary"),
        name=name,
    )(qvt, h, qvt, h, *extra_args)


KEY_PARTS = 3


def _fox_prep_kernel(fl_ref, bias_ref, ct_ref, ck_ref, *, seq):
    row = lax.broadcasted_iota(jnp.int32, (LANES, LANES), 0)
    col = lax.broadcasted_iota(jnp.int32, (LANES, LANES), 1)
    upto = jnp.where(row <= col, 1.0, 0.0).astype(BF16)
    downto = jnp.where(col <= row, 1.0, 0.0).astype(BF16)
    place = [jnp.where((col == row + j * N_HEADS) & (row < N_HEADS), 1.0, 0.0).astype(BF16)
             for j in range(KEY_PARTS)]
    carry_t = jnp.zeros((LANES, 1), F32)
    carry = jnp.zeros((1, LANES), F32)
    for blk in range(seq // LANES):
        x = fl_ref[blk * LANES:(blk + 1) * LANES, :] + bias_ref[...]
        log_f = jnp.minimum(x, 0.0) - jnp.log1p(jnp.exp(-jnp.abs(x)))
        cs_t = sum(_dot(part, upto) for part in _split3(log_f.T)) + carry_t
        ct_ref[:, blk * LANES:(blk + 1) * LANES] = cs_t * LOG2E
        carry_t = cs_t[:, LANES - 1:LANES]
        cs = sum(_dot(downto, part) for part in _split3(log_f)) + carry
        carry = cs[LANES - 1:LANES, :]
        parts = _split3(-(cs * LOG2E))
        ck_ref[blk * LANES:(blk + 1) * LANES, :] = sum(
            _dot(part, sel) for part, sel in zip(parts, place)).astype(ck_ref.dtype)


def _fox_prep(fl, fgate_bias, batch, seq):
    assert KEY_PARTS * N_HEADS <= LANES
    bias = jnp.zeros((1, LANES), F32).at[0, :N_HEADS].set(fgate_bias.astype(F32))
    return pl.pallas_call(
        functools.partial(_fox_prep_kernel, seq=seq),
        grid=(batch,),
        in_specs=[pl.BlockSpec((seq, LANES), lambda b: (b, 0)),
                  pl.BlockSpec((1, LANES), lambda b: (0, 0))],
        out_specs=[pl.BlockSpec((None, LANES, seq), lambda b: (b, 0, 0)),
                   pl.BlockSpec((seq, LANES), lambda b: (b, 0))],
        out_shape=[jax.ShapeDtypeStruct((batch, LANES, seq), F32),
                   jax.ShapeDtypeStruct((batch * seq, LANES), BF16)],
        compiler_params=_cparams("arbitrary"),
        name="fox_prep",
    )(fl, bias)


def _fox_kernel(qt_ref, k_ref, vt_ref, z_ref, crow_ref, ck_ref, o_ref, *, t):
    first_head = pl.program_id(1) * HEADS_PER_STEP
    i = pl.program_id(2)
    key = lax.broadcasted_iota(jnp.int32, (t, t), 0)
    qry = lax.broadcasted_iota(jnp.int32, (t, t), 1)
    causal = key <= qry
    row = lax.broadcasted_iota(jnp.int32, (LANES, t), 0)
    qt_aug = [jnp.concatenate(
        [qt_ref[_head(j), :],
         jnp.where((row % N_HEADS == first_head + j) & (row < KEY_PARTS * N_HEADS), 1.0, 0.0).astype(BF16)], axis=0)
        for j in range(HEADS_PER_STEP)]

    def scores(j, kb):
        start = pl.multiple_of(kb * t, t)
        k_aug = jnp.concatenate([k_ref[pl.ds(start, t), _head(j)], ck_ref[pl.ds(start, t), :]], axis=1)
        return _dot(k_aug, qt_aug[j])

    def consume(j, x, kb, m, acc, diag):
        start = pl.multiple_of(kb * t, t)
        c_q = crow_ref[pl.ds((first_head + j) % F32_ROWS, 1), pl.ds(pl.multiple_of(i * t, t), t)]
        if diag:
            x = jnp.where(causal, x, NEG_INF)
        m_new = jnp.maximum(m, jnp.max(x, axis=0, keepdims=True) + c_q)
        p = jnp.exp2(x + (c_q - m_new))
        vt = _with_ones_rows(vt_ref[_head(j), pl.ds(start, t)])
        return m_new, jnp.exp2(m - m_new) * acc + _dot(vt, p.astype(BF16))

    final = _pipelined_heads(i, scores, consume,
                             (jnp.full((1, t), M_INIT, F32), jnp.zeros((HEAD_DIM + BF16_ROWS, t), F32)))
    o = jnp.concatenate([(acc[:HEAD_DIM] / jnp.maximum(acc[HEAD_DIM:HEAD_DIM + 1], 1e-30)).T for _, acc in final],
                        axis=1)
    o_ref[...] = (o * _silu(z_ref[...].astype(F32))).astype(o_ref.dtype)


class _Weight:
    def __init__(self, w):
        self.features_first = w.shape[1] % LANES != 0
        self.src = w.T if self.features_first else w

    def pick(self, *ranges, scale_first=None, group_pad=None):
        axis = 0 if self.features_first else 1
        parts = [lax.slice_in_dim(self.src, a, b, axis=axis) for a, b in ranges]
        if scale_first is not None:
            parts[0] = parts[0] * scale_first
        out = jnp.concatenate(parts, axis=axis)
        if group_pad is not None:
            groups, width = group_pad
            out = jnp.moveaxis(out, axis, 0)
            out = out.reshape(groups, -1, out.shape[1])
            out = jnp.pad(out, ((0, 0), (0, width - out.shape[1]), (0, 0))).reshape(groups * width, -1)
            out = jnp.moveaxis(out, 0, axis)
        return out.astype(BF16)

    def project(self, xb, operand, name, out_dtype=BF16, transposed_out=False):
        return _projection(xb, operand, out_dtype, name, features_first=self.features_first,
                           transposed_out=transposed_out)


def _layer_nsa(xb, batch, seq, rel_bias, w_in, cmp_pe_k, cmp_w1_k, cmp_w2_k, cmp_pe_v, cmp_w1_v, cmp_w2_v):
    kc0, ks0, vs0, kw0, vw0, gate0 = (WIDTH + j * KV_WIDTH for j in (0, 2, 3, 4, 5, 6))
    gate1 = gate0 + 3 * N_HEADS
    w = _Weight(w_in)
    h = w.project(xb, w.pick((kc0, vs0), (kw0, vw0), (gate1, w_in.shape[1])), "proj_nsa")
    qvt = w.project(xb, w.pick((0, WIDTH), (vs0, kw0), (vw0, gate0), scale_first=QSCALE), "proj_nsa_t",
                    transposed_out=True)
    gl = w.project(xb, w.pick((gate0, gate1), group_pad=(N_KV, LANES)), "proj_nsa_gates", out_dtype=F32)
    blk = lambda j: j * N_KV
    vrow = WIDTH // HEAD_DIM
    k_cmp, v_cmp = _nsa_compress(h, batch, seq, blk(0), blk(1), cmp_pe_k, cmp_w1_k, cmp_w2_k,
                                 cmp_pe_v, cmp_w1_v, cmp_w2_v)
    o_cmp, sel = _nsa_cmp_select(qvt, batch, seq, k_cmp, v_cmp, rel_bias)
    o_sel = _nsa_sel_attn(h, qvt, batch, seq, blk(2), vrow, sel, rel_bias)
    nd = NSA_WINDOW // LANES + 1
    return _banded_gqa(h, qvt, batch, seq, blk(3), vrow + N_KV, 4 * KV_WIDTH // GROUP_W,
                       _band_tables(rel_bias, nd, NSA_WINDOW, True), mix=(o_cmp, o_sel, gl), name="nsa_window_mix")


def _layer_swa(xb, batch, seq, rel_bias, w_in, sinks):
    v0, z0 = WIDTH + KV_WIDTH, WIDTH + 2 * KV_WIDTH
    w = _Weight(w_in)
    h = w.project(xb, w.pick((WIDTH, v0), (z0, w_in.shape[1])), "proj_swa")
    qvt = w.project(xb, w.pick((0, WIDTH), (v0, z0), scale_first=QSCALE), "proj_swa_t",
                    transposed_out=True)
    nd = SWA_WINDOW // LANES + 1
    return _banded_gqa(h, qvt, batch, seq, 0, WIDTH // HEAD_DIM, KV_WIDTH // GROUP_W,
                       _band_tables(rel_bias, nd, SWA_WINDOW, True), sinks=sinks, name="swa")


def _layer_sb(xb, batch, seq, w_in, t=256):
    w = _Weight(w_in)
    h = w.project(xb, w.pick((0, WIDTH), (WIDTH, w_in.shape[1]), scale_first=QSCALE), "proj_sb")
    return _sb_call(h, batch, seq, t)


def _layer_fox(xb, batch, seq, w_in, fgate_bias, t=256):
    f0 = 3 * WIDTH
    f1 = f0 + N_HEADS
    w = _Weight(w_in)
    h = w.project(xb, w.pick((WIDTH, 2 * WIDTH), (f1, w_in.shape[1])), "proj_fox")
    qvt = w.project(xb, w.pick((0, WIDTH), (2 * WIDTH, f0), scale_first=QSCALE), "proj_fox_t",
                    transposed_out=True)
    fl = w.project(xb, w.pick((f0, f1), group_pad=(1, LANES)), "proj_fox_gates", out_dtype=F32)
    ct, ck = _fox_prep(fl, fgate_bias, batch, seq)
    assert F32_ROWS % HEADS_PER_STEP == 0
    extra_specs = [pl.BlockSpec((None, F32_ROWS, seq), lambda b, hp, i: (b, hp * HEADS_PER_STEP // F32_ROWS, 0)),
                   pl.BlockSpec((seq, LANES), lambda b, hp, i: (b, 0))]
    return _full_attn_call(_fox_kernel, h, qvt, batch, seq, t, extra_specs, [ct, ck], "forgetting")


def kernel(x, rel_bias, w_in_a, w_out_a, ln_g_a, ln_b_a, cmp_pe_k, cmp_w1_k, cmp_w2_k, cmp_pe_v, cmp_w1_v, cmp_w2_v, w_in_b, w_out_b, ln_g_b, ln_b_b, sinks_b, w_in_c, w_out_c, ln_g_c, ln_b_c, w_in_d, w_out_d, ln_g_d, ln_b_d, fgate_bias_d):
    batch, seq, d_model = x.shape
    xf = x.reshape(batch * seq, d_model)
    xb = xf.astype(BF16)
    og = _layer_nsa(xb, batch, seq, rel_bias, w_in_a, cmp_pe_k, cmp_w1_k, cmp_w2_k, cmp_pe_v, cmp_w1_v, cmp_w2_v)
    xf, xb = _out_proj_ln(og, w_out_a, xf, ln_g_a, ln_b_a, True)
    og = _layer_swa(xb, batch, seq, rel_bias, w_in_b, sinks_b)
    xf, xb = _out_proj_ln(og, w_out_b, xf, ln_g_b, ln_b_b, True)
    og = _layer_sb(xb, batch, seq, w_in_c)
    xf, xb = _out_proj_ln(og, w_out_c, xf, ln_g_c, ln_b_c, True)
    og = _layer_fox(xb, batch, seq, w_in_d, fgate_bias_d)
    (xf,) = _out_proj_ln(og, w_out_d, xf, ln_g_d, ln_b_d, False)
    return xf.reshape(batch, seq, d_model)
```

```python
import functools
import math

import numpy as np
import jax
import jax.numpy as jnp
from jax import lax
from jax.experimental import pallas as pl
from jax.experimental.pallas import tpu as pltpu

F32 = jnp.float32
BF16 = jnp.bfloat16

N_HEADS = 16
HEAD_DIM = 128
N_KV = 4
HPG = N_HEADS // N_KV
WIDTH = N_HEADS * HEAD_DIM
KV_WIDTH = N_KV * HEAD_DIM
GROUP_W = HPG * HEAD_DIM
REL_BUCKETS = 32
REL_MAX_DIST = 128
CMP_LEN = 32
CMP_STRIDE = 16
SEL_LEN = 64
SEL_TOPK = 8
NSA_WINDOW = 512
SWA_WINDOW = 128
DEPTH = 4
DN_ALPHA = (2 * DEPTH) ** 0.25
LN_EPS = 1e-5
NEG_INF = -1e30
FORCED_SCORE = 1e4
SCALE = HEAD_DIM ** -0.5
LOG2E = math.log2(math.e)
QSCALE = SCALE * LOG2E

LANES = 128
BF16_ROWS = 16
F32_ROWS = 8
LN_ROWS = 128
TILES_PER_STEP = 16
M_INIT = -1e29
VMEM_LIMIT = 56 * 1024 * 1024


def _cparams(*sem):
    return pltpu.CompilerParams(dimension_semantics=sem, vmem_limit_bytes=VMEM_LIMIT)


def _silu(z):
    return z / (1.0 + jnp.exp(-z))


def _dot_nt(a, b):
    return lax.dot_general(a, b, (((1,), (1,)), ((), ())), preferred_element_type=F32)


def _dot(a, b):
    return jnp.dot(a, b, preferred_element_type=F32)


def _head(j):
    return slice(j * HEAD_DIM, (j + 1) * HEAD_DIM)


def _group_queries(qt):
    return jnp.concatenate([qt[r * HEAD_DIM:(r + 1) * HEAD_DIM] for r in range(HPG)], axis=1)


def _split3(x):
    a1 = x.astype(BF16)
    r1 = x - a1.astype(F32)
    a2 = r1.astype(BF16)
    a3 = (r1 - a2.astype(F32)).astype(BF16)
    return a1, a2, a3


def _proj_kernel(x_ref, w_ref, o_ref, *, w_dim, transposed_out):
    x, w = x_ref[...], w_ref[...]
    if transposed_out:
        y = lax.dot_general(w, x, (((w_dim,), (1,)), ((), ())), preferred_element_type=F32)
    else:
        y = lax.dot_general(x, w, (((1,), (w_dim,)), ((), ())), preferred_element_type=F32)
    o_ref[...] = y.astype(o_ref.dtype)


def _feature_tile(n):
    return next(t for t in (1024, 1280, 1536, 768, 512, 256, 128) if n % t == 0)


def _projection(x, w, out_dtype, name, *, features_first=False, transposed_out=False, tm=2048):
    m, k = x.shape
    n = w.shape[0] if features_first else w.shape[1]
    tn = _feature_tile(n)
    assert m % tm == 0
    w_spec = (pl.BlockSpec((tn, k), lambda j, i: (j, 0)) if features_first
              else pl.BlockSpec((k, tn), lambda j, i: (0, j)))
    if transposed_out:
        out_spec, out_shape = pl.BlockSpec((tn, tm), lambda j, i: (j, i)), (n, m)
    else:
        out_spec, out_shape = pl.BlockSpec((tm, tn), lambda j, i: (i, j)), (m, n)
    return pl.pallas_call(
        functools.partial(_proj_kernel, w_dim=1 if features_first else 0, transposed_out=transposed_out),
        grid=(n // tn, m // tm),
        in_specs=[pl.BlockSpec((tm, k), lambda j, i: (i, 0)), w_spec],
        out_specs=out_spec,
        out_shape=jax.ShapeDtypeStruct(out_shape, out_dtype),
        compiler_params=_cparams("arbitrary", "arbitrary"),
        name=name,
    )(x, w)


def _outln_kernel(og_ref, w_ref, x_ref, g_ref, b_ref, xo_ref, *maybe_xb_ref):
    for c in range(og_ref.shape[0] // LN_ROWS):
        rows = slice(c * LN_ROWS, (c + 1) * LN_ROWS)
        t = DN_ALPHA * x_ref[rows, :] + _dot(og_ref[rows, :], w_ref[...])
        mu = jnp.mean(t, axis=-1, keepdims=True)
        d = t - mu
        var = jnp.mean(d * d, axis=-1, keepdims=True)
        out = d * lax.rsqrt(var + LN_EPS) * g_ref[...] + b_ref[...]
        xo_ref[rows, :] = out
        for xb_ref in maybe_xb_ref:
            xb_ref[rows, :] = out.astype(BF16)


def _out_proj_ln(og, w_out, x, ln_g, ln_b, with_bf16, tm=512):
    m, k = og.shape
    n = w_out.shape[1]
    rows = pl.BlockSpec((tm, n), lambda i: (i, 0))
    out_dtypes = (F32, BF16) if with_bf16 else (F32,)
    return pl.pallas_call(
        _outln_kernel,
        grid=(m // tm,),
        in_specs=[pl.BlockSpec((tm, k), lambda i: (i, 0)),
                  pl.BlockSpec((k, n), lambda i: (0, 0)),
                  rows,
                  pl.BlockSpec((1, n), lambda i: (0, 0)),
                  pl.BlockSpec((1, n), lambda i: (0, 0))],
        out_specs=[rows] * len(out_dtypes),
        out_shape=[jax.ShapeDtypeStruct((m, n), dt) for dt in out_dtypes],
        compiler_params=_cparams("arbitrary"),
        name="out_proj_ln",
    )(og, w_out.astype(BF16), x, ln_g.reshape(1, n), ln_b.reshape(1, n))


def _bucket_np(dist):
    max_exact = REL_BUCKETS // 2
    ratio = np.maximum(dist, max_exact).astype(np.float32) / max_exact
    large = max_exact + (np.log(ratio) / math.log(REL_MAX_DIST / max_exact)
                         * (REL_BUCKETS - max_exact)).astype(np.int32)
    return np.where(dist < max_exact, dist, np.minimum(large, REL_BUCKETS - 1))


def _rel_table(rel_bias, dist, ok):
    bucket = _bucket_np(np.maximum(dist, 0)).reshape(-1)
    onehot_t = np.zeros((REL_BUCKETS, bucket.size), np.float32)
    onehot_t[bucket, np.arange(bucket.size)] = 1.0
    t = jnp.dot(rel_bias.T.astype(F32) * LOG2E, jnp.asarray(onehot_t), precision=lax.Precision.HIGHEST)
    blocked = np.where(ok.reshape(-1), 0.0, NEG_INF).astype(np.float32)
    return (t + blocked[None, :]).reshape((N_HEADS,) + dist.shape)


def _band_tables(rel_bias, nd, window, with_blocked=False):
    key = np.arange(LANES)[:, None]
    qry = np.arange(LANES)[None, :]
    dist = np.stack([LANES * d + qry - key for d in range(nd)])
    ok = dist >= 0
    if window is not None:
        ok &= dist < window
    if with_blocked:
        dist = np.concatenate([dist, np.zeros((1, LANES, LANES), dist.dtype)])
        ok = np.concatenate([ok, np.zeros((1, LANES, LANES), bool)])
    return _rel_table(rel_bias, dist, ok)


def _group_table(tbl_ref, idx):
    return jnp.concatenate([tbl_ref[r, idx] for r in range(HPG)], axis=1)


def _with_ones_rows(vt):
    return jnp.concatenate([vt, jnp.ones((BF16_ROWS, vt.shape[1]), BF16)], axis=0)


def _finish_group(acc):
    o_t = acc[:HEAD_DIM] / jnp.maximum(acc[HEAD_DIM:HEAD_DIM + 1], 1e-30)
    return jnp.concatenate([o_t[:, r * LANES:(r + 1) * LANES].T for r in range(HPG)], axis=1)


def _online_update(x, vt, m, acc):
    m_new = jnp.maximum(m, jnp.max(x, axis=0, keepdims=True))
    p = jnp.exp2(x - m_new)
    acc = jnp.exp2(m - m_new) * acc + _dot(_with_ones_rows(vt), p.astype(BF16))
    return m_new, acc


def _banded_kernel(*refs, nd, has_sink, mixed):
    refs = list(refs)
    qt_ref, k_ref, vt_ref, tbl_ref, z_ref = refs[:5]
    rest = refs[5:]
    sink_ref = rest.pop(0) if has_sink else None
    oc_ref, os_ref, gl_ref = (rest.pop(0), rest.pop(0), rest.pop(0)) if mixed else (None, None, None)
    (o_ref,) = rest
    for u in range(TILES_PER_STEP):
        i = pl.program_id(2) * TILES_PER_STEP + u
        tile = slice(u * LANES, (u + 1) * LANES)
        ks, vts, tbls = [], [], []
        for d in range(nd):
            kb = i - d
            start = pl.multiple_of(jnp.maximum(kb, 0) * LANES, LANES)
            ks.append(k_ref[pl.ds(start, LANES), :])
            vts.append(vt_ref[:, pl.ds(start, LANES)])
            tbls.append(_group_table(tbl_ref, jnp.where(kb >= 0, d, nd)))
        x = _dot(jnp.concatenate(ks, axis=0), _group_queries(qt_ref[:, tile])) + jnp.concatenate(tbls, axis=0)
        m = jnp.max(x, axis=0, keepdims=True)
        if has_sink:
            m = jnp.maximum(m, sink_ref[...])
        p = jnp.exp2(x - m)
        acc = _dot(_with_ones_rows(jnp.concatenate(vts, axis=1)), p.astype(BF16))
        if has_sink:
            rows = lax.broadcasted_iota(jnp.int32, acc.shape, 0)
            acc = acc + jnp.where(rows >= HEAD_DIM, jnp.exp2(sink_ref[...] - m), 0.0)
        o = _finish_group(acc)
        if mixed:
            gates = 1.0 / (1.0 + jnp.exp(-gl_ref[tile, :]))
            oc = oc_ref[tile, :].astype(F32)
            osel = os_ref[tile, :].astype(F32)
            o = jnp.concatenate(
                [gates[:, 3 * r:3 * r + 1] * oc[:, _head(r)] + gates[:, 3 * r + 1:3 * r + 2] * osel[:, _head(r)]
                 + gates[:, 3 * r + 2:3 * r + 3] * o[:, _head(r)] for r in range(HPG)], axis=1)
        o_ref[tile, :] = (o * _silu(z_ref[tile, :].astype(F32))).astype(o_ref.dtype)


def _banded_gqa(h, qvt, batch, seq, kblk, vtblk, zblk, tbl, sinks=None, mix=None, name="banded"):
    step = TILES_PER_STEP * LANES
    nq = seq // step
    nd = tbl.shape[1] - 1
    rows_of_group = lambda blk: pl.BlockSpec((step, GROUP_W), lambda b, g, i: (b * nq + i, blk + g))
    in_specs = [pl.BlockSpec((GROUP_W, step), lambda b, g, i: (g, b * nq + i)),
                pl.BlockSpec((seq, HEAD_DIM), lambda b, g, i: (b, kblk + g)),
                pl.BlockSpec((HEAD_DIM, seq), lambda b, g, i: (vtblk + g, b)),
                pl.BlockSpec((HPG, nd + 1, LANES, LANES), lambda b, g, i: (g, 0, 0, 0)),
                rows_of_group(zblk)]
    args = [qvt, h, qvt, tbl, h]
    if sinks is not None:
        sink_row = jnp.repeat(sinks.astype(F32).reshape(N_KV, HPG) * LOG2E, LANES, axis=1)
        in_specs.append(pl.BlockSpec((None, 1, HPG * LANES), lambda b, g, i: (g, 0, 0)))
        args.append(sink_row.reshape(N_KV, 1, HPG * LANES))
    if mix is not None:
        in_specs += [rows_of_group(0), rows_of_group(0),
                     pl.BlockSpec((step, LANES), lambda b, g, i: (b * nq + i, g))]
        args += list(mix)
    return pl.pallas_call(
        functools.partial(_banded_kernel, nd=nd, has_sink=sinks is not None, mixed=mix is not None),
        grid=(batch, N_KV, nq),
        in_specs=in_specs,
        out_specs=pl.BlockSpec((step, GROUP_W), lambda b, g, i: (b * nq + i, g)),
        out_shape=jax.ShapeDtypeStruct((batch * seq, WIDTH), BF16),
        compiler_params=_cparams("arbitrary", "arbitrary", "arbitrary"),
        name=name,
    )(*args)


def _compress_kernel(kc_ref, vc_ref, pek_ref, w1k_ref, w2k_ref, pev_ref, w1v_ref, w2v_ref,
                     ko_ref, vo_ref, xs_ref, *, nc):
    half = CMP_LEN // 2
    for x_ref, pe_ref, w1_ref, w2_ref, o_ref in ((kc_ref, pek_ref, w1k_ref, w2k_ref, ko_ref),
                                                 (vc_ref, pev_ref, w1v_ref, w2v_ref, vo_ref)):
        xs_ref[...] = x_ref[...].astype(F32)
        first = jnp.zeros((nc, HEAD_DIM), F32)
        second = jnp.zeros((nc, HEAD_DIM), F32)
        for l in range(half):
            xl = xs_ref[pl.ds(l, nc, stride=CMP_STRIDE), :]
            first += _dot((xl + pe_ref[l:l + 1, :]).astype(BF16), w1_ref[l])
            second += _dot((xl + pe_ref[half + l:half + l + 1, :]).astype(BF16), w1_ref[half + l])
        hid = _silu(first + pltpu.roll(second, nc - 1, 0))
        o_ref[...] = _dot(hid.astype(BF16), w2_ref[...]).astype(o_ref.dtype)


def _nsa_compress(h, batch, seq, kcblk, vcblk, pe_k, w1_k, w2_k, pe_v, w1_v, w2_v):
    assert CMP_LEN == 2 * CMP_STRIDE
    nc = seq // CMP_STRIDE
    kv_spec = lambda blk: pl.BlockSpec((seq, HEAD_DIM), lambda b, g: (b, blk + g))
    full = lambda shape: pl.BlockSpec(shape, lambda b, g: (0,) * len(shape))
    out_spec = pl.BlockSpec((None, None, nc, HEAD_DIM), lambda b, g: (b, g, 0, 0))
    out_shape = jax.ShapeDtypeStruct((batch, N_KV, nc, HEAD_DIM), BF16)
    return pl.pallas_call(
        functools.partial(_compress_kernel, nc=nc),
        grid=(batch, N_KV),
        in_specs=[kv_spec(kcblk), kv_spec(vcblk),
                  full((CMP_LEN, HEAD_DIM)), full((CMP_LEN, HEAD_DIM, HEAD_DIM)), full((HEAD_DIM, HEAD_DIM)),
                  full((CMP_LEN, HEAD_DIM)), full((CMP_LEN, HEAD_DIM, HEAD_DIM)), full((HEAD_DIM, HEAD_DIM))],
        out_specs=[out_spec, out_spec],
        out_shape=[out_shape, out_shape],
        scratch_shapes=[pltpu.VMEM((seq, HEAD_DIM), F32)],
        compiler_params=_cparams("arbitrary", "arbitrary"),
        name="nsa_compress",
    )(h, h, pe_k, w1_k.astype(BF16), w2_k.astype(BF16), pe_v, w1_v.astype(BF16), w2_v.astype(BF16))


def _cmp_table(rel_bias, nc):
    u = np.arange(2 * nc)[:, None] - nc
    dist = np.arange(LANES)[None, :] - (u * CMP_STRIDE + CMP_LEN - 1)
    return _rel_table(rel_bias, dist, dist >= 0)


def _inter_t(seq, nc):
    nb = seq // SEL_LEN
    cstart = np.arange(nc) * CMP_STRIDE
    sstart = np.arange(nb) * SEL_LEN
    inter = np.clip(np.minimum(cstart[None, :] + CMP_LEN, sstart[:, None] + SEL_LEN)
                    - np.maximum(cstart[None, :], sstart[:, None]), 0, None) / CMP_LEN
    return inter.astype(np.float32)


def _cmp_select_kernel(qt_ref, kc_ref, vc_ref, tbl_ref, inter_ref, o_ref, sel_ref, *, nb, nc):
    first = pl.program_id(1) * TILES_PER_STEP
    width = TILES_PER_STEP * LANES
    kc = kc_ref[...]
    vc = vc_ref[...]
    per_tile = LANES // CMP_STRIDE
    row0 = [pl.multiple_of(nc - (first + u) * per_tile, per_tile) for u in range(TILES_PER_STEP)]
    outs = []
    p_sum_t = None
    for r in range(HPG):
        tbl = jnp.concatenate([tbl_ref[r, pl.ds(row, nc), :] for row in row0], axis=1)
        st = _dot(kc, qt_ref[r * HEAD_DIM:(r + 1) * HEAD_DIM, :]) + tbl
        pt = jnp.where(tbl > 0.5 * NEG_INF, jnp.exp2(st - jnp.max(st, axis=0, keepdims=True)), 0.0)
        pt = pt / jnp.maximum(jnp.sum(pt, axis=0, keepdims=True), 1e-30)
        p_sum_t = pt if p_sum_t is None else p_sum_t + pt
        outs.append(_dot(pt.T.astype(BF16), vc))
    o_ref[...] = jnp.concatenate(outs, axis=1).astype(o_ref.dtype)

    inter = inter_ref[...]
    imp = sum(_dot(inter, part) for part in _split3(p_sum_t))
    blk = lax.broadcasted_iota(jnp.int32, (nb, width), 0)
    pos = first * LANES + lax.broadcasted_iota(jnp.int32, (nb, width), 1)
    cur = lax.shift_right_logical(pos, int(math.log2(SEL_LEN)))
    allowed = blk * SEL_LEN <= pos
    forced = (blk == 0) | (blk == cur) | (blk == cur - 1)
    imp = jnp.where(allowed, jnp.where(forced, FORCED_SCORE, imp), NEG_INF)
    rank = jnp.zeros((nb, width), F32)
    for c in range(nb):
        row = imp[c:c + 1, :]
        earlier = jnp.where(blk > c, 1.0, 0.0)
        rank += jnp.where(row > imp, 1.0, jnp.where(row == imp, earlier, 0.0))
    sel_t = jnp.where((rank < SEL_TOPK) & (imp > 0.5 * NEG_INF), 1.0, 0.0)
    sel_t = jnp.concatenate([sel_t, jnp.zeros((LANES - nb, width), F32)], axis=0).astype(sel_ref.dtype)
    for u in range(TILES_PER_STEP):
        sel_ref[u] = sel_t[:, u * LANES:(u + 1) * LANES]


def _nsa_cmp_select(qvt, batch, seq, k_cmp, v_cmp, rel_bias):
    step = TILES_PER_STEP * LANES
    nq = seq // step
    nc = seq // CMP_STRIDE
    nb = seq // SEL_LEN
    assert nc == LANES and nb <= LANES
    tbl = _cmp_table(rel_bias, nc)
    inter = jnp.asarray(_inter_t(seq, nc), BF16)
    cmp_spec = pl.BlockSpec((None, None, nc, HEAD_DIM), lambda g, i, b: (b, g, 0, 0))
    return pl.pallas_call(
        functools.partial(_cmp_select_kernel, nb=nb, nc=nc),
        grid=(N_KV, nq, batch),
        in_specs=[pl.BlockSpec((GROUP_W, step), lambda g, i, b: (g, b * nq + i)),
                  cmp_spec, cmp_spec,
                  pl.BlockSpec((HPG, 2 * nc, LANES), lambda g, i, b: (g, 0, 0)),
                  pl.BlockSpec((nb, nc), lambda g, i, b: (0, 0))],
        out_specs=[pl.BlockSpec((step, GROUP_W), lambda g, i, b: (b * nq + i, g)),
                   pl.BlockSpec((None, None, TILES_PER_STEP, LANES, LANES), lambda g, i, b: (b, g, i, 0, 0))],
        out_shape=[jax.ShapeDtypeStruct((batch * seq, WIDTH), BF16),
                   jax.ShapeDtypeStruct((batch, N_KV, seq // LANES, LANES, LANES), BF16)],
        compiler_params=_cparams("arbitrary", "arbitrary", "arbitrary"),
        name="nsa_cmp_select",
    )(qvt, k_cmp, v_cmp, tbl, inter)


SEL_TILES = 2
SEL_KEYS = SEL_TILES * LANES


def _expand_np(seq):
    j = np.arange(LANES)[None, None, :]
    key = np.arange(seq // SEL_KEYS)[:, None, None] * SEL_KEYS + np.arange(SEL_KEYS)[None, :, None]
    return (j == key // SEL_LEN).astype(np.float32)


def _sel_attn_kernel(qt_ref, k_ref, vt_ref, sel_ref, tbl_ref, exp_ref, o_ref):
    last = pl.program_id(2)
    lanes = HPG * LANES
    beyond = tbl_ref.shape[1] - 1
    far = beyond - 1
    tiles = range(SEL_TILES)
    qt_aug = []
    for u in tiles:
        blocked = ((1.0 - sel_ref[u].astype(F32)) * NEG_INF).astype(BF16)
        qt_aug.append(jnp.concatenate([_group_queries(qt_ref[:, u * LANES:(u + 1) * LANES]),
                                       jnp.concatenate([blocked] * HPG, axis=1)], axis=0))

    def scores(u, kt):
        start = pl.multiple_of(kt * SEL_KEYS, SEL_KEYS)
        k_aug = jnp.concatenate([k_ref[pl.ds(start, SEL_KEYS), :], exp_ref[kt]], axis=1)
        return _dot(k_aug, qt_aug[u])

    def body(kt, carry):
        start = pl.multiple_of(kt * SEL_KEYS, SEL_KEYS)
        out = []
        for u in tiles:
            s, m, acc = carry[u]
            s_next = scores(u, jnp.minimum(kt + 1, last))
            behind = [(last - kt) * SEL_TILES + u - c for c in tiles]
            x = s + jnp.concatenate(
                [_group_table(tbl_ref, jnp.where(d >= 0, jnp.minimum(d, far), beyond)) for d in behind], axis=0)
            out.append((s_next,) + _online_update(x, vt_ref[:, pl.ds(start, SEL_KEYS)], m, acc))
        return tuple(out)

    init = tuple((scores(u, 0), jnp.full((1, lanes), M_INIT, F32),
                  jnp.zeros((HEAD_DIM + BF16_ROWS, lanes), F32)) for u in tiles)
    final = lax.fori_loop(0, last + 1, body, init)
    for u in tiles:
        o_ref[u * LANES:(u + 1) * LANES, :] = _finish_group(final[u][2]).astype(o_ref.dtype)


def _nsa_sel_attn(h, qvt, batch, seq, kblk, vtblk, sel, rel_bias):
    nq = seq // SEL_KEYS
    tbl = _band_tables(rel_bias, 3, None, True)
    expand = jnp.asarray(_expand_np(seq), BF16)
    return pl.pallas_call(
        _sel_attn_kernel,
        grid=(batch, N_KV, nq),
        in_specs=[pl.BlockSpec((GROUP_W, SEL_KEYS), lambda b, g, i: (g, b * nq + i)),
                  pl.BlockSpec((seq, HEAD_DIM), lambda b, g, i: (b, kblk + g)),
                  pl.BlockSpec((HEAD_DIM, seq), lambda b, g, i: (vtblk + g, b)),
                  pl.BlockSpec((None, None, SEL_TILES, LANES, LANES), lambda b, g, i: (b, g, i, 0, 0)),
                  pl.BlockSpec((HPG, 4, LANES, LANES), lambda b, g, i: (g, 0, 0, 0)),
                  pl.BlockSpec((seq // SEL_KEYS, SEL_KEYS, LANES), lambda b, g, i: (0, 0, 0))],
        out_specs=pl.BlockSpec((SEL_KEYS, GROUP_W), lambda b, g, i: (b * nq + i, g)),
        out_shape=jax.ShapeDtypeStruct((batch * seq, WIDTH), BF16),
        compiler_params=_cparams("arbitrary", "arbitrary", "arbitrary"),
        name="nsa_sel_attn",
    )(qvt, h, qvt, sel, tbl, expand)


HEADS_PER_STEP = 8
STEP_W = HEADS_PER_STEP * HEAD_DIM


def _pipelined_heads(i, scores, consume, init):
    heads = range(HEADS_PER_STEP)

    def body(step, carry):
        kb = i - step
        return tuple((scores(j, jnp.maximum(kb - 1, 0)),) + consume(j, carry[j][0], kb, *carry[j][1:], step)
                     for j in heads)

    carry = lax.fori_loop(0, i + 1, body, tuple((scores(j, i),) + tuple(init) for j in heads))
    return [c[1:] for c in carry]


SB_HEADS = 8
SB_SPAN = 2
SB_W = SB_HEADS * HEAD_DIM


def _sb_kernel(q_ref, k_ref, v_ref, z_ref, o_ref, *, t):
    i = pl.program_id(2)
    tq = SB_SPAN * t
    qry = lax.broadcasted_iota(jnp.int32, (tq, t), 0)
    key = lax.broadcasted_iota(jnp.int32, (tq, t), 1)
    row = lax.broadcasted_iota(jnp.int32, (t, t), 0)
    col = lax.broadcasted_iota(jnp.int32, (t, t), 1)
    later = jnp.where(row >= col, 1.0, 0.0).astype(BF16)
    heads = range(SB_HEADS)

    def scores(j, kb):
        start = pl.multiple_of(kb * t, t)
        return _dot_nt(q_ref[:, _head(j)], k_ref[pl.ds(start, t), _head(j)])

    def consume(j, y, kb, run, acc, strict=None):
        start = pl.multiple_of(kb * t, t)
        neg = -y
        soft = jnp.log(1.0 + jnp.exp2(jnp.minimum(y, neg))) * LOG2E
        log_keep = jnp.minimum(neg, 0.0) - soft
        if strict is not None:
            log_keep = jnp.where(strict, log_keep, 0.0)
        keep = _dot(log_keep.astype(BF16), later)
        a = jnp.exp2(y + keep + run)
        if strict is not None:
            a = jnp.where(strict, a, 0.0)
        acc = acc + _dot(a.astype(BF16), v_ref[pl.ds(start, t), _head(j)])
        return run + keep[:, 0:1], acc

    top = SB_SPAN * i + SB_SPAN - 1
    state = [(scores(j, top), jnp.zeros((tq, 1), F32), jnp.zeros((tq, HEAD_DIM), F32)) for j in heads]
    for c in reversed(range(SB_SPAN)):
        kb = SB_SPAN * i + c
        strict = key + c * t < qry
        state = [(scores(j, jnp.maximum(kb - 1, 0)),) + consume(j, state[j][0], kb, *state[j][1:], strict)
                 for j in heads]

    def body(step, carry):
        kb = SB_SPAN * i - step
        return tuple((scores(j, jnp.maximum(kb - 1, 0)),) + consume(j, carry[j][0], kb, *carry[j][1:])
                     for j in heads)

    final = lax.fori_loop(1, SB_SPAN * i + 1, body, tuple(state))
    o = jnp.concatenate([acc for _, _, acc in final], axis=1)
    o_ref[...] = (o * _silu(z_ref[...].astype(F32))).astype(o_ref.dtype)


def _sb_call(h, batch, seq, t):
    tq = SB_SPAN * t
    nq = seq // tq
    per_w = WIDTH // SB_W
    rows = lambda part: pl.BlockSpec((tq, SB_W), lambda b, hp, i: (b * nq + i, part * per_w + hp))
    whole = lambda part: pl.BlockSpec((seq, SB_W), lambda b, hp, i: (b, part * per_w + hp))
    return pl.pallas_call(
        functools.partial(_sb_kernel, t=t),
        grid=(batch, per_w, nq),
        in_specs=[rows(0), whole(1), whole(2), rows(3)],
        out_specs=rows(0),
        out_shape=jax.ShapeDtypeStruct((batch * seq, WIDTH), BF16),
        compiler_params=_cparams("arbitrary", "arbitrary", "arbitrary"),
        name="stick_breaking",
    )(h, h, h, h)


def _full_attn_call(kernel, h, qvt, batch, seq, t, extra_specs, extra_args, name):
    nq = seq // t
    per_w = WIDTH // STEP_W
    in_specs = [pl.BlockSpec((STEP_W, t), lambda b, hp, i: (hp, b * nq + i)),
                pl.BlockSpec((seq, STEP_W), lambda b, hp, i: (b, hp)),
                pl.BlockSpec((STEP_W, seq), lambda b, hp, i: (per_w + hp, b)),
                pl.BlockSpec((t, STEP_W), lambda b, hp, i: (b * nq + i, per_w + hp))]
    return pl.pallas_call(
        functools.partial(kernel, t=t),
        grid=(batch, per_w, nq),
        in_specs=in_specs + extra_specs,
        out_specs=pl.BlockSpec((t, STEP_W), lambda b, hp, i: (b * nq + i, hp)),
        out_shape=jax.ShapeDtypeStruct((batch * seq, WIDTH), BF16),
        compiler_params=_cparams("arbitrary", "arbitrary", "arbitrary"),
        name=name,
    )(qvt, h, qvt, h, *extra_args)


KEY_PARTS = 3


def _fox_prep_kernel(fl_ref, bias_ref, ct_ref, ck_ref, *, seq):
    row = lax.broadcasted_iota(jnp.int32, (LANES, LANES), 0)
    col = lax.broadcasted_iota(jnp.int32, (LANES, LANES), 1)
    upto = jnp.where(row <= col, 1.0, 0.0).astype(BF16)
    downto = jnp.where(col <= row, 1.0, 0.0).astype(BF16)
    place = [jnp.where((col == row + j * N_HEADS) & (row < N_HEADS), 1.0, 0.0).astype(BF16)
             for j in range(KEY_PARTS)]
    carry_t = jnp.zeros((LANES, 1), F32)
    carry = jnp.zeros((1, LANES), F32)
    for blk in range(seq // LANES):
        x = fl_ref[blk * LANES:(blk + 1) * LANES, :] + bias_ref[...]
        log_f = jnp.minimum(x, 0.0) - jnp.log1p(jnp.exp(-jnp.abs(x)))
        cs_t = sum(_dot(part, upto) for part in _split3(log_f.T)) + carry_t
        ct_ref[:, blk * LANES:(blk + 1) * LANES] = cs_t * LOG2E
        carry_t = cs_t[:, LANES - 1:LANES]
        cs = sum(_dot(downto, part) for part in _split3(log_f)) + carry
        carry = cs[LANES - 1:LANES, :]
        parts = _split3(-(cs * LOG2E))
        ck_ref[blk * LANES:(blk + 1) * LANES, :] = sum(
            _dot(part, sel) for part, sel in zip(parts, place)).astype(ck_ref.dtype)


def _fox_prep(fl, fgate_bias, batch, seq):
    assert KEY_PARTS * N_HEADS <= LANES
    bias = jnp.zeros((1, LANES), F32).at[0, :N_HEADS].set(fgate_bias.astype(F32))
    return pl.pallas_call(
        functools.partial(_fox_prep_kernel, seq=seq),
        grid=(batch,),
        in_specs=[pl.BlockSpec((seq, LANES), lambda b: (b, 0)),
                  pl.BlockSpec((1, LANES), lambda b: (0, 0))],
        out_specs=[pl.BlockSpec((None, LANES, seq), lambda b: (b, 0, 0)),
                   pl.BlockSpec((seq, LANES), lambda b: (b, 0))],
        out_shape=[jax.ShapeDtypeStruct((batch, LANES, seq), F32),
                   jax.ShapeDtypeStruct((batch * seq, LANES), BF16)],
        compiler_params=_cparams("arbitrary"),
        name="fox_prep",
    )(fl, bias)


def _fox_kernel(qt_ref, k_ref, vt_ref, z_ref, crow_ref, ck_ref, o_ref, *, t):
    first_head = pl.program_id(1) * HEADS_PER_STEP
    i = pl.program_id(2)
    key = lax.broadcasted_iota(jnp.int32, (t, t), 0)
    qry = lax.broadcasted_iota(jnp.int32, (t, t), 1)
    row = lax.broadcasted_iota(jnp.int32, (LANES, t), 0)
    qt_aug = [jnp.concatenate(
        [qt_ref[_head(j), :],
         jnp.where((row % N_HEADS == first_head + j) & (row < KEY_PARTS * N_HEADS), 1.0, 0.0).astype(BF16)], axis=0)
        for j in range(HEADS_PER_STEP)]

    def scores(j, kb):
        start = pl.multiple_of(kb * t, t)
        k_aug = jnp.concatenate([k_ref[pl.ds(start, t), _head(j)], ck_ref[pl.ds(start, t), :]], axis=1)
        return _dot(k_aug, qt_aug[j])

    def consume(j, x, kb, m, acc, tiles_behind):
        start = pl.multiple_of(kb * t, t)
        c_q = crow_ref[pl.ds((first_head + j) % F32_ROWS, 1), pl.ds(pl.multiple_of(i * t, t), t)]
        x = jnp.where(key <= qry + tiles_behind * t, x, NEG_INF)
        m_new = jnp.maximum(m, jnp.max(x, axis=0, keepdims=True) + c_q)
        p = jnp.exp2(x + (c_q - m_new))
        vt = _with_ones_rows(vt_ref[_head(j), pl.ds(start, t)])
        return m_new, jnp.exp2(m - m_new) * acc + _dot(vt, p.astype(BF16))

    final = _pipelined_heads(i, scores, consume,
                             (jnp.full((1, t), M_INIT, F32), jnp.zeros((HEAD_DIM + BF16_ROWS, t), F32)))
    o = jnp.concatenate([(acc[:HEAD_DIM] / jnp.maximum(acc[HEAD_DIM:HEAD_DIM + 1], 1e-30)).T for _, acc in final],
                        axis=1)
    o_ref[...] = (o * _silu(z_ref[...].astype(F32))).astype(o_ref.dtype)


class _Weight:
    def __init__(self, w):
        self.features_first = w.shape[1] % LANES != 0
        self.src = w.T if self.features_first else w

    def pick(self, *ranges, scale_first=None, group_pad=None):
        axis = 0 if self.features_first else 1
        parts = [lax.slice_in_dim(self.src, a, b, axis=axis) for a, b in ranges]
        if scale_first is not None:
            parts[0] = parts[0] * scale_first
        out = jnp.concatenate(parts, axis=axis)
        if group_pad is not None:
            groups, width = group_pad
            out = jnp.moveaxis(out, axis, 0)
            out = out.reshape(groups, -1, out.shape[1])
            out = jnp.pad(out, ((0, 0), (0, width - out.shape[1]), (0, 0))).reshape(groups * width, -1)
            out = jnp.moveaxis(out, 0, axis)
        return out.astype(BF16)

    def project(self, xb, operand, name, out_dtype=BF16, transposed_out=False):
        return _projection(xb, operand, out_dtype, name, features_first=self.features_first,
                           transposed_out=transposed_out)


def _layer_nsa(xb, batch, seq, rel_bias, w_in, cmp_pe_k, cmp_w1_k, cmp_w2_k, cmp_pe_v, cmp_w1_v, cmp_w2_v):
    kc0, ks0, vs0, kw0, vw0, gate0 = (WIDTH + j * KV_WIDTH for j in (0, 2, 3, 4, 5, 6))
    gate1 = gate0 + 3 * N_HEADS
    w = _Weight(w_in)
    h = w.project(xb, w.pick((kc0, vs0), (kw0, vw0), (gate1, w_in.shape[1])), "proj_nsa")
    qvt = w.project(xb, w.pick((0, WIDTH), (vs0, kw0), (vw0, gate0), scale_first=QSCALE), "proj_nsa_t",
                    transposed_out=True)
    gl = w.project(xb, w.pick((gate0, gate1), group_pad=(N_KV, LANES)), "proj_nsa_gates", out_dtype=F32)
    blk = lambda j: j * N_KV
    vrow = WIDTH // HEAD_DIM
    k_cmp, v_cmp = _nsa_compress(h, batch, seq, blk(0), blk(1), cmp_pe_k, cmp_w1_k, cmp_w2_k,
                                 cmp_pe_v, cmp_w1_v, cmp_w2_v)
    o_cmp, sel = _nsa_cmp_select(qvt, batch, seq, k_cmp, v_cmp, rel_bias)
    o_sel = _nsa_sel_attn(h, qvt, batch, seq, blk(2), vrow, sel, rel_bias)
    nd = NSA_WINDOW // LANES + 1
    return _banded_gqa(h, qvt, batch, seq, blk(3), vrow + N_KV, 4 * KV_WIDTH // GROUP_W,
                       _band_tables(rel_bias, nd, NSA_WINDOW, True), mix=(o_cmp, o_sel, gl), name="nsa_window_mix")


def _layer_swa(xb, batch, seq, rel_bias, w_in, sinks):
    v0, z0 = WIDTH + KV_WIDTH, WIDTH + 2 * KV_WIDTH
    w = _Weight(w_in)
    h = w.project(xb, w.pick((WIDTH, v0), (z0, w_in.shape[1])), "proj_swa")
    qvt = w.project(xb, w.pick((0, WIDTH), (v0, z0), scale_first=QSCALE), "proj_swa_t",
                    transposed_out=True)
    nd = SWA_WINDOW // LANES + 1
    return _banded_gqa(h, qvt, batch, seq, 0, WIDTH // HEAD_DIM, KV_WIDTH // GROUP_W,
                       _band_tables(rel_bias, nd, SWA_WINDOW, True), sinks=sinks, name="swa")


def _layer_sb(xb, batch, seq, w_in, t=256):
    w = _Weight(w_in)
    h = w.project(xb, w.pick((0, WIDTH), (WIDTH, w_in.shape[1]), scale_first=QSCALE), "proj_sb")
    return _sb_call(h, batch, seq, t)


def _layer_fox(xb, batch, seq, w_in, fgate_bias, t=256):
    f0 = 3 * WIDTH
    f1 = f0 + N_HEADS
    w = _Weight(w_in)
    h = w.project(xb, w.pick((WIDTH, 2 * WIDTH), (f1, w_in.shape[1])), "proj_fox")
    qvt = w.project(xb, w.pick((0, WIDTH), (2 * WIDTH, f0), scale_first=QSCALE), "proj_fox_t",
                    transposed_out=True)
    fl = w.project(xb, w.pick((f0, f1), group_pad=(1, LANES)), "proj_fox_gates", out_dtype=F32)
    ct, ck = _fox_prep(fl, fgate_bias, batch, seq)
    assert F32_ROWS % HEADS_PER_STEP == 0
    extra_specs = [pl.BlockSpec((None, F32_ROWS, seq), lambda b, hp, i: (b, hp * HEADS_PER_STEP // F32_ROWS, 0)),
                   pl.BlockSpec((seq, LANES), lambda b, hp, i: (b, 0))]
    return _full_attn_call(_fox_kernel, h, qvt, batch, seq, t, extra_specs, [ct, ck], "forgetting")


def kernel(x, rel_bias, w_in_a, w_out_a, ln_g_a, ln_b_a, cmp_pe_k, cmp_w1_k, cmp_w2_k, cmp_pe_v, cmp_w1_v, cmp_w2_v, w_in_b, w_out_b, ln_g_b, ln_b_b, sinks_b, w_in_c, w_out_c, ln_g_c, ln_b_c, w_in_d, w_out_d, ln_g_d, ln_b_d, fgate_bias_d):
    batch, seq, d_model = x.shape
    xf = x.reshape(batch * seq, d_model)
    xb = xf.astype(BF16)
    og = _layer_nsa(xb, batch, seq, rel_bias, w_in_a, cmp_pe_k, cmp_w1_k, cmp_w2_k, cmp_pe_v, cmp_w1_v, cmp_w2_v)
    xf, xb = _out_proj_ln(og, w_out_a, xf, ln_g_a, ln_b_a, True)
    og = _layer_swa(xb, batch, seq, rel_bias, w_in_b, sinks_b)
    xf, xb = _out_proj_ln(og, w_out_b, xf, ln_g_b, ln_b_b, True)
    og = _layer_sb(xb, batch, seq, w_in_c)
    xf, xb = _out_proj_ln(og, w_out_c, xf, ln_g_c, ln_b_c, True)
    og = _layer_fox(xb, batch, seq, w_in_d, fgate_bias_d)
    (xf,) = _out_proj_ln(og, w_out_d, xf, ln_g_d, ln_b_d, False)
    return xf.reshape(batch, seq, d_model)
```
